```python
import math
import jax, jax.numpy as jnp
from jax import lax
import numpy as np

D_MODEL = 1024
BATCH = 8
SEQ = 2048
DEPTH = 2
DEC_BATCH = 128
DEC_SEQ = 4
PAST_LEN = 16384
PAGE_SIZE = 128

HEAD_DIM = 128
MIX_WIDTH = D_MODEL // 2
N_HEADS = MIX_WIDTH // HEAD_DIM
N_BRANCH = 3
CONV_W = 4
D_FF = 4 * D_MODEL
CHUNK = 64
EPS = 1e-6

_SPLIT_SIZES = [MIX_WIDTH] * 4 + [N_HEADS] * 2 \
    + [MIX_WIDTH] * 4 + [N_HEADS] * 2 \
    + [MIX_WIDTH] * 4 \
    + [D_MODEL] * N_BRANCH
IN_COLS = sum(_SPLIT_SIZES)

kernel_name = "hybrid_mlstm_gdn_hgrn2_decode_step"


def rmsnorm(x, w):
    xf = x.astype(jnp.float32)
    y = xf * lax.rsqrt(jnp.mean(xf * xf, -1, keepdims=True) + EPS)
    return (y * w.astype(jnp.float32)).astype(x.dtype)


def head_rms(x):
    return x * lax.rsqrt(jnp.mean(x * x, -1, keepdims=True) + EPS)


def l2n(x):
    return x * lax.rsqrt(jnp.sum(x * x, -1, keepdims=True) + EPS)


def heads(x):
    b, t, _ = x.shape
    return x.reshape(b, t, N_HEADS, HEAD_DIM).transpose(0, 2, 1, 3)


def unheads(x):
    b, h, t, d = x.shape
    return x.transpose(0, 2, 1, 3).reshape(b, t, h * d)


def to_chunks(x, c):
    b, h, t = x.shape[:3]
    x = x.reshape(b, h, t // c, c, *x.shape[3:])
    return jnp.moveaxis(x, 2, 0)


def from_chunks(y):
    y = jnp.moveaxis(y, 0, 2)
    return y.reshape(y.shape[0], y.shape[1], -1, *y.shape[4:])


def split_cols(proj):
    idx = [int(i) for i in np.cumsum(_SPLIT_SIZES)[:-1]]
    return jnp.split(proj, idx, axis=-1)


def mlstm_scan(q, k, v, i_log, f_log, C0, n0, m0):
    t_len = q.shape[2]
    c = math.gcd(t_len, CHUNK)
    causal = jnp.tril(jnp.ones((c, c), bool))

    def step(carry, inp):
        C, n, m = carry
        qc, kc, vc, ic, fc = inp
        b = jnp.cumsum(fc, -1)
        a = b + m[..., None]
        D = b[..., :, None] - b[..., None, :] + ic[..., None, :]
        D = jnp.where(causal, D, -jnp.inf)
        mt = jnp.maximum(a, jnp.max(D, -1))
        P = jnp.einsum('bhtd,bhsd->bhts', qc, kc) * jnp.exp(D - mt[..., None])
        w_st = jnp.exp(a - mt)
        num = jnp.einsum('bhts,bhsv->bhtv', P, vc) + w_st[..., None] * jnp.einsum('bhtk,bhkv->bhtv', qc, C)
        den = jnp.sum(P, -1) + w_st * jnp.einsum('bhtk,bhk->bht', qc, n)
        h = num / jnp.maximum(jnp.abs(den), jnp.exp(-mt))[..., None]
        m_new = mt[..., -1]
        ws = jnp.exp(D[..., -1, :] - m_new[..., None])
        wc = jnp.exp(a[..., -1] - m_new)
        C_new = wc[..., None, None] * C + jnp.einsum('bhs,bhsk,bhsv->bhkv', ws, kc, vc)
        n_new = wc[..., None] * n + jnp.einsum('bhs,bhsk->bhk', ws, kc)
        return (C_new, n_new, m_new), h

    xs = tuple(to_chunks(z, c) for z in (q, k, v, i_log, f_log))
    (C, n, m), h = lax.scan(step, (C0, n0, m0), xs)
    return from_chunks(h), C, n, m


def gdn_scan(q, k, v, beta, logg, S0):
    t_len = q.shape[2]
    dv = v.shape[-1]
    c = math.gcd(t_len, CHUNK)
    incl = jnp.tril(jnp.ones((c, c), bool))
    strict = jnp.tril(jnp.ones((c, c), bool), -1)
    eye = jnp.eye(c, dtype=jnp.float32)

    def step(S, inp):
        qc, kc, vc, bc, gc = inp
        G = jnp.cumsum(gc, -1)
        decay = jnp.exp(jnp.where(incl, G[..., :, None] - G[..., None, :], -jnp.inf))
        kk = jnp.einsum('bhtd,bhsd->bhts', kc, kc)
        A = eye + jnp.where(strict, bc[..., :, None] * kk * decay, 0.0)
        rhs = jnp.concatenate([bc[..., None] * vc, (bc * jnp.exp(G))[..., None] * kc], -1)
        sol = lax.linalg.triangular_solve(A, rhs, left_side=True, lower=True, unit_diagonal=True)
        u, w = sol[..., :dv], sol[..., dv:]
        v_new = u - jnp.einsum('bhtk,bhkv->bhtv', w, S)
        qk = jnp.einsum('bhtd,bhsd->bhts', qc, kc) * decay
        o = jnp.einsum('bhtk,bhkv->bhtv', qc * jnp.exp(G)[..., None], S) \
            + jnp.einsum('bhts,bhsv->bhtv', qk, v_new)
        GL = G[..., -1]
        S_new = jnp.exp(GL)[..., None, None] * S \
            + jnp.einsum('bhsk,bhsv->bhkv', kc * jnp.exp(GL[..., None] - G)[..., None], v_new)
        return S_new, o

    xs = tuple(to_chunks(z, c) for z in (q, k, v, beta, logg))
    S, o = lax.scan(step, S0, xs)
    return from_chunks(o), S


def hgrn_scan(q, k, v, f_log, S0):
    t_len = q.shape[2]
    c = math.gcd(t_len, CHUNK)
    incl = jnp.tril(jnp.ones((c, c), bool))

    def step(S, inp):
        qc, kc, vc, fc = inp
        Bc = jnp.cumsum(fc, 2)
        diff = Bc[:, :, :, None, :] - Bc[:, :, None, :, :]
        diff = jnp.where(incl[..., None], diff, -jnp.inf)
        A = jnp.einsum('bhtk,bhsk,bhtsk->bhts', qc, kc, jnp.exp(diff))
        o = jnp.einsum('bhtk,bhkv->bhtv', qc * jnp.exp(Bc), S) + jnp.einsum('bhts,bhsv->bhtv', A, vc)
        BL = Bc[:, :, -1]
        S_new = jnp.exp(BL)[..., None] * S \
            + jnp.einsum('bhsk,bhsv->bhkv', kc * jnp.exp(BL[:, :, None] - Bc), vc)
        return S_new, o

    xs = tuple(to_chunks(z, c) for z in (q, k, v, f_log))
    S, o = lax.scan(step, S0, xs)
    return from_chunks(o), S


def mixer(xn, p, st):
    f32 = jnp.float32
    t_len = xn.shape[1]
    proj = jnp.einsum('btd,dc->btc', xn, p['w_in']).astype(f32)
    (ml_q, ml_k, ml_v, ml_o, ml_i, ml_f,
     gd_q, gd_k, gd_v, gd_z, gd_b, gd_a,
     hg_q, hg_f, hg_i, hg_g,
     gate_a, gate_b, gate_c) = split_cols(proj)
    C0, n0, m0, S0_gd, conv0, S0_hg = st

    i_log = jnp.swapaxes(ml_i + p['ml_i_bias'].astype(f32), 1, 2)
    f_log = jnp.swapaxes(jax.nn.log_sigmoid(ml_f + p['ml_f_bias'].astype(f32)), 1, 2)
    h, C1, n1, m1 = mlstm_scan(heads(ml_q), heads(ml_k) * HEAD_DIM ** -0.5, heads(ml_v),
                               i_log, f_log, C0.astype(f32), n0.astype(f32), m0.astype(f32))
    y_ml = unheads(head_rms(h)) * p['ml_norm_w'].astype(f32) * jax.nn.sigmoid(ml_o)

    qkv = jnp.concatenate([gd_q, gd_k, gd_v], -1)
    buf = jnp.concatenate([conv0.astype(f32), qkv], 1)
    cw = p['gd_conv_w'].astype(f32)
    conv = sum(buf[:, j:j + t_len] * cw[j] for j in range(CONV_W))
    conv = jax.nn.silu(conv)
    conv1 = buf[:, t_len:]
    cq, ck, cv = jnp.split(conv, 3, axis=-1)
    beta = jnp.swapaxes(jax.nn.sigmoid(gd_b), 1, 2)
    logg = jnp.swapaxes(-jnp.exp(p['gd_A_log'].astype(f32))
                        * jax.nn.softplus(gd_a + p['gd_dt_bias'].astype(f32)), 1, 2)
    o_gd, S1_gd = gdn_scan(l2n(heads(cq)) * HEAD_DIM ** -0.5, l2n(heads(ck)), heads(cv),
                           beta, logg, S0_gd.astype(f32))
    y_gd = unheads(head_rms(o_gd) * p['gd_norm_w'].astype(f32)) * jax.nn.silu(gd_z)

    lb = p['hg_lb'].astype(f32)
    g_f = lb + (1.0 - lb) * jax.nn.sigmoid(hg_f)
    k_hg = (1.0 - lb) * jax.nn.sigmoid(-hg_f)
    o_hg, S1_hg = hgrn_scan(heads(jax.nn.silu(hg_q)), heads(k_hg), heads(hg_i),
                            heads(jnp.log(g_f)), S0_hg.astype(f32))
    y_hg = rmsnorm(unheads(o_hg), p['hg_norm_w']) * jax.nn.silu(hg_g)

    ys = jnp.stack([y_ml, y_gd, y_hg], 2)
    br = jnp.einsum('btnw,nwd->btnd', ys, p['w_branch'].astype(f32))
    gates = jax.nn.sigmoid(jnp.stack([gate_a, gate_b, gate_c], 2))
    merged = jnp.sum(gates * br, 2).astype(xn.dtype)
    out = jnp.einsum('btd,de->bte', merged, p['w_out'])
    return out, (C1, n1, m1, S1_gd, conv1, S1_hg)


def run_group(x, states, params):
    new = [[] for _ in range(len(states))]
    for l in range(DEPTH):
        p = {name: arr[l] for name, arr in params.items()}
        mix, st = mixer(rmsnorm(x, p['norm1_w']), p, tuple(s[l] for s in states))
        x = x + mix
        xn = rmsnorm(x, p['norm2_w'])
        hid = jnp.square(jax.nn.relu(xn @ p['w_up']))
        x = x + hid @ p['w_down']
        for lst, s in zip(new, st):
            lst.append(s)
    return x, tuple(jnp.stack(s) for s in new)


def setup_inputs(seed: int = 0) -> dict:
    key = jax.random.key(seed)
    ks = iter(jax.random.split(key, 32))

    def nrm(shape, s):
        return jax.random.normal(next(ks), shape, jnp.float32) * s

    W, H, Dh, D = MIX_WIDTH, N_HEADS, HEAD_DIM, D_MODEL
    dt = jnp.exp(jax.random.uniform(next(ks), (DEPTH, H), jnp.float32, math.log(1e-3), math.log(1e-1)))
    inp = {}
    inp['x_prompt'] = nrm((BATCH, SEQ, D), 1.0)
    inp['x_sample'] = nrm((DEC_BATCH, DEC_SEQ, D), 1.0)
    inp['state_mlstm_C'] = nrm((DEPTH, DEC_BATCH, H, Dh, Dh), 0.3)
    inp['state_mlstm_n'] = nrm((DEPTH, DEC_BATCH, H, Dh), 0.3)
    inp['state_mlstm_m'] = nrm((DEPTH, DEC_BATCH, H), 1.0)
    inp['state_gdn_S'] = nrm((DEPTH, DEC_BATCH, H, Dh, Dh), 0.3)
    inp['state_gdn_conv'] = nrm((DEPTH, DEC_BATCH, CONV_W - 1, 3 * W), 1.0)
    inp['state_hgrn_S'] = nrm((DEPTH, DEC_BATCH, H, Dh, Dh), 0.5)
    inp['norm1_w'] = 1.0 + nrm((DEPTH, D), 0.1)
    inp['w_in'] = nrm((DEPTH, D, IN_COLS), D ** -0.5)
    inp['ml_i_bias'] = nrm((DEPTH, H), 0.1)
    inp['ml_f_bias'] = jnp.linspace(3.0, 6.0, H, dtype=jnp.float32) + nrm((DEPTH, H), 0.1)
    inp['ml_norm_w'] = 1.0 + nrm((DEPTH, W), 0.1)
    inp['gd_conv_w'] = nrm((DEPTH, CONV_W, 3 * W), 0.5)
    inp['gd_A_log'] = jnp.log(jax.random.uniform(next(ks), (DEPTH, H), jnp.float32, 1.0, 16.0))
    inp['gd_dt_bias'] = dt + jnp.log(-jnp.expm1(-dt))
    inp['gd_norm_w'] = 1.0 + nrm((DEPTH, Dh), 0.1)
    inp['hg_lb_logits'] = nrm((DEPTH, W), 0.5)
    inp['hg_norm_w'] = 1.0 + nrm((DEPTH, W), 0.1)
    inp['w_branch'] = nrm((DEPTH, N_BRANCH, W, D), W ** -0.5)
    inp['w_out'] = nrm((DEPTH, D, D), D ** -0.5)
    inp['norm2_w'] = 1.0 + nrm((DEPTH, D), 0.1)
    inp['w_up'] = nrm((DEPTH, D, D_FF), D ** -0.5)
    inp['w_down'] = nrm((DEPTH, D_FF, D), 0.5 * D_FF ** -0.5)
    inp['final_norm_w'] = 1.0 + nrm((D,), 0.1)
    return inp


def reference(x_prompt, x_sample, state_mlstm_C, state_mlstm_n, state_mlstm_m, state_gdn_S,
              state_gdn_conv, state_hgrn_S, norm1_w, w_in, ml_i_bias, ml_f_bias, ml_norm_w,
              gd_conv_w, gd_A_log, gd_dt_bias, gd_norm_w, hg_lb_logits, hg_norm_w, w_branch,
              w_out, norm2_w, w_up, w_down, final_norm_w):
    f32 = jnp.float32
    sm = jax.nn.softmax(hg_lb_logits.astype(f32), axis=0)
    hg_lb = jnp.cumsum(sm, axis=0) - sm[0:1]
    params = dict(norm1_w=norm1_w, w_in=w_in, ml_i_bias=ml_i_bias, ml_f_bias=ml_f_bias,
                  ml_norm_w=ml_norm_w, gd_conv_w=gd_conv_w, gd_A_log=gd_A_log,
                  gd_dt_bias=gd_dt_bias, gd_norm_w=gd_norm_w, hg_lb=hg_lb, hg_norm_w=hg_norm_w,
                  w_branch=w_branch, w_out=w_out, norm2_w=norm2_w, w_up=w_up, w_down=w_down)

    bp = x_prompt.shape[0]
    zero_states = (jnp.zeros((DEPTH, bp, N_HEADS, HEAD_DIM, HEAD_DIM), f32),
                   jnp.zeros((DEPTH, bp, N_HEADS, HEAD_DIM), f32),
                   jnp.zeros((DEPTH, bp, N_HEADS), f32),
                   jnp.zeros((DEPTH, bp, N_HEADS, HEAD_DIM, HEAD_DIM), f32),
                   jnp.zeros((DEPTH, bp, CONV_W - 1, 3 * MIX_WIDTH), f32),
                   jnp.zeros((DEPTH, bp, N_HEADS, HEAD_DIM, HEAD_DIM), f32))
    hp, (p_mC, p_mn, p_mm, p_gS, p_gc, p_hS) = run_group(x_prompt, zero_states, params)
    y_prompt = rmsnorm(hp, final_norm_w)

    past = (state_mlstm_C, state_mlstm_n, state_mlstm_m, state_gdn_S, state_gdn_conv, state_hgrn_S)
    hs, (s_mC, s_mn, s_mm, s_gS, s_gc, s_hS) = run_group(x_sample, past, params)
    y_sample = rmsnorm(hs, final_norm_w)

    return (y_prompt, y_sample, p_mC, p_mn, p_mm, p_gS, p_gc, p_hS,
            s_mC, s_mn, s_mm, s_gS, s_gc, s_hS)
```

```python
import functools
import math

import jax
import jax.numpy as jnp
from jax import lax
from jax.experimental import pallas as pl
from jax.experimental.pallas import tpu as pltpu

F32 = jnp.float32
BF16 = jnp.bfloat16

HEAD_DIM = 128
N_HEADS = 4
MIX_WIDTH = N_HEADS * HEAD_DIM
N_BRANCH = 3
CONV_W = 4
EPS = 1e-6
CHUNK = 64
SUBLANES = 8
BF16_ROWS = 16
SMALL_COLS = 128
NEG = -1e30
VMEM_LIMIT_BYTES = 56 * 1024 * 1024

SEG_ML_Q, SEG_ML_K, SEG_ML_V, SEG_ML_O = 0, 1, 2, 3
SEG_GD_Q, SEG_GD_K, SEG_GD_V, SEG_GD_Z = 4, 5, 6, 7
SEG_HG_Q, SEG_HG_F, SEG_HG_I, SEG_HG_G = 8, 9, 10, 11
N_MIX_SEGS = 12
SM_I, SM_F, SM_B, SM_A = 0, N_HEADS, 2 * N_HEADS, 3 * N_HEADS


def _dot(a, b):
    return jnp.dot(a.astype(BF16), b.astype(BF16), preferred_element_type=F32)


def _dot_nt(a, b):
    return lax.dot_general(a.astype(BF16), b.astype(BF16), (((1,), (1,)), ((), ())),
                           preferred_element_type=F32)


def _dot_tn(a, b):
    return _dot(a.T, b)


def _split2(a):
    hi = a.astype(BF16)
    lo = (a - hi.astype(F32)).astype(BF16)
    return hi, lo


def _split3(a):
    hi = a.astype(BF16)
    r = a - hi.astype(F32)
    mid = r.astype(BF16)
    lo = (r - mid.astype(F32)).astype(BF16)
    return hi, mid, lo


def _dot_hi(a, b):
    ah, al = _split2(a)
    bh, bl = _split2(b)
    return (jnp.dot(ah, bh, preferred_element_type=F32) + jnp.dot(ah, bl, preferred_element_type=F32)
            + jnp.dot(al, bh, preferred_element_type=F32))


def _dot_exact_lhs(a01, b):
    a = a01.astype(BF16)
    return sum(jnp.dot(a, part, preferred_element_type=F32) for part in _split3(b))


def _dot_exact_rhs(a, b01):
    b = b01.astype(BF16)
    return sum(jnp.dot(part, b, preferred_element_type=F32) for part in _split3(a))


def _sigmoid(x):
    return 1.0 / (1.0 + jnp.exp(-x))


def _silu(x):
    return x * _sigmoid(x)


def _softplus(x):
    return jnp.maximum(x, 0.0) + jnp.log(1.0 + jnp.exp(-jnp.abs(x)))


def _log_sigmoid(x):
    return -_softplus(-x)


def _rms(x, axis=-1):
    return x * lax.rsqrt(jnp.mean(x * x, axis=axis, keepdims=True) + EPS)


def _inv_unit_lower(n):
    c = n.shape[0]
    eye = (lax.broadcasted_iota(jnp.int32, (c, c), 0) == lax.broadcasted_iota(jnp.int32, (c, c), 1)).astype(F32)
    acc = eye - n
    power = n
    span = 2
    while span < c:
        power = _dot_hi(power, power)
        acc = _dot_hi(acc, eye + power)
        span *= 2
    return acc


def _in_proj_kernel(x_ref, nw_ref, wbig_ref, wsmall_ref, big_ref, small_ref, xn_ref):
    j = pl.program_id(1)

    @pl.when(j == 0)
    def _():
        xn = _rms(x_ref[...]) * nw_ref[...]
        xn_ref[...] = xn.astype(BF16)
        small_ref[...] = jnp.dot(xn_ref[...], wsmall_ref[...], preferred_element_type=F32)

    big_ref[...] = jnp.dot(xn_ref[...], wbig_ref[...], preferred_element_type=F32)


def _in_proj(x2d, norm_w, w_big, w_small, *, tm, tn):
    m, d = x2d.shape
    n_big = w_big.shape[1]
    assert m % tm == 0 and n_big % tn == 0
    return pl.pallas_call(
        _in_proj_kernel,
        grid=(m // tm, n_big // tn),
        in_specs=[
            pl.BlockSpec((tm, d), lambda i, j: (i, 0)),
            pl.BlockSpec((1, d), lambda i, j: (0, 0)),
            pl.BlockSpec((d, tn), lambda i, j: (0, j)),
            pl.BlockSpec((d, SMALL_COLS), lambda i, j: (0, 0)),
        ],
        out_specs=[
            pl.BlockSpec((tm, tn), lambda i, j: (i, j)),
            pl.BlockSpec((tm, SMALL_COLS), lambda i, j: (i, 0)),
        ],
        out_shape=[
            jax.ShapeDtypeStruct((m, n_big), F32),
            jax.ShapeDtypeStruct((m, SMALL_COLS), F32),
        ],
        scratch_shapes=[pltpu.VMEM((tm, d), BF16)],
        compiler_params=pltpu.CompilerParams(
            dimension_semantics=("parallel", "arbitrary"), vmem_limit_bytes=VMEM_LIMIT_BYTES),
        name="in_proj",
    )(x2d, norm_w, w_big, w_small)


def _mlstm_chunk(q, k, v, i_col, b_col, i_row, b_row, causal, c_state, n_state, m_state):
    c = q.shape[0]
    a_col = b_col + m_state
    d = jnp.where(causal, b_col - b_row + i_row, NEG)
    mt = jnp.maximum(a_col, jnp.max(d, axis=1, keepdims=True))
    p = _dot_nt(q, k) * jnp.exp(d - mt)
    w_st = jnp.exp(a_col - mt)
    num = _dot(p, v) + w_st * _dot(q, c_state)
    den = jnp.sum(p, axis=1, keepdims=True) + w_st * jnp.sum(q * n_state, axis=1, keepdims=True)
    h = num / jnp.maximum(jnp.abs(den), jnp.exp(-mt))
    m_new = mt[c - 1:c, :]
    b_last = b_col[c - 1:c, :]
    ws_col = jnp.exp(b_last - b_col + i_col - m_new)
    wc = jnp.exp(b_last + m_state - m_new)
    kw = k * ws_col
    c_new = wc * c_state + _dot_tn(kw, v)
    n_new = wc * n_state + jnp.sum(kw, axis=0, keepdims=True)
    return h, c_new, n_new, m_new


def _gdn_chunk(q, k, v, beta_col, g_col, g_row, incl, strict, s_state):
    c = q.shape[0]
    dh = v.shape[1]
    decay = jnp.exp(jnp.where(incl, g_col - g_row, NEG))
    qk_kk = _dot_nt(jnp.concatenate([q, k], axis=0), k)
    qk = qk_kk[:c]
    kk = qk_kk[c:]
    n = jnp.where(strict, beta_col * kk * decay, 0.0)
    a_inv = _inv_unit_lower(n)
    eg_col = jnp.exp(g_col)
    rhs = jnp.concatenate([beta_col * v, (beta_col * eg_col) * k], axis=1)
    sol = _dot_hi(a_inv, rhs)
    u = sol[:, :dh]
    w = sol[:, dh:]
    wq_s = _dot(jnp.concatenate([w, q * eg_col], axis=0), s_state)
    v_new = u - wq_s[:c]
    o = wq_s[c:] + _dot(qk * decay, v_new)
    g_last = g_col[c - 1:c, :]
    s_new = jnp.exp(g_last) * s_state + _dot_tn(k * jnp.exp(g_last - g_col), v_new)
    return o, s_new


def _row_block_bcast(x, block, row):
    c, l = x.shape
    x3 = x.reshape(c // block, block, l)
    return jnp.broadcast_to(x3[:, row:row + 1, :], x3.shape).reshape(c, l)


def _hgrn_levels(c):
    return [hb for hb in (8, 16, 32, 64, 128) if 2 * hb <= c]


def _hgrn_level_masks(c):
    t = lax.broadcasted_iota(jnp.int32, (c, c), 0)
    s = lax.broadcasted_iota(jnp.int32, (c, c), 1)
    masks = []
    for hb in _hgrn_levels(c):
        same = (t // (2 * hb)) == (s // (2 * hb))
        masks.append(same & ((t % (2 * hb)) >= hb) & ((s % (2 * hb)) < hb))
    return masks


def _hgrn_chunk(q, k, v, f_log, ltri, level_masks, s_state):
    c, dh = q.shape
    bc = _dot_exact_lhs(ltri, f_log)
    o = _dot(q * jnp.exp(bc), s_state)

    a = jnp.zeros((c, c), F32)
    for hb, mask in zip(_hgrn_levels(c), level_masks):
        mid = _row_block_bcast(bc, 2 * hb, hb - 1)
        ql = q * jnp.exp(jnp.minimum(bc - mid, 0.0))
        kl = k * jnp.exp(jnp.minimum(mid - bc, 0.0))
        a = a + jnp.where(mask, _dot_nt(ql, kl), 0.0)
    if _hgrn_levels(c):
        o = o + _dot(a, v)

    nb = c // SUBLANES
    q3 = q.reshape(nb, SUBLANES, dh)
    k3 = k.reshape(nb, SUBLANES, dh)
    v3 = v.reshape(nb, SUBLANES, dh)
    b3 = bc.reshape(nb, SUBLANES, dh)
    row = lax.broadcasted_iota(jnp.int32, (nb, SUBLANES, 1), 1)
    od = jnp.zeros((nb, SUBLANES, dh), F32)
    for s in range(SUBLANES):
        e = q3 * k3[:, s:s + 1, :] * jnp.exp(jnp.minimum(b3 - b3[:, s:s + 1, :], 0.0))
        col = jnp.sum(e, axis=2, keepdims=True)
        od = od + jnp.where(row >= s, col, 0.0) * v3[:, s:s + 1, :]
    o = o + od.reshape(c, dh)

    b_last = bc[c - 1:c, :]
    decay_col = jnp.broadcast_to(jnp.exp(b_last), (SUBLANES, dh)).T[:, 0:1]
    s_new = decay_col * s_state + _dot_tn(k * jnp.exp(b_last - bc), v)
    return o, s_new


def _mixers_kernel(*refs, layer, n_valid, has_state):
    if has_state:
        (big_ref, small_ref, bias_ref, alog_ref, mlnw_ref, convw_ref, gdnw_ref, lbl_ref, hgnw_ref,
         c0_ref, n0_ref, m0_ref, sg0_ref, cv0_ref, sh0_ref,
         ys_ref, c_ref, n_ref, m_ref, sg_ref, cv_ref, sh_ref, convbuf) = refs
    else:
        (big_ref, small_ref, bias_ref, alog_ref, mlnw_ref, convw_ref, gdnw_ref, lbl_ref, hgnw_ref,
         ys_ref, c_ref, n_ref, m_ref, sg_ref, cv_ref, sh_ref, convbuf) = refs
    j = pl.program_id(1)
    nj = pl.num_programs(1)
    c = big_ref.shape[0]
    w = MIX_WIDTH
    dh = HEAD_DIM
    tail = CONV_W - 1

    @pl.when(j == 0)
    def _():
        if has_state:
            c_ref[...] = c0_ref[...]
            n_ref[...] = n0_ref[...]
            m_ref[...] = m0_ref[...]
            sg_ref[...] = sg0_ref[...]
            sh_ref[...] = sh0_ref[...]
            convbuf[SUBLANES - tail:SUBLANES, :] = cv0_ref[...]
        else:
            c_ref[...] = jnp.zeros_like(c_ref)
            n_ref[...] = jnp.zeros_like(n_ref)
            m_ref[...] = jnp.zeros_like(m_ref)
            sg_ref[...] = jnp.zeros_like(sg_ref)
            sh_ref[...] = jnp.zeros_like(sh_ref)
            convbuf[SUBLANES - tail:SUBLANES, :] = jnp.zeros((tail, N_BRANCH * w), F32)

    def seg(i, h=None):
        if h is None:
            return big_ref[:, i * w:(i + 1) * w]
        return big_ref[:, i * w + h * dh:i * w + (h + 1) * dh]

    lane = lax.broadcasted_iota(jnp.int32, (c, SMALL_COLS), 1)
    rowi = lax.broadcasted_iota(jnp.int32, (c, SMALL_COLS), 0)
    valid = rowi < n_valid
    pre = small_ref[...] + bias_ref[...]
    is_f = (lane >= SM_F) & (lane < SM_B)
    is_b = (lane >= SM_B) & (lane < SM_A)
    is_a = (lane >= SM_A) & (lane < SM_A + N_HEADS)
    logg = -jnp.exp(alog_ref[...]) * _softplus(pre)
    gates = jnp.where(is_f, _log_sigmoid(pre), jnp.where(is_b, _sigmoid(pre), jnp.where(is_a, logg, pre)))
    gates = jnp.where(valid, gates, jnp.where(lane < SM_F, NEG, 0.0))
    tt = lax.broadcasted_iota(jnp.int32, (c, c), 0)
    ss = lax.broadcasted_iota(jnp.int32, (c, c), 1)
    incl = tt >= ss
    strict = tt > ss
    ltri = incl.astype(F32)
    cum_src = jnp.where(is_f | is_a, gates, 0.0)
    cums = _dot_exact_lhs(ltri, cum_src)
    gates_t = gates.T
    cums_t = _dot_exact_rhs(cum_src.T, (ss >= tt).astype(F32))

    convbuf[SUBLANES:SUBLANES + c, :] = big_ref[:, SEG_GD_Q * w:(SEG_GD_V + 1) * w]
    conv = convw_ref[CONV_W - 1:CONV_W, :] * convbuf[SUBLANES:SUBLANES + c, :]
    for jj in range(CONV_W - 1):
        off = SUBLANES - tail + jj
        conv = conv + convw_ref[jj:jj + 1, :] * convbuf[off:off + c, :]
    conv = _silu(conv)
    new_tail = convbuf[SUBLANES + n_valid - tail:SUBLANES + n_valid, :]
    convbuf[SUBLANES - tail:SUBLANES, :] = new_tail

    lbl = lbl_ref[...]
    sm = jnp.exp(lbl - jnp.max(lbl, axis=0, keepdims=True))
    sm = sm / jnp.sum(sm, axis=0, keepdims=True)
    lb = jnp.zeros((1, w), F32)
    for l in range(1, layer + 1):
        lb = lb + sm[l:l + 1, :]

    level_masks = _hgrn_level_masks(c)
    o_hg = []
    for h in range(N_HEADS):
        hs = slice(h * dh, (h + 1) * dh)
        i_col = gates[:, SM_I + h:SM_I + h + 1]
        b_col = cums[:, SM_F + h:SM_F + h + 1]
        i_row = gates_t[SM_I + h:SM_I + h + 1, :]
        b_row = cums_t[SM_F + h:SM_F + h + 1, :]
        hh, c_new, n_new, m_new = _mlstm_chunk(
            seg(SEG_ML_Q, h), seg(SEG_ML_K, h) * (dh ** -0.5), seg(SEG_ML_V, h),
            i_col, b_col, i_row, b_row, incl, c_ref[h], n_ref[h:h + 1, :], m_ref[:, h:h + 1])
        c_ref[h] = c_new
        n_ref[h:h + 1, :] = n_new
        m_lane = lax.broadcasted_iota(jnp.int32, m_ref.shape, 1)
        m_ref[...] = jnp.where(m_lane == h, m_new, m_ref[...])
        y_ml = _rms(hh) * mlnw_ref[:, hs] * _sigmoid(seg(SEG_ML_O, h))
        ys_ref[:, hs] = y_ml.astype(ys_ref.dtype)

        cq = conv[:, hs]
        ck = conv[:, w + h * dh:w + (h + 1) * dh]
        cv = conv[:, 2 * w + h * dh:2 * w + (h + 1) * dh]
        qn = cq * lax.rsqrt(jnp.sum(cq * cq, axis=1, keepdims=True) + EPS) * (dh ** -0.5)
        kn = ck * lax.rsqrt(jnp.sum(ck * ck, axis=1, keepdims=True) + EPS)
        beta_col = gates[:, SM_B + h:SM_B + h + 1]
        g_col = cums[:, SM_A + h:SM_A + h + 1]
        g_row = cums_t[SM_A + h:SM_A + h + 1, :]
        o_gd, sg_new = _gdn_chunk(qn, kn, cv, beta_col, g_col, g_row, incl, strict, sg_ref[h])
        sg_ref[h] = sg_new
        y_gd = _rms(o_gd) * gdnw_ref[...] * _silu(seg(SEG_GD_Z, h))
        ys_ref[:, w + h * dh:w + (h + 1) * dh] = y_gd.astype(ys_ref.dtype)

        hf = seg(SEG_HG_F, h)
        lb_h = lb[:, hs]
        g_f = lb_h + (1.0 - lb_h) * _sigmoid(hf)
        k_hg = jnp.where(valid[:, :dh], (1.0 - lb_h) * _sigmoid(-hf), 0.0)
        f_log = jnp.where(valid[:, :dh], jnp.log(g_f), 0.0)
        o_h, sh_new = _hgrn_chunk(_silu(seg(SEG_HG_Q, h)), k_hg, seg(SEG_HG_I, h), f_log,
                                  ltri, level_masks, sh_ref[h])
        sh_ref[h] = sh_new
        o_hg.append(o_h)

    o_all = jnp.concatenate(o_hg, axis=1)
    y_hg = _rms(o_all) * hgnw_ref[...] * _silu(seg(SEG_HG_G))
    ys_ref[:, 2 * w:3 * w] = y_hg.astype(ys_ref.dtype)

    @pl.when(j == nj - 1)
    def _():
        cv_ref[...] = convbuf[SUBLANES - tail:SUBLANES, :]


def _mixers(big3, small3, params, states_in, states_prev, *, layer, depth, chunk, n_valid):
    b, t, n_big = big3.shape
    assert t % chunk == 0 and chunk % SUBLANES == 0 and CONV_W - 1 <= n_valid <= chunk
    assert n_valid == chunk or t == chunk
    w, dh, nh = MIX_WIDTH, HEAD_DIM, N_HEADS
    has_state = states_in is not None
    bias_row, alog_row, ml_norm_w, conv_w, gd_norm_w, lb_logits, hg_norm_w = params

    def const(shape):
        return pl.BlockSpec(shape, lambda i, j: (0,) * len(shape))

    def st(shape):
        return pl.BlockSpec((None, None) + shape, lambda i, j: (layer, i) + (0,) * len(shape))

    state_shapes = [(nh, dh, dh), (nh, dh), (1, SMALL_COLS), (nh, dh, dh), (CONV_W - 1, N_BRANCH * w), (nh, dh, dh)]
    in_specs = [
        pl.BlockSpec((None, chunk, N_MIX_SEGS * w), lambda i, j: (i, j, 0)),
        pl.BlockSpec((None, chunk, SMALL_COLS), lambda i, j: (i, j, 0)),
        const((1, SMALL_COLS)), const((1, SMALL_COLS)), const((1, w)), const((CONV_W, N_BRANCH * w)),
        const((1, dh)), const((depth, w)), const((1, w)),
    ]
    args = [big3, small3, bias_row, alog_row, ml_norm_w, conv_w, gd_norm_w, lb_logits, hg_norm_w]
    if has_state:
        in_specs += [st(s) for s in state_shapes]
        args += list(states_in)
    aliases = {}
    if states_prev is not None:
        for k_out, arr in enumerate(states_prev):
            aliases[len(args)] = 1 + k_out
            in_specs.append(pl.BlockSpec(memory_space=pl.ANY))
            args.append(arr)
    out_specs = [pl.BlockSpec((None, chunk, N_BRANCH * w), lambda i, j: (i, j, 0))] + [st(s) for s in state_shapes]
    out_shape = [jax.ShapeDtypeStruct((b, t, N_BRANCH * w), BF16)] + [
        jax.ShapeDtypeStruct((depth, b) + s, F32) for s in state_shapes]

    def body(*refs):
        n_in = len(args) - (len(states_prev) if states_prev is not None else 0)
        kept = refs[:n_in] + refs[len(args):]
        _mixers_kernel(*kept, layer=layer, n_valid=n_valid, has_state=has_state)

    outs = pl.pallas_call(
        body,
        grid=(b, t // chunk),
        in_specs=in_specs,
        out_specs=out_specs,
        out_shape=out_shape,
        scratch_shapes=[pltpu.VMEM((SUBLANES + chunk, N_BRANCH * w), F32)],
        input_output_aliases=aliases,
        compiler_params=pltpu.CompilerParams(
            dimension_semantics=("parallel", "arbitrary"), vmem_limit_bytes=VMEM_LIMIT_BYTES),
        name="mixers",
    )(*args)
    return outs[0], tuple(outs[1:])


def _merge_ffn_kernel(x_ref, ys_ref, gate_ref, wbr_ref, wout_ref, n2_ref, wup_ref, wdown_ref, fin_ref,
                      out_ref, *, last):
    w = MIX_WIDTH
    d = x_ref.shape[1]
    merged = jnp.zeros(x_ref.shape, F32)
    for n in range(N_BRANCH):
        br = jnp.dot(ys_ref[:, n * w:(n + 1) * w], wbr_ref[n], preferred_element_type=F32)
        merged = merged + _sigmoid(gate_ref[:, n * d:(n + 1) * d]) * br
    x1 = x_ref[...] + jnp.dot(merged.astype(BF16), wout_ref[...], preferred_element_type=F32)
    xn = (_rms(x1) * n2_ref[...]).astype(BF16)
    hid = jnp.square(jnp.maximum(jnp.dot(xn, wup_ref[...], preferred_element_type=F32), 0.0))
    x2 = x1 + jnp.dot(hid.astype(BF16), wdown_ref[...], preferred_element_type=F32)
    if last:
        x2 = _rms(x2) * fin_ref[...]
    out_ref[...] = x2


def _merge_ffn(x2d, ys2d, big2d, w_branch, w_out, norm2_w, w_up, w_down, final_w, *, tm, last):
    m, d = x2d.shape
    w = MIX_WIDTH
    d_ff = w_up.shape[1]
    gate_block = N_BRANCH * d
    assert m % tm == 0 and (N_MIX_SEGS * w) % gate_block == 0
    gate_idx = (N_MIX_SEGS * w) // gate_block

    def const(shape):
        return pl.BlockSpec(shape, lambda i: (0,) * len(shape), pipeline_mode=pl.Buffered(1))

    return pl.pallas_call(
        functools.partial(_merge_ffn_kernel, last=last),
        grid=(m // tm,),
        in_specs=[
            pl.BlockSpec((tm, d), lambda i: (i, 0)),
            pl.BlockSpec((tm, N_BRANCH * w), lambda i: (i, 0)),
            pl.BlockSpec((tm, gate_block), lambda i: (i, gate_idx)),
            const((N_BRANCH, w, d)), const((d, d)), const((1, d)), const((d, d_ff)), const((d_ff, d)),
            const((1, d)),
        ],
        out_specs=pl.BlockSpec((tm, d), lambda i: (i, 0)),
        out_shape=jax.ShapeDtypeStruct((m, d), F32),
        compiler_params=pltpu.CompilerParams(
            dimension_semantics=("parallel",), vmem_limit_bytes=VMEM_LIMIT_BYTES),
        name="merge_ffn",
    )(x2d, ys2d, big2d, w_branch, w_out, norm2_w, w_up, w_down, final_w)


def _split_w_in(w_in_l):
    w, nh, d = MIX_WIDTH, N_HEADS, w_in_l.shape[0]
    sizes = [w] * 4 + [nh] * 2 + [w] * 4 + [nh] * 2 + [w] * 4 + [d] * N_BRANCH
    offs = [0]
    for s in sizes:
        offs.append(offs[-1] + s)
    cols = [w_in_l[:, offs[i]:offs[i + 1]] for i in range(len(sizes))]
    wide = [cols[i] for i in (0, 1, 2, 3, 6, 7, 8, 9, 12, 13, 14, 15, 16, 17, 18)]
    small = [cols[i] for i in (4, 5, 10, 11)]
    w_big = jnp.concatenate(wide, axis=1).astype(BF16)
    w_small = jnp.concatenate(small + [jnp.zeros((d, SMALL_COLS - 4 * nh), w_in_l.dtype)], axis=1).astype(BF16)
    return w_big, w_small


def _pad_lanes(parts, total):
    row = jnp.concatenate([p.reshape(1, -1).astype(F32) for p in parts], axis=1)
    return jnp.pad(row, ((0, 0), (0, total - row.shape[1])))


def _tiles(m, n_big):
    tm_proj = math.gcd(m, 512)
    tm_merge = math.gcd(m, 256)
    tn = math.gcd(n_big, 1536)
    return tm_proj, tm_merge, tn


def _run_group(x3, states_in, n_valid, weights, final_norm_w):
    b, t, d = x3.shape
    depth = len(weights)
    chunk = min(t, CHUNK)
    x2d = x3.reshape(b * t, d)
    tm, tm_merge, tn = _tiles(b * t, weights[0]["w_big"].shape[1])
    states_prev = None
    for l, wl in enumerate(weights):
        big, small = _in_proj(x2d, wl["norm1_w"], wl["w_big"], wl["w_small"], tm=tm, tn=tn)
        ys, states_prev = _mixers(
            big.reshape(b, t, -1), small.reshape(b, t, -1), wl["mix_params"], states_in, states_prev,
            layer=l, depth=depth, chunk=chunk, n_valid=n_valid)
        x2d = _merge_ffn(x2d, ys.reshape(b * t, -1), big, wl["w_branch"], wl["w_out"], wl["norm2_w"],
                         wl["w_up"], wl["w_down"], final_norm_w, tm=tm_merge, last=(l == depth - 1))
    return x2d.reshape(b, t, d), states_prev


def kernel(x_prompt, x_sample, state_mlstm_C, state_mlstm_n, state_mlstm_m, state_gdn_S, state_gdn_conv, state_hgrn_S, norm1_w, w_in, ml_i_bias, ml_f_bias, ml_norm_w, gd_conv_w, gd_A_log, gd_dt_bias, gd_norm_w, hg_lb_logits, hg_norm_w, w_branch, w_out, norm2_w, w_up, w_down, final_norm_w):
    depth, d = norm1_w.shape
    nh = N_HEADS
    weights = []
    for l in range(depth):
        w_big, w_small = _split_w_in(w_in[l])
        zeros_h = jnp.zeros((nh,), F32)
        bias_row = _pad_lanes([ml_i_bias[l], ml_f_bias[l], zeros_h, gd_dt_bias[l]], SMALL_COLS)
        alog_row = _pad_lanes([zeros_h, zeros_h, zeros_h, gd_A_log[l]], SMALL_COLS)
        mix_params = (bias_row, alog_row, ml_norm_w[l].reshape(1, -1), gd_conv_w[l],
                      gd_norm_w[l].reshape(1, -1), hg_lb_logits, hg_norm_w[l].reshape(1, -1))
        weights.append(dict(
            norm1_w=norm1_w[l].reshape(1, d), w_big=w_big, w_small=w_small, mix_params=mix_params,
            w_branch=w_branch[l].astype(BF16), w_out=w_out[l].astype(BF16), norm2_w=norm2_w[l].reshape(1, d),
            w_up=w_up[l].astype(BF16), w_down=w_down[l].astype(BF16)))
    fin = final_norm_w.reshape(1, d)

    def unpack(states):
        s_c, s_n, s_m, s_g, s_cv, s_h = states
        return s_c, s_n, s_m[:, :, 0, :nh], s_g, s_cv, s_h

    assert x_prompt.shape[1] % CHUNK == 0
    y_prompt, p_states = _run_group(x_prompt, None, CHUNK, weights, fin)

    bs, ts, _ = x_sample.shape
    assert ts <= BF16_ROWS
    xs = jnp.pad(x_sample, ((0, 0), (0, BF16_ROWS - ts), (0, 0)))
    m_in = jnp.pad(state_mlstm_m, ((0, 0), (0, 0), (0, SMALL_COLS - nh))).reshape(depth, bs, 1, SMALL_COLS)
    s_in = (state_mlstm_C, state_mlstm_n, m_in, state_gdn_S, state_gdn_conv, state_hgrn_S)
    y_s, s_states = _run_group(xs, s_in, ts, weights, fin)
    y_sample = y_s[:, :ts, :]

    return (y_prompt, y_sample) + unpack(p_states) + unpack(s_states)
```

```python
import functools
import math

import jax
import jax.numpy as jnp
from jax import lax
from jax.experimental import pallas as pl
from jax.experimental.pallas import tpu as pltpu

F32 = jnp.float32
BF16 = jnp.bfloat16

HEAD_DIM = 128
N_HEADS = 4
MIX_WIDTH = N_HEADS * HEAD_DIM
N_BRANCH = 3
CONV_W = 4
EPS = 1e-6
CHUNK = 64
SUBLANES = 8
BF16_ROWS = 16
SMALL_COLS = 128
NEG = -1e30
VMEM_LIMIT_BYTES = 56 * 1024 * 1024

SEG_ML_Q, SEG_ML_K, SEG_ML_V, SEG_ML_O = 0, 1, 2, 3
SEG_GD_Q, SEG_GD_K, SEG_GD_V, SEG_GD_Z = 4, 5, 6, 7
SEG_HG_Q, SEG_HG_F, SEG_HG_I, SEG_HG_G = 8, 9, 10, 11
N_MIX_SEGS = 12
SM_I, SM_F, SM_B, SM_A = 0, N_HEADS, 2 * N_HEADS, 3 * N_HEADS


def _dot(a, b):
    return jnp.dot(a.astype(BF16), b.astype(BF16), preferred_element_type=F32)


def _dot_nt(a, b):
    return lax.dot_general(a.astype(BF16), b.astype(BF16), (((1,), (1,)), ((), ())),
                           preferred_element_type=F32)


def _dot_tn(a, b):
    return _dot(a.T, b)


def _f32_dot(a, b):
    return jnp.dot(a, b, preferred_element_type=F32)


def _split2(a):
    hi = a.astype(BF16)
    lo = (a - hi.astype(F32)).astype(BF16)
    return hi, lo


def _split3(a):
    hi = a.astype(BF16)
    r = a - hi.astype(F32)
    mid = r.astype(BF16)
    lo = (r - mid.astype(F32)).astype(BF16)
    return hi, mid, lo


def _dot_exact_lhs(a01, b):
    a = a01.astype(BF16)
    return sum(_f32_dot(a, part) for part in _split3(b))


def _dot_exact_rhs(a, b01):
    b = b01.astype(BF16)
    return sum(_f32_dot(part, b) for part in _split3(a))


def _sigmoid(x):
    return 1.0 / (1.0 + jnp.exp(-x))


def _silu(x):
    return x * _sigmoid(x)


def _softplus(x):
    return jnp.maximum(x, 0.0) + jnp.log(1.0 + jnp.exp(-jnp.abs(x)))


def _log_sigmoid(x):
    return -_softplus(-x)


def _rms(x, axis=-1):
    return x * lax.rsqrt(jnp.mean(x * x, axis=axis, keepdims=True) + EPS)


def _in_proj_kernel(x_ref, nw_ref, wbig_ref, wsmall_ref, big_ref, small_ref, xn_ref):
    j = pl.program_id(1)

    @pl.when(j == 0)
    def _():
        xn = _rms(x_ref[...]) * nw_ref[...]
        xn_ref[...] = xn.astype(BF16)
        small_ref[...] = jnp.dot(xn_ref[...], wsmall_ref[...], preferred_element_type=F32)

    big_ref[...] = jnp.dot(xn_ref[...], wbig_ref[...], preferred_element_type=F32)


def _in_proj(x2d, norm_w, w_big, w_small, *, tm, tn):
    m, d = x2d.shape
    n_big = w_big.shape[1]
    assert m % tm == 0 and n_big % tn == 0
    return pl.pallas_call(
        _in_proj_kernel,
        grid=(m // tm, n_big // tn),
        in_specs=[
            pl.BlockSpec((tm, d), lambda i, j: (i, 0)),
            pl.BlockSpec((1, d), lambda i, j: (0, 0)),
            pl.BlockSpec((d, tn), lambda i, j: (0, j)),
            pl.BlockSpec((d, SMALL_COLS), lambda i, j: (0, 0)),
        ],
        out_specs=[
            pl.BlockSpec((tm, tn), lambda i, j: (i, j)),
            pl.BlockSpec((tm, SMALL_COLS), lambda i, j: (i, 0)),
        ],
        out_shape=[
            jax.ShapeDtypeStruct((m, n_big), F32),
            jax.ShapeDtypeStruct((m, SMALL_COLS), F32),
        ],
        scratch_shapes=[pltpu.VMEM((tm, d), BF16)],
        compiler_params=pltpu.CompilerParams(
            dimension_semantics=("parallel", "arbitrary"), vmem_limit_bytes=VMEM_LIMIT_BYTES),
        name="in_proj",
    )(x2d, norm_w, w_big, w_small)


def _heads(fn, *xs):
    return jnp.stack([fn(*(x[h] for x in xs)) for h in range(xs[0].shape[0])])


def _head_major(x):
    return jnp.stack([x[:, h * HEAD_DIM:(h + 1) * HEAD_DIM] for h in range(x.shape[1] // HEAD_DIM)])


def _token_major(x):
    return jnp.concatenate([x[h] for h in range(x.shape[0])], axis=1)


def _heads_dot_hi(a, b):
    ah, al = _split2(a)
    bh, bl = _split2(b)
    return _heads(_f32_dot, ah, bh) + _heads(_f32_dot, ah, bl) + _heads(_f32_dot, al, bh)


def _inv_unit_lower(n):
    c = n.shape[-1]
    eye = (lax.broadcasted_iota(jnp.int32, (c, c), 0) == lax.broadcasted_iota(jnp.int32, (c, c), 1)).astype(F32)
    acc = eye - n
    power = n
    span = 2
    while span < c:
        power = _heads_dot_hi(power, power)
        acc = _heads_dot_hi(acc, eye + power)
        span *= 2
    return acc


def _mlstm_chunk(q, k, v, i_col, b_col, i_row, b_row, causal, c_state, n_state, m_state):
    c = q.shape[1]
    a_col = b_col + m_state
    d = jnp.where(causal, b_col - b_row + i_row, NEG)
    mt = jnp.maximum(a_col, jnp.max(d, axis=2, keepdims=True))
    p = _heads(_dot_nt, q, k) * jnp.exp(d - mt)
    w_st = jnp.exp(a_col - mt)
    num = _heads(_dot, p, v) + w_st * _heads(_dot, q, c_state)
    den = jnp.sum(p, axis=2, keepdims=True) + w_st * jnp.sum(q * n_state, axis=2, keepdims=True)
    h = num / jnp.maximum(jnp.abs(den), jnp.exp(-mt))
    m_new = mt[:, c - 1:c, :]
    b_last = b_col[:, c - 1:c, :]
    ws_col = jnp.exp(b_last - b_col + i_col - m_new)
    wc = jnp.exp(b_last + m_state - m_new)
    kw = k * ws_col
    c_new = wc * c_state + _heads(_dot_tn, kw, v)
    n_new = wc * n_state + jnp.sum(kw, axis=1, keepdims=True)
    return h, c_new, n_new, m_new


def _gdn_chunk(q, k, v, beta_col, g_col, g_row, incl, strict, s_state):
    c = q.shape[1]
    dh = v.shape[2]
    decay = jnp.exp(jnp.where(incl, g_col - g_row, NEG))
    qk_kk = _heads(_dot_nt, jnp.concatenate([q, k], axis=1), k)
    qk = qk_kk[:, :c]
    kk = qk_kk[:, c:]
    n = jnp.where(strict, beta_col * kk * decay, 0.0)
    a_inv = _inv_unit_lower(n)
    eg_col = jnp.exp(g_col)
    rhs = jnp.concatenate([beta_col * v, (beta_col * eg_col) * k], axis=2)
    sol = _heads_dot_hi(a_inv, rhs)
    u = sol[:, :, :dh]
    w = sol[:, :, dh:]
    wq_s = _heads(_dot, jnp.concatenate([w, q * eg_col], axis=1), s_state)
    v_new = u - wq_s[:, :c]
    o = wq_s[:, c:] + _heads(_dot, qk * decay, v_new)
    g_last = g_col[:, c - 1:c, :]
    s_new = jnp.exp(g_last) * s_state + _heads(_dot_tn, k * jnp.exp(g_last - g_col), v_new)
    return o, s_new


def _row_block_bcast(x, block, row):
    nh, c, l = x.shape
    x3 = x.reshape(nh * c // block, block, l)
    return jnp.broadcast_to(x3[:, row:row + 1, :], x3.shape).reshape(nh, c, l)


def _hgrn_levels(c):
    return [hb for hb in (8, 16, 32, 64, 128) if 2 * hb <= c]


def _hgrn_level_masks(c):
    t = lax.broadcasted_iota(jnp.int32, (c, c), 0)
    s = lax.broadcasted_iota(jnp.int32, (c, c), 1)
    masks = []
    for hb in _hgrn_levels(c):
        same = (t // (2 * hb)) == (s // (2 * hb))
        masks.append(same & ((t % (2 * hb)) >= hb) & ((s % (2 * hb)) < hb))
    return masks


def _hgrn_chunk(q, k, v, bc, level_masks, s_state):
    nh, c, dh = q.shape
    o = _heads(_dot, q * jnp.exp(bc), s_state)

    a = jnp.zeros((nh, c, c), F32)
    for hb, mask in zip(_hgrn_levels(c), level_masks):
        mid = _row_block_bcast(bc, 2 * hb, hb - 1)
        ql = q * jnp.exp(jnp.minimum(bc - mid, 0.0))
        kl = k * jnp.exp(jnp.minimum(mid - bc, 0.0))
        a = a + jnp.where(mask, _heads(_dot_nt, ql, kl), 0.0)
    if _hgrn_levels(c):
        o = o + _heads(_dot, a, v)

    nb = nh * c // SUBLANES
    q3 = q.reshape(nb, SUBLANES, dh)
    k3 = k.reshape(nb, SUBLANES, dh)
    v3 = v.reshape(nb, SUBLANES, dh)
    b3 = bc.reshape(nb, SUBLANES, dh)
    row = lax.broadcasted_iota(jnp.int32, (nb, SUBLANES, 1), 1)
    od = jnp.zeros((nb, SUBLANES, dh), F32)
    for s in range(SUBLANES):
        e = q3 * k3[:, s:s + 1, :] * jnp.exp(jnp.minimum(b3 - b3[:, s:s + 1, :], 0.0))
        col = jnp.sum(e, axis=2, keepdims=True)
        od = od + jnp.where(row >= s, col, 0.0) * v3[:, s:s + 1, :]
    o = o + od.reshape(nh, c, dh)

    b_last = bc[:, c - 1:c, :]
    decay_col = _heads(lambda r: jnp.broadcast_to(r, (SUBLANES, dh)).T[:, 0:1], jnp.exp(b_last))
    s_new = decay_col * s_state + _heads(_dot_tn, k * jnp.exp(b_last - bc), v)
    return o, s_new


def _mixers_kernel(*refs, layer, n_valid, has_state):
    if has_state:
        (big_ref, small_ref, bias_ref, alog_ref, mlnw_ref, convw_ref, gdnw_ref, lbl_ref, hgnw_ref,
         c0_ref, n0_ref, m0_ref, sg0_ref, cv0_ref, sh0_ref,
         ys_ref, c_ref, n_ref, m_ref, sg_ref, cv_ref, sh_ref, convbuf) = refs
    else:
        (big_ref, small_ref, bias_ref, alog_ref, mlnw_ref, convw_ref, gdnw_ref, lbl_ref, hgnw_ref,
         ys_ref, c_ref, n_ref, m_ref, sg_ref, cv_ref, sh_ref, convbuf) = refs
    j = pl.program_id(1)
    nj = pl.num_programs(1)
    c = big_ref.shape[0]
    w = MIX_WIDTH
    dh = HEAD_DIM
    nh = N_HEADS
    tail = CONV_W - 1

    @pl.when(j == 0)
    def _():
        if has_state:
            c_ref[...] = c0_ref[...]
            n_ref[...] = n0_ref[...]
            m_ref[...] = m0_ref[...]
            sg_ref[...] = sg0_ref[...]
            sh_ref[...] = sh0_ref[...]
            convbuf[SUBLANES - tail:SUBLANES, :] = cv0_ref[...]
        else:
            c_ref[...] = jnp.zeros_like(c_ref)
            n_ref[...] = jnp.zeros_like(n_ref)
            m_ref[...] = jnp.zeros_like(m_ref)
            sg_ref[...] = jnp.zeros_like(sg_ref)
            sh_ref[...] = jnp.zeros_like(sh_ref)
            convbuf[SUBLANES - tail:SUBLANES, :] = jnp.zeros((tail, N_BRANCH * w), F32)

    def seg(i):
        return big_ref[:, i * w:(i + 1) * w]

    lane = lax.broadcasted_iota(jnp.int32, (c, SMALL_COLS), 1)
    rowi = lax.broadcasted_iota(jnp.int32, (c, SMALL_COLS), 0)
    pre = small_ref[...] + bias_ref[...]
    is_f = (lane >= SM_F) & (lane < SM_B)
    is_b = (lane >= SM_B) & (lane < SM_A)
    is_a = (lane >= SM_A) & (lane < SM_A + nh)
    logg = -jnp.exp(alog_ref[...]) * _softplus(pre)
    gates = jnp.where(is_f, _log_sigmoid(pre), jnp.where(is_b, _sigmoid(pre), jnp.where(is_a, logg, pre)))
    gates = jnp.where(rowi < n_valid, gates, jnp.where(lane < SM_F, NEG, 0.0))
    tt = lax.broadcasted_iota(jnp.int32, (c, c), 0)
    ss = lax.broadcasted_iota(jnp.int32, (c, c), 1)
    incl = tt >= ss
    strict = tt > ss
    ltri = incl.astype(F32)
    cum_src = jnp.where(is_f | is_a, gates, 0.0)
    cums = _dot_exact_lhs(ltri, cum_src)
    gates_t = gates.T
    cums_t = _dot_exact_rhs(cum_src.T, (ss >= tt).astype(F32))

    def cols(x, first):
        return jnp.stack([x[:, first + h:first + h + 1] for h in range(nh)])

    def rows(x, first):
        return jnp.stack([x[first + h:first + h + 1, :] for h in range(nh)])

    m_old = jnp.stack([m_ref[:, h:h + 1] for h in range(nh)])
    hh, c_new, n_new, m_new = _mlstm_chunk(
        _head_major(seg(SEG_ML_Q)), _head_major(seg(SEG_ML_K)) * (dh ** -0.5), _head_major(seg(SEG_ML_V)),
        cols(gates, SM_I), cols(cums, SM_F), rows(gates_t, SM_I), rows(cums_t, SM_F), incl,
        c_ref[...], n_ref[...], m_old)
    c_ref[...] = c_new
    n_ref[...] = n_new
    m_lane = lax.broadcasted_iota(jnp.int32, m_ref.shape, 1)
    m_row = jnp.zeros(m_ref.shape, F32)
    for h in range(nh):
        m_row = jnp.where(m_lane == h, m_new[h], m_row)
    m_ref[...] = m_row
    y_ml = _token_major(_rms(hh)) * mlnw_ref[...] * _sigmoid(seg(SEG_ML_O))
    ys_ref[:, 0:w] = y_ml.astype(ys_ref.dtype)

    convbuf[SUBLANES:SUBLANES + c, :] = big_ref[:, SEG_GD_Q * w:(SEG_GD_V + 1) * w]
    conv = convw_ref[CONV_W - 1:CONV_W, :] * convbuf[SUBLANES:SUBLANES + c, :]
    for jj in range(CONV_W - 1):
        off = SUBLANES - tail + jj
        conv = conv + convw_ref[jj:jj + 1, :] * convbuf[off:off + c, :]
    conv = _silu(conv)
    new_tail = convbuf[SUBLANES + n_valid - tail:SUBLANES + n_valid, :]
    convbuf[SUBLANES - tail:SUBLANES, :] = new_tail
    cq = _head_major(conv[:, 0:w])
    ck = _head_major(conv[:, w:2 * w])
    cv = _head_major(conv[:, 2 * w:3 * w])
    qn = cq * lax.rsqrt(jnp.sum(cq * cq, axis=2, keepdims=True) + EPS) * (dh ** -0.5)
    kn = ck * lax.rsqrt(jnp.sum(ck * ck, axis=2, keepdims=True) + EPS)
    o_gd, sg_new = _gdn_chunk(qn, kn, cv, cols(gates, SM_B), cols(cums, SM_A), rows(cums_t, SM_A),
                              incl, strict, sg_ref[...])
    sg_ref[...] = sg_new
    y_gd = _token_major(_rms(o_gd) * gdnw_ref[...]) * _silu(seg(SEG_GD_Z))
    ys_ref[:, w:2 * w] = y_gd.astype(ys_ref.dtype)

    lbl = lbl_ref[...]
    sm = jnp.exp(lbl - jnp.max(lbl, axis=0, keepdims=True))
    sm = sm / jnp.sum(sm, axis=0, keepdims=True)
    lb = jnp.zeros((1, w), F32)
    for l in range(1, layer + 1):
        lb = lb + sm[l:l + 1, :]
    valid = lax.broadcasted_iota(jnp.int32, (c, w), 0) < n_valid
    hf = seg(SEG_HG_F)
    g_f = lb + (1.0 - lb) * _sigmoid(hf)
    k_hg = jnp.where(valid, (1.0 - lb) * _sigmoid(-hf), 0.0)
    f_log = jnp.where(valid, jnp.log(g_f), 0.0)
    bc = _dot_exact_lhs(ltri, f_log)
    o_hg, sh_new = _hgrn_chunk(_head_major(_silu(seg(SEG_HG_Q))), _head_major(k_hg), _head_major(seg(SEG_HG_I)),
                               _head_major(bc), _hgrn_level_masks(c), sh_ref[...])
    sh_ref[...] = sh_new
    y_hg = _rms(_token_major(o_hg)) * hgnw_ref[...] * _silu(seg(SEG_HG_G))
    ys_ref[:, 2 * w:3 * w] = y_hg.astype(ys_ref.dtype)

    @pl.when(j == nj - 1)
    def _():
        cv_ref[...] = convbuf[SUBLANES - tail:SUBLANES, :]


def _mixers(big3, small3, params, states_in, states_prev, *, layer, depth, chunk, n_valid):
    b, t, n_big = big3.shape
    assert t % chunk == 0 and chunk % SUBLANES == 0 and CONV_W - 1 <= n_valid <= chunk
    assert n_valid == chunk or t == chunk
    w, dh, nh = MIX_WIDTH, HEAD_DIM, N_HEADS
    has_state = states_in is not None
    bias_row, alog_row, ml_norm_w, conv_w, gd_norm_w, lb_logits, hg_norm_w = params

    def const(shape):
        return pl.BlockSpec(shape, lambda i, j: (0,) * len(shape))

    def st(shape):
        return pl.BlockSpec((None, None) + shape, lambda i, j: (layer, i) + (0,) * len(shape))

    state_shapes = [(nh, dh, dh), (nh, 1, dh), (1, SMALL_COLS), (nh, dh, dh), (CONV_W - 1, N_BRANCH * w), (nh, dh, dh)]
    in_specs = [
        pl.BlockSpec((None, chunk, N_MIX_SEGS * w), lambda i, j: (i, j, 0)),
        pl.BlockSpec((None, chunk, SMALL_COLS), lambda i, j: (i, j, 0)),
        const((1, SMALL_COLS)), const((1, SMALL_COLS)), const((1, w)), const((CONV_W, N_BRANCH * w)),
        const((1, dh)), const((depth, w)), const((1, w)),
    ]
    args = [big3, small3, bias_row, alog_row, ml_norm_w, conv_w, gd_norm_w, lb_logits, hg_norm_w]
    if has_state:
        in_specs += [st(s) for s in state_shapes]
        args += list(states_in)
    aliases = {}
    if states_prev is not None:
        for k_out, arr in enumerate(states_prev):
            aliases[len(args)] = 1 + k_out
            in_specs.append(pl.BlockSpec(memory_space=pl.ANY))
            args.append(arr)
    out_specs = [pl.BlockSpec((None, chunk, N_BRANCH * w), lambda i, j: (i, j, 0))] + [st(s) for s in state_shapes]
    out_shape = [jax.ShapeDtypeStruct((b, t, N_BRANCH * w), BF16)] + [
        jax.ShapeDtypeStruct((depth, b) + s, F32) for s in state_shapes]

    def body(*refs):
        n_in = len(args) - (len(states_prev) if states_prev is not None else 0)
        kept = refs[:n_in] + refs[len(args):]
        _mixers_kernel(*kept, layer=layer, n_valid=n_valid, has_state=has_state)

    outs = pl.pallas_call(
        body,
        grid=(b, t // chunk),
        in_specs=in_specs,
        out_specs=out_specs,
        out_shape=out_shape,
        scratch_shapes=[pltpu.VMEM((SUBLANES + chunk, N_BRANCH * w), F32)],
        input_output_aliases=aliases,
        compiler_params=pltpu.CompilerParams(
            dimension_semantics=("parallel", "arbitrary"), vmem_limit_bytes=VMEM_LIMIT_BYTES),
        name="mixers",
    )(*args)
    return outs[0], tuple(outs[1:])


def _merge_ffn_kernel(x_ref, ys_ref, gate_ref, wbr_ref, wout_ref, n2_ref, wup_ref, wdown_ref, fin_ref,
                      out_ref, *, last):
    w = MIX_WIDTH
    d = x_ref.shape[1]
    merged = jnp.zeros(x_ref.shape, F32)
    for n in range(N_BRANCH):
        br = jnp.dot(ys_ref[:, n * w:(n + 1) * w], wbr_ref[n], preferred_element_type=F32)
        merged = merged + _sigmoid(gate_ref[:, n * d:(n + 1) * d]) * br
    x1 = x_ref[...] + jnp.dot(merged.astype(BF16), wout_ref[...], preferred_element_type=F32)
    xn = (_rms(x1) * n2_ref[...]).astype(BF16)
    hid = jnp.square(jnp.maximum(jnp.dot(xn, wup_ref[...], preferred_element_type=F32), 0.0))
    x2 = x1 + jnp.dot(hid.astype(BF16), wdown_ref[...], preferred_element_type=F32)
    if last:
        x2 = _rms(x2) * fin_ref[...]
    out_ref[...] = x2


def _merge_ffn(x2d, ys2d, big2d, w_branch, w_out, norm2_w, w_up, w_down, final_w, *, tm, last):
    m, d = x2d.shape
    w = MIX_WIDTH
    d_ff = w_up.shape[1]
    gate_block = N_BRANCH * d
    assert m % tm == 0 and (N_MIX_SEGS * w) % gate_block == 0
    gate_idx = (N_MIX_SEGS * w) // gate_block

    def const(shape):
        return pl.BlockSpec(shape, lambda i: (0,) * len(shape), pipeline_mode=pl.Buffered(1))

    return pl.pallas_call(
        functools.partial(_merge_ffn_kernel, last=last),
        grid=(m // tm,),
        in_specs=[
            pl.BlockSpec((tm, d), lambda i: (i, 0)),
            pl.BlockSpec((tm, N_BRANCH * w), lambda i: (i, 0)),
            pl.BlockSpec((tm, gate_block), lambda i: (i, gate_idx)),
            const((N_BRANCH, w, d)), const((d, d)), const((1, d)), const((d, d_ff)), const((d_ff, d)),
            const((1, d)),
        ],
        out_specs=pl.BlockSpec((tm, d), lambda i: (i, 0)),
        out_shape=jax.ShapeDtypeStruct((m, d), F32),
        compiler_params=pltpu.CompilerParams(
            dimension_semantics=("parallel",), vmem_limit_bytes=VMEM_LIMIT_BYTES),
        name="merge_ffn",
    )(x2d, ys2d, big2d, w_branch, w_out, norm2_w, w_up, w_down, final_w)


def _split_w_in(w_in_l):
    w, nh, d = MIX_WIDTH, N_HEADS, w_in_l.shape[0]
    sizes = [w] * 4 + [nh] * 2 + [w] * 4 + [nh] * 2 + [w] * 4 + [d] * N_BRANCH
    offs = [0]
    for s in sizes:
        offs.append(offs[-1] + s)
    cols = [w_in_l[:, offs[i]:offs[i + 1]] for i in range(len(sizes))]
    wide = [cols[i] for i in (0, 1, 2, 3, 6, 7, 8, 9, 12, 13, 14, 15, 16, 17, 18)]
    small = [cols[i] for i in (4, 5, 10, 11)]
    w_big = jnp.concatenate(wide, axis=1).astype(BF16)
    w_small = jnp.concatenate(small + [jnp.zeros((d, SMALL_COLS - 4 * nh), w_in_l.dtype)], axis=1).astype(BF16)
    return w_big, w_small


def _pad_lanes(parts, total):
    row = jnp.concatenate([p.reshape(1, -1).astype(F32) for p in parts], axis=1)
    return jnp.pad(row, ((0, 0), (0, total - row.shape[1])))


def _tiles(m, n_big):
    tm_proj = math.gcd(m, 512)
    tm_merge = math.gcd(m, 256)
    tn = math.gcd(n_big, 1536)
    return tm_proj, tm_merge, tn


def _run_group(x3, states_in, n_valid, weights, final_norm_w):
    b, t, d = x3.shape
    depth = len(weights)
    chunk = min(t, CHUNK)
    x2d = x3.reshape(b * t, d)
    tm, tm_merge, tn = _tiles(b * t, weights[0]["w_big"].shape[1])
    states_prev = None
    for l, wl in enumerate(weights):
        big, small = _in_proj(x2d, wl["norm1_w"], wl["w_big"], wl["w_small"], tm=tm, tn=tn)
        ys, states_prev = _mixers(
            big.reshape(b, t, -1), small.reshape(b, t, -1), wl["mix_params"], states_in, states_prev,
            layer=l, depth=depth, chunk=chunk, n_valid=n_valid)
        x2d = _merge_ffn(x2d, ys.reshape(b * t, -1), big, wl["w_branch"], wl["w_out"], wl["norm2_w"],
                         wl["w_up"], wl["w_down"], final_norm_w, tm=tm_merge, last=(l == depth - 1))
    return x2d.reshape(b, t, d), states_prev


def kernel(x_prompt, x_sample, state_mlstm_C, state_mlstm_n, state_mlstm_m, state_gdn_S, state_gdn_conv, state_hgrn_S, norm1_w, w_in, ml_i_bias, ml_f_bias, ml_norm_w, gd_conv_w, gd_A_log, gd_dt_bias, gd_norm_w, hg_lb_logits, hg_norm_w, w_branch, w_out, norm2_w, w_up, w_down, final_norm_w):
    depth, d = norm1_w.shape
    nh = N_HEADS
    weights = []
    for l in range(depth):
        w_big, w_small = _split_w_in(w_in[l])
        zeros_h = jnp.zeros((nh,), F32)
        bias_row = _pad_lanes([ml_i_bias[l], ml_f_bias[l], zeros_h, gd_dt_bias[l]], SMALL_COLS)
        alog_row = _pad_lanes([zeros_h, zeros_h, zeros_h, gd_A_log[l]], SMALL_COLS)
        mix_params = (bias_row, alog_row, ml_norm_w[l].reshape(1, -1), gd_conv_w[l],
                      gd_norm_w[l].reshape(1, -1), hg_lb_logits, hg_norm_w[l].reshape(1, -1))
        weights.append(dict(
            norm1_w=norm1_w[l].reshape(1, d), w_big=w_big, w_small=w_small, mix_params=mix_params,
            w_branch=w_branch[l].astype(BF16), w_out=w_out[l].astype(BF16), norm2_w=norm2_w[l].reshape(1, d),
            w_up=w_up[l].astype(BF16), w_down=w_down[l].astype(BF16)))
    fin = final_norm_w.reshape(1, d)

    def unpack(states):
        s_c, s_n, s_m, s_g, s_cv, s_h = states
        return s_c, s_n.reshape(s_n.shape[:3] + s_n.shape[4:]), s_m[:, :, 0, :nh], s_g, s_cv, s_h

    assert x_prompt.shape[1] % CHUNK == 0
    y_prompt, p_states = _run_group(x_prompt, None, CHUNK, weights, fin)

    bs, ts, _ = x_sample.shape
    assert ts <= BF16_ROWS
    xs = jnp.pad(x_sample, ((0, 0), (0, BF16_ROWS - ts), (0, 0)))
    m_in = jnp.pad(state_mlstm_m, ((0, 0), (0, 0), (0, SMALL_COLS - nh))).reshape(depth, bs, 1, SMALL_COLS)
    n_in = state_mlstm_n.reshape(depth, bs, nh, 1, HEAD_DIM)
    s_in = (state_mlstm_C, n_in, m_in, state_gdn_S, state_gdn_conv, state_hgrn_S)
    y_s, s_states = _run_group(xs, s_in, ts, weights, fin)
    y_sample = y_s[:, :ts, :]

    return (y_prompt, y_sample) + unpack(p_states) + unpack(s_states)
```

```python
import functools
import math

import jax
import jax.numpy as jnp
from jax import lax
from jax.experimental import pallas as pl
from jax.experimental.pallas import tpu as pltpu

F32 = jnp.float32
BF16 = jnp.bfloat16

HEAD_DIM = 128
N_HEADS = 4
MIX_WIDTH = N_HEADS * HEAD_DIM
N_BRANCH = 3
CONV_W = 4
EPS = 1e-6
CHUNK = 64
SUBLANES = 8
BF16_ROWS = 16
SMALL_COLS = 128
NEG = -1e30
VMEM_LIMIT_BYTES = 56 * 1024 * 1024
PROJ_TN = 1536

SEG_ML_Q, SEG_ML_K, SEG_ML_V, SEG_ML_O = 0, 1, 2, 3
SEG_GD_Q, SEG_GD_K, SEG_GD_V, SEG_GD_Z = 4, 5, 6, 7
SEG_HG_Q, SEG_HG_F, SEG_HG_I, SEG_HG_G = 8, 9, 10, 11
N_MIX_SEGS = 12
SM_I, SM_F, SM_B, SM_A = 0, N_HEADS, 2 * N_HEADS, 3 * N_HEADS


def _dot(a, b):
    return jnp.dot(a.astype(BF16), b.astype(BF16), preferred_element_type=F32)


def _dot_nt(a, b):
    return lax.dot_general(a.astype(BF16), b.astype(BF16), (((1,), (1,)), ((), ())),
                           preferred_element_type=F32)


def _dot_tn(a, b):
    return _dot(a.T, b)


def _f32_dot(a, b):
    return jnp.dot(a, b, preferred_element_type=F32)


def _split2(a):
    hi = a.astype(BF16)
    lo = (a - hi.astype(F32)).astype(BF16)
    return hi, lo


def _split3(a):
    hi = a.astype(BF16)
    r = a - hi.astype(F32)
    mid = r.astype(BF16)
    lo = (r - mid.astype(F32)).astype(BF16)
    return hi, mid, lo


def _dot_exact_lhs(a01, b):
    a = a01.astype(BF16)
    return sum(_f32_dot(a, part) for part in _split3(b))


def _dot_exact_rhs(a, b01):
    b = b01.astype(BF16)
    return sum(_f32_dot(part, b) for part in _split3(a))


def _sigmoid(x):
    return 1.0 / (1.0 + jnp.exp(-x))


def _silu(x):
    return x * _sigmoid(x)


def _softplus(x):
    return jnp.maximum(x, 0.0) + jnp.log(1.0 + jnp.exp(-jnp.abs(x)))


def _log_sigmoid(x):
    return -_softplus(-x)


def _rms(x, axis=-1):
    return x * lax.rsqrt(jnp.mean(x * x, axis=axis, keepdims=True) + EPS)


def _in_proj_kernel(x_ref, nw_ref, wbig_ref, wsmall_ref, big_ref, small_ref, xn_ref):
    j = pl.program_id(1)

    @pl.when(j == 0)
    def _():
        xn = _rms(x_ref[...]) * nw_ref[...]
        xn_ref[...] = xn.astype(BF16)
        small_ref[...] = jnp.dot(xn_ref[...], wsmall_ref[...], preferred_element_type=F32)

    big_ref[...] = jnp.dot(xn_ref[...], wbig_ref[j], preferred_element_type=F32)


def _in_proj(x2d, norm_w, w_big, w_small, *, tm):
    m, d = x2d.shape
    n_tiles, _, tn = w_big.shape
    n_big = n_tiles * tn
    assert m % tm == 0
    resident = dict(pipeline_mode=pl.Buffered(1))
    return pl.pallas_call(
        _in_proj_kernel,
        grid=(m // tm, n_tiles),
        in_specs=[
            pl.BlockSpec((tm, d), lambda i, j: (i, 0)),
            pl.BlockSpec((1, d), lambda i, j: (0, 0), **resident),
            pl.BlockSpec((n_tiles, d, tn), lambda i, j: (0, 0, 0), **resident),
            pl.BlockSpec((d, SMALL_COLS), lambda i, j: (0, 0), **resident),
        ],
        out_specs=[
            pl.BlockSpec((tm, tn), lambda i, j: (i, j)),
            pl.BlockSpec((tm, SMALL_COLS), lambda i, j: (i, 0)),
        ],
        out_shape=[
            jax.ShapeDtypeStruct((m, n_big), F32),
            jax.ShapeDtypeStruct((m, SMALL_COLS), F32),
        ],
        scratch_shapes=[pltpu.VMEM((tm, d), BF16)],
        compiler_params=pltpu.CompilerParams(
            dimension_semantics=("parallel", "arbitrary"), vmem_limit_bytes=VMEM_LIMIT_BYTES),
        name="in_proj",
    )(x2d, norm_w, w_big, w_small)


def _heads(fn, *xs):
    return jnp.stack([fn(*(x[h] for x in xs)) for h in range(xs[0].shape[0])])


def _head_major(x):
    return jnp.stack([x[:, h * HEAD_DIM:(h + 1) * HEAD_DIM] for h in range(x.shape[1] // HEAD_DIM)])


def _token_major(x):
    return jnp.concatenate([x[h] for h in range(x.shape[0])], axis=1)


def _heads_dot_hi(a, b):
    ah, al = _split2(a)
    bh, bl = _split2(b)
    return _heads(_f32_dot, ah, bh) + _heads(_f32_dot, ah, bl) + _heads(_f32_dot, al, bh)


def _interleave(stage_generators, stages_per_round):
    results = [None] * len(stage_generators)
    live = list(enumerate(stage_generators))
    while live:
        still = []
        for idx, gen in live:
            try:
                for _ in range(stages_per_round[idx]):
                    next(gen)
                still.append((idx, gen))
            except StopIteration as stop:
                results[idx] = stop.value
        live = still
    return results


def _inv_unit_lower(n):
    c = n.shape[-1]
    eye = (lax.broadcasted_iota(jnp.int32, (c, c), 0) == lax.broadcasted_iota(jnp.int32, (c, c), 1)).astype(F32)
    acc = eye - n
    power = n
    span = 2
    while span < c:
        ph, pl_ = _split2(power)
        if span > 2:
            ah, al = _split2(acc)
            acc = acc + (_heads(_f32_dot, ah, ph) + _heads(_f32_dot, ah, pl_) + _heads(_f32_dot, al, ph))
        power = _heads(_f32_dot, ph, ph) + _heads(_f32_dot, ph, pl_) + _heads(_f32_dot, pl_, ph)
        span *= 2
        yield
    if c > 2:
        acc = acc + _heads_dot_hi(acc, power)
    return acc


def _mlstm_chunk(q, k, v, qk, qc, i_col, b_col, i_row, b_row, causal, c_state, n_state, m_state):
    c = q.shape[1]
    a_col = b_col + m_state
    d = jnp.where(causal, b_col - b_row + i_row, NEG)
    mt = jnp.maximum(a_col, jnp.max(d, axis=2, keepdims=True))
    p = qk * jnp.exp(d - mt)
    w_st = jnp.exp(a_col - mt)
    yield
    num = _heads(_dot, p, v) + w_st * qc
    yield
    den = jnp.sum(p, axis=2, keepdims=True) + w_st * jnp.sum(q * n_state, axis=2, keepdims=True)
    h = num / jnp.maximum(jnp.abs(den), jnp.exp(-mt))
    m_new = mt[:, c - 1:c, :]
    b_last = b_col[:, c - 1:c, :]
    ws_col = jnp.exp(b_last - b_col + i_col - m_new)
    wc = jnp.exp(b_last + m_state - m_new)
    kw = k * ws_col
    yield
    c_new = wc * c_state + _heads(_dot_tn, kw, v)
    n_new = wc * n_state + jnp.sum(kw, axis=1, keepdims=True)
    return h, c_new, n_new, m_new


def _gdn_chunk(q, k, v, qk_kk, beta_col, g_col, g_row, incl, strict, s_state):
    c = q.shape[1]
    dh = v.shape[2]
    decay = jnp.exp(jnp.where(incl, g_col - g_row, NEG))
    qk = qk_kk[:, :c]
    kk = qk_kk[:, c:]
    n = jnp.where(strict, beta_col * kk * decay, 0.0)
    yield
    a_inv = yield from _inv_unit_lower(n)
    eg_col = jnp.exp(g_col)
    rhs = jnp.concatenate([beta_col * v, (beta_col * eg_col) * k], axis=2)
    yield
    sol = _heads_dot_hi(a_inv, rhs)
    u = sol[:, :, :dh]
    w = sol[:, :, dh:]
    yield
    wq_s = _heads(_dot, jnp.concatenate([w, q * eg_col], axis=1), s_state)
    v_new = u - wq_s[:, :c]
    yield
    o = wq_s[:, c:] + _heads(_dot, qk * decay, v_new)
    g_last = g_col[:, c - 1:c, :]
    yield
    s_new = jnp.exp(g_last) * s_state + _heads(_dot_tn, k * jnp.exp(g_last - g_col), v_new)
    return o, s_new


def _row_block_bcast(x, block, row):
    nh, c, l = x.shape
    x3 = x.reshape(nh * c // block, block, l)
    return jnp.broadcast_to(x3[:, row:row + 1, :], x3.shape).reshape(nh, c, l)


def _boundary_rows(x, hb):
    if 2 * hb >= SUBLANES:
        return _row_block_bcast(x, 2 * hb, hb - 1)
    sub = lax.broadcasted_iota(jnp.int32, (1, x.shape[1], 1), 1) % SUBLANES
    out = _row_block_bcast(x, SUBLANES, hb - 1)
    for start in range(2 * hb, SUBLANES, 2 * hb):
        out = jnp.where(sub >= start, _row_block_bcast(x, SUBLANES, start + hb - 1), out)
    return out


def _hgrn_levels(c):
    return [hb for hb in (1, 2, 4, 8, 16, 32, 64) if 2 * hb <= c]


def _hgrn_level_masks(c):
    t = lax.broadcasted_iota(jnp.int32, (c, c), 0)
    s = lax.broadcasted_iota(jnp.int32, (c, c), 1)
    masks = []
    for hb in _hgrn_levels(c):
        same = (t // (2 * hb)) == (s // (2 * hb))
        masks.append(same & ((t % (2 * hb)) >= hb) & ((s % (2 * hb)) < hb))
    return masks, t == s


def _hgrn_chunk(q, k, v, bc, level_masks, s_state):
    nh, c, dh = q.shape
    masks, diag = level_masks
    o = _heads(_dot, q * jnp.exp(bc), s_state)
    yield

    a = jnp.where(diag, jnp.sum(q * k, axis=2, keepdims=True), 0.0)
    for hb, mask in zip(_hgrn_levels(c), masks):
        mid = _boundary_rows(bc, hb)
        ql = q * jnp.exp(jnp.minimum(bc - mid, 0.0))
        kl = k * jnp.exp(jnp.minimum(mid - bc, 0.0))
        a = a + jnp.where(mask, _heads(_dot_nt, ql, kl), 0.0)
        yield
    o = o + _heads(_dot, a, v)
    yield

    b_last = bc[:, c - 1:c, :]
    decay_col = _heads(lambda r: jnp.broadcast_to(r, (SUBLANES, dh)).T[:, 0:1], jnp.exp(b_last))
    s_new = decay_col * s_state + _heads(_dot_tn, k * jnp.exp(b_last - bc), v)
    return o, s_new


def _mixers_kernel(*refs, layer, n_valid, has_state):
    if has_state:
        (big_ref, small_ref, bias_ref, alog_ref, mlnw_ref, convw_ref, gdnw_ref, lbl_ref, hgnw_ref,
         c0_ref, n0_ref, m0_ref, sg0_ref, cv0_ref, sh0_ref,
         ys_ref, c_ref, n_ref, m_ref, sg_ref, cv_ref, sh_ref, convbuf) = refs
    else:
        (big_ref, small_ref, bias_ref, alog_ref, mlnw_ref, convw_ref, gdnw_ref, lbl_ref, hgnw_ref,
         ys_ref, c_ref, n_ref, m_ref, sg_ref, cv_ref, sh_ref, convbuf) = refs
    j = pl.program_id(1)
    nj = pl.num_programs(1)
    c = big_ref.shape[0]
    w = MIX_WIDTH
    dh = HEAD_DIM
    nh = N_HEADS
    tail = CONV_W - 1

    @pl.when(j == 0)
    def _():
        if has_state:
            c_ref[...] = c0_ref[...]
            n_ref[...] = n0_ref[...]
            m_ref[...] = m0_ref[...]
            sg_ref[...] = sg0_ref[...]
            sh_ref[...] = sh0_ref[...]
            convbuf[SUBLANES - tail:SUBLANES, :] = cv0_ref[...]
        else:
            c_ref[...] = jnp.zeros_like(c_ref)
            n_ref[...] = jnp.zeros_like(n_ref)
            m_ref[...] = jnp.zeros_like(m_ref)
            sg_ref[...] = jnp.zeros_like(sg_ref)
            sh_ref[...] = jnp.zeros_like(sh_ref)
            convbuf[SUBLANES - tail:SUBLANES, :] = jnp.zeros((tail, N_BRANCH * w), F32)

    def seg(i):
        return big_ref[:, i * w:(i + 1) * w]

    lane = lax.broadcasted_iota(jnp.int32, (c, SMALL_COLS), 1)
    rowi = lax.broadcasted_iota(jnp.int32, (c, SMALL_COLS), 0)
    pre = small_ref[...] + bias_ref[...]
    is_f = (lane >= SM_F) & (lane < SM_B)
    is_b = (lane >= SM_B) & (lane < SM_A)
    is_a = (lane >= SM_A) & (lane < SM_A + nh)
    logg = -jnp.exp(alog_ref[...]) * _softplus(pre)
    gates = jnp.where(is_f, _log_sigmoid(pre), jnp.where(is_b, _sigmoid(pre), jnp.where(is_a, logg, pre)))
    gates = jnp.where(rowi < n_valid, gates, jnp.where(lane < SM_F, NEG, 0.0))
    tt = lax.broadcasted_iota(jnp.int32, (c, c), 0)
    ss = lax.broadcasted_iota(jnp.int32, (c, c), 1)
    incl = tt >= ss
    strict = tt > ss
    ltri = incl.astype(F32)
    cum_src = jnp.where(is_f | is_a, gates, 0.0)
    cums = _dot_exact_lhs(ltri, cum_src)
    gates_t = gates.T
    cums_t = _dot_exact_rhs(cum_src.T, (ss >= tt).astype(F32))

    def cols(x, first):
        return jnp.stack([x[:, first + h:first + h + 1] for h in range(nh)])

    def rows(x, first):
        return jnp.stack([x[first + h:first + h + 1, :] for h in range(nh)])

    lbl = lbl_ref[...]
    sm = jnp.exp(lbl - jnp.max(lbl, axis=0, keepdims=True))
    sm = sm / jnp.sum(sm, axis=0, keepdims=True)
    lb = jnp.zeros((1, w), F32)
    for l in range(1, layer + 1):
        lb = lb + sm[l:l + 1, :]
    valid = lax.broadcasted_iota(jnp.int32, (c, w), 0) < n_valid
    hf = seg(SEG_HG_F)
    g_f = lb + (1.0 - lb) * _sigmoid(hf)
    k_hg = jnp.where(valid, (1.0 - lb) * _sigmoid(-hf), 0.0)
    f_log = jnp.where(valid, jnp.log(g_f), 0.0)
    bc = _dot_exact_lhs(ltri, f_log)

    convbuf[SUBLANES:SUBLANES + c, :] = big_ref[:, SEG_GD_Q * w:(SEG_GD_V + 1) * w]
    conv = convw_ref[CONV_W - 1:CONV_W, :] * convbuf[SUBLANES:SUBLANES + c, :]
    for jj in range(CONV_W - 1):
        off = SUBLANES - tail + jj
        conv = conv + convw_ref[jj:jj + 1, :] * convbuf[off:off + c, :]
    conv = _silu(conv)
    new_tail = convbuf[SUBLANES + n_valid - tail:SUBLANES + n_valid, :]
    convbuf[SUBLANES - tail:SUBLANES, :] = new_tail
    cq = _head_major(conv[:, 0:w])
    ck = _head_major(conv[:, w:2 * w])
    cv = _head_major(conv[:, 2 * w:3 * w])
    qn = cq * lax.rsqrt(jnp.sum(cq * cq, axis=2, keepdims=True) + EPS) * (dh ** -0.5)
    kn = ck * lax.rsqrt(jnp.sum(ck * ck, axis=2, keepdims=True) + EPS)
    qk_kk = _heads(_dot_nt, jnp.concatenate([qn, kn], axis=1), kn)

    ml_q = _head_major(seg(SEG_ML_Q))
    ml_k = _head_major(seg(SEG_ML_K)) * (dh ** -0.5)
    c_old = c_ref[...]
    ml_qk = _heads(_dot_nt, ml_q, ml_k)
    ml_qc = _heads(_dot, ml_q, c_old)
    m_old = jnp.stack([m_ref[:, h:h + 1] for h in range(nh)])

    mlstm = _mlstm_chunk(
        ml_q, ml_k, _head_major(seg(SEG_ML_V)), ml_qk, ml_qc,
        cols(gates, SM_I), cols(cums, SM_F), rows(gates_t, SM_I), rows(cums_t, SM_F), incl,
        c_old, n_ref[...], m_old)
    gdn = _gdn_chunk(qn, kn, cv, qk_kk, cols(gates, SM_B), cols(cums, SM_A), rows(cums_t, SM_A),
                     incl, strict, sg_ref[...])
    hgrn = _hgrn_chunk(_head_major(_silu(seg(SEG_HG_Q))), _head_major(k_hg), _head_major(seg(SEG_HG_I)),
                       _head_major(bc), _hgrn_level_masks(c), sh_ref[...])
    (hh, c_new, n_new, m_new), (o_gd, sg_new), (o_hg, sh_new) = _interleave((mlstm, gdn, hgrn), (1, 3, 2))

    c_ref[...] = c_new
    n_ref[...] = n_new
    m_lane = lax.broadcasted_iota(jnp.int32, m_ref.shape, 1)
    m_row = jnp.zeros(m_ref.shape, F32)
    for h in range(nh):
        m_row = jnp.where(m_lane == h, m_new[h], m_row)
    m_ref[...] = m_row
    y_ml = _token_major(_rms(hh)) * mlnw_ref[...] * _sigmoid(seg(SEG_ML_O))
    ys_ref[:, 0:w] = y_ml.astype(ys_ref.dtype)

    sg_ref[...] = sg_new
    y_gd = _token_major(_rms(o_gd) * gdnw_ref[...]) * _silu(seg(SEG_GD_Z))
    ys_ref[:, w:2 * w] = y_gd.astype(ys_ref.dtype)

    sh_ref[...] = sh_new
    y_hg = _rms(_token_major(o_hg)) * hgnw_ref[...] * _silu(seg(SEG_HG_G))
    ys_ref[:, 2 * w:3 * w] = y_hg.astype(ys_ref.dtype)

    @pl.when(j == nj - 1)
    def _():
        cv_ref[...] = convbuf[SUBLANES - tail:SUBLANES, :]


def _mixers(big3, small3, params, states_in, states_prev, *, layer, depth, chunk, n_valid):
    b, t, n_big = big3.shape
    assert t % chunk == 0 and chunk % SUBLANES == 0 and CONV_W - 1 <= n_valid <= chunk
    assert n_valid == chunk or t == chunk
    w, dh, nh = MIX_WIDTH, HEAD_DIM, N_HEADS
    has_state = states_in is not None
    bias_row, alog_row, ml_norm_w, conv_w, gd_norm_w, lb_logits, hg_norm_w = params

    def const(shape):
        return pl.BlockSpec(shape, lambda i, j: (0,) * len(shape))

    def st(shape):
        return pl.BlockSpec((None, None) + shape, lambda i, j: (layer, i) + (0,) * len(shape))

    state_shapes = [(nh, dh, dh), (nh, 1, dh), (1, SMALL_COLS), (nh, dh, dh), (CONV_W - 1, N_BRANCH * w), (nh, dh, dh)]
    in_specs = [
        pl.BlockSpec((None, chunk, N_MIX_SEGS * w), lambda i, j: (i, j, 0)),
        pl.BlockSpec((None, chunk, SMALL_COLS), lambda i, j: (i, j, 0)),
        const((1, SMALL_COLS)), const((1, SMALL_COLS)), const((1, w)), const((CONV_W, N_BRANCH * w)),
        const((1, dh)), const((depth, w)), const((1, w)),
    ]
    args = [big3, small3, bias_row, alog_row, ml_norm_w, conv_w, gd_norm_w, lb_logits, hg_norm_w]
    if has_state:
        in_specs += [st(s) for s in state_shapes]
        args += list(states_in)
    aliases = {}
    if states_prev is not None:
        for k_out, arr in enumerate(states_prev):
            aliases[len(args)] = 1 + k_out
            in_specs.append(pl.BlockSpec(memory_space=pl.ANY))
            args.append(arr)
    out_specs = [pl.BlockSpec((None, chunk, N_BRANCH * w), lambda i, j: (i, j, 0))] + [st(s) for s in state_shapes]
    out_shape = [jax.ShapeDtypeStruct((b, t, N_BRANCH * w), BF16)] + [
        jax.ShapeDtypeStruct((depth, b) + s, F32) for s in state_shapes]

    def body(*refs):
        n_in = len(args) - (len(states_prev) if states_prev is not None else 0)
        kept = refs[:n_in] + refs[len(args):]
        _mixers_kernel(*kept, layer=layer, n_valid=n_valid, has_state=has_state)

    outs = pl.pallas_call(
        body,
        grid=(b, t // chunk),
        in_specs=in_specs,
        out_specs=out_specs,
        out_shape=out_shape,
        scratch_shapes=[pltpu.VMEM((SUBLANES + chunk, N_BRANCH * w), F32)],
        input_output_aliases=aliases,
        compiler_params=pltpu.CompilerParams(
            dimension_semantics=("parallel", "arbitrary"), vmem_limit_bytes=VMEM_LIMIT_BYTES),
        name="mixers",
    )(*args)
    return outs[0], tuple(outs[1:])


def _merge_ffn_kernel(x_ref, ys_ref, gate_ref, wbr_ref, wout_ref, n2_ref, wup_ref, wdown_ref, fin_ref,
                      out_ref, *, last):
    w = MIX_WIDTH
    d = x_ref.shape[1]
    merged = jnp.zeros(x_ref.shape, F32)
    for n in range(N_BRANCH):
        br = jnp.dot(ys_ref[:, n * w:(n + 1) * w], wbr_ref[n], preferred_element_type=F32)
        merged = merged + _sigmoid(gate_ref[:, n * d:(n + 1) * d]) * br
    x1 = x_ref[...] + jnp.dot(merged.astype(BF16), wout_ref[...], preferred_element_type=F32)
    xn = (_rms(x1) * n2_ref[...]).astype(BF16)
    hid = jnp.square(jnp.maximum(jnp.dot(xn, wup_ref[...], preferred_element_type=F32), 0.0))
    x2 = x1 + jnp.dot(hid.astype(BF16), wdown_ref[...], preferred_element_type=F32)
    if last:
        x2 = _rms(x2) * fin_ref[...]
    out_ref[...] = x2


def _merge_ffn(x2d, ys2d, big2d, w_branch, w_out, norm2_w, w_up, w_down, final_w, *, tm, last):
    m, d = x2d.shape
    w = MIX_WIDTH
    d_ff = w_up.shape[1]
    gate_block = N_BRANCH * d
    assert m % tm == 0 and (N_MIX_SEGS * w) % gate_block == 0
    gate_idx = (N_MIX_SEGS * w) // gate_block

    def const(shape):
        return pl.BlockSpec(shape, lambda i: (0,) * len(shape), pipeline_mode=pl.Buffered(1))

    return pl.pallas_call(
        functools.partial(_merge_ffn_kernel, last=last),
        grid=(m // tm,),
        in_specs=[
            pl.BlockSpec((tm, d), lambda i: (i, 0)),
            pl.BlockSpec((tm, N_BRANCH * w), lambda i: (i, 0)),
            pl.BlockSpec((tm, gate_block), lambda i: (i, gate_idx)),
            const((N_BRANCH, w, d)), const((d, d)), const((1, d)), const((d, d_ff)), const((d_ff, d)),
            const((1, d)),
        ],
        out_specs=pl.BlockSpec((tm, d), lambda i: (i, 0)),
        out_shape=jax.ShapeDtypeStruct((m, d), F32),
        compiler_params=pltpu.CompilerParams(
            dimension_semantics=("parallel",), vmem_limit_bytes=VMEM_LIMIT_BYTES),
        name="merge_ffn",
    )(x2d, ys2d, big2d, w_branch, w_out, norm2_w, w_up, w_down, final_w)


def _split_w_in(w_in_l):
    w, nh, d = MIX_WIDTH, N_HEADS, w_in_l.shape[0]
    sizes = [w] * 4 + [nh] * 2 + [w] * 4 + [nh] * 2 + [w] * 4 + [d] * N_BRANCH
    offs = [0]
    for s in sizes:
        offs.append(offs[-1] + s)
    cols = [w_in_l[:, offs[i]:offs[i + 1]] for i in range(len(sizes))]
    wide = [cols[i] for i in (0, 1, 2, 3, 6, 7, 8, 9, 12, 13, 14, 15, 16, 17, 18)]
    small = [cols[i] for i in (4, 5, 10, 11)]
    w_big = jnp.concatenate(wide, axis=1).astype(BF16)
    assert w_big.shape[1] % PROJ_TN == 0
    w_big = w_big.reshape(d, w_big.shape[1] // PROJ_TN, PROJ_TN).transpose(1, 0, 2)
    w_small = jnp.concatenate(small + [jnp.zeros((d, SMALL_COLS - 4 * nh), w_in_l.dtype)], axis=1).astype(BF16)
    return w_big, w_small


def _pad_lanes(parts, total):
    row = jnp.concatenate([p.reshape(1, -1).astype(F32) for p in parts], axis=1)
    return jnp.pad(row, ((0, 0), (0, total - row.shape[1])))


def _tiles(m):
    return math.gcd(m, 512), math.gcd(m, 256)


def _run_group(x3, states_in, n_valid, weights, final_norm_w):
    b, t, d = x3.shape
    depth = len(weights)
    chunk = min(t, CHUNK)
    x2d = x3.reshape(b * t, d)
    tm, tm_merge = _tiles(b * t)
    states_prev = None
    for l, wl in enumerate(weights):
        big, small = _in_proj(x2d, wl["norm1_w"], wl["w_big"], wl["w_small"], tm=tm)
        ys, states_prev = _mixers(
            big.reshape(b, t, -1), small.reshape(b, t, -1), wl["mix_params"], states_in, states_prev,
            layer=l, depth=depth, chunk=chunk, n_valid=n_valid)
        x2d = _merge_ffn(x2d, ys.reshape(b * t, -1), big, wl["w_branch"], wl["w_out"], wl["norm2_w"],
                         wl["w_up"], wl["w_down"], final_norm_w, tm=tm_merge, last=(l == depth - 1))
    return x2d.reshape(b, t, d), states_prev


def kernel(x_prompt, x_sample, state_mlstm_C, state_mlstm_n, state_mlstm_m, state_gdn_S, state_gdn_conv, state_hgrn_S, norm1_w, w_in, ml_i_bias, ml_f_bias, ml_norm_w, gd_conv_w, gd_A_log, gd_dt_bias, gd_norm_w, hg_lb_logits, hg_norm_w, w_branch, w_out, norm2_w, w_up, w_down, final_norm_w):
    depth, d = norm1_w.shape
    nh = N_HEADS
    weights = []
    for l in range(depth):
        w_big, w_small = _split_w_in(w_in[l])
        zeros_h = jnp.zeros((nh,), F32)
        bias_row = _pad_lanes([ml_i_bias[l], ml_f_bias[l], zeros_h, gd_dt_bias[l]], SMALL_COLS)
        alog_row = _pad_lanes([zeros_h, zeros_h, zeros_h, gd_A_log[l]], SMALL_COLS)
        mix_params = (bias_row, alog_row, ml_norm_w[l].reshape(1, -1), gd_conv_w[l],
                      gd_norm_w[l].reshape(1, -1), hg_lb_logits, hg_norm_w[l].reshape(1, -1))
        weights.append(dict(
            norm1_w=norm1_w[l].reshape(1, d), w_big=w_big, w_small=w_small, mix_params=mix_params,
            w_branch=w_branch[l].astype(BF16), w_out=w_out[l].astype(BF16), norm2_w=norm2_w[l].reshape(1, d),
            w_up=w_up[l].astype(BF16), w_down=w_down[l].astype(BF16)))
    fin = final_norm_w.reshape(1, d)

    def unpack(states):
        s_c, s_n, s_m, s_g, s_cv, s_h = states
        return s_c, s_n.reshape(s_n.shape[:3] + s_n.shape[4:]), s_m[:, :, 0, :nh], s_g, s_cv, s_h

    assert x_prompt.shape[1] % CHUNK == 0
    y_prompt, p_states = _run_group(x_prompt, None, CHUNK, weights, fin)

    bs, ts, _ = x_sample.shape
    assert ts <= BF16_ROWS
    xs = jnp.pad(x_sample, ((0, 0), (0, BF16_ROWS - ts), (0, 0)))
    m_in = jnp.pad(state_mlstm_m, ((0, 0), (0, 0), (0, SMALL_COLS - nh))).reshape(depth, bs, 1, SMALL_COLS)
    n_in = state_mlstm_n.reshape(depth, bs, nh, 1, HEAD_DIM)
    s_in = (state_mlstm_C, n_in, m_in, state_gdn_S, state_gdn_conv, state_hgrn_S)
    y_s, s_states = _run_group(xs, s_in, ts, weights, fin)
    y_sample = y_s[:, :ts, :]

    return (y_prompt, y_sample) + unpack(p_states) + unpack(s_states)
```

```python
import functools
import math

import jax
import jax.numpy as jnp
from jax import lax
from jax.experimental import pallas as pl
from jax.experimental.pallas import tpu as pltpu

F32 = jnp.float32
BF16 = jnp.bfloat16

HEAD_DIM = 128
N_HEADS = 4
MIX_WIDTH = N_HEADS * HEAD_DIM
N_BRANCH = 3
CONV_W = 4
EPS = 1e-6
CHUNK = 64
SUBLANES = 8
BF16_ROWS = 16
SMALL_COLS = 128
NEG = -1e30
VMEM_LIMIT_BYTES = 56 * 1024 * 1024
PROJ_TN = 1536
PROMPT_SEQS_PER_STEP = 2
SAMPLE_SEQS_PER_STEP = 8

SEG_ML_Q, SEG_ML_K, SEG_ML_V, SEG_ML_O = 0, 1, 2, 3
SEG_GD_Q, SEG_GD_K, SEG_GD_V, SEG_GD_Z = 4, 5, 6, 7
SEG_HG_Q, SEG_HG_F, SEG_HG_I, SEG_HG_G = 8, 9, 10, 11
N_MIX_SEGS = 12
SM_I, SM_F, SM_B, SM_A = 0, N_HEADS, 2 * N_HEADS, 3 * N_HEADS


def _dot(a, b):
    return jnp.dot(a.astype(BF16), b.astype(BF16), preferred_element_type=F32)


def _dot_nt(a, b):
    return lax.dot_general(a.astype(BF16), b.astype(BF16), (((1,), (1,)), ((), ())),
                           preferred_element_type=F32)


def _dot_tn(a, b):
    return _dot(a.T, b)


def _f32_dot(a, b):
    return jnp.dot(a, b, preferred_element_type=F32)


def _split2(a):
    hi = a.astype(BF16)
    lo = (a - hi.astype(F32)).astype(BF16)
    return hi, lo


def _split3(a):
    hi = a.astype(BF16)
    r = a - hi.astype(F32)
    mid = r.astype(BF16)
    lo = (r - mid.astype(F32)).astype(BF16)
    return hi, mid, lo


def _dot_exact_lhs(a01, b):
    a = a01.astype(BF16)
    return sum(_f32_dot(a, part) for part in _split3(b))


def _dot_exact_rhs(a, b01):
    b = b01.astype(BF16)
    return sum(_f32_dot(part, b) for part in _split3(a))


def _sigmoid(x):
    return 1.0 / (1.0 + jnp.exp(-x))


def _silu(x):
    return x * _sigmoid(x)


def _softplus(x):
    return jnp.maximum(x, 0.0) + jnp.log(1.0 + jnp.exp(-jnp.abs(x)))


def _log_sigmoid(x):
    return -_softplus(-x)


def _rms(x, axis=-1):
    return x * lax.rsqrt(jnp.mean(x * x, axis=axis, keepdims=True) + EPS)


def _in_proj_kernel(x_ref, nw_ref, wbig_ref, wsmall_ref, big_ref, small_ref, xn_ref):
    j = pl.program_id(1)

    @pl.when(j == 0)
    def _():
        xn = _rms(x_ref[...]) * nw_ref[...]
        xn_ref[...] = xn.astype(BF16)
        small_ref[...] = jnp.dot(xn_ref[...], wsmall_ref[...], preferred_element_type=F32)

    big_ref[...] = jnp.dot(xn_ref[...], wbig_ref[j], preferred_element_type=F32)


def _in_proj(x2d, norm_w, w_big, w_small, *, tm):
    m, d = x2d.shape
    n_tiles, _, tn = w_big.shape
    n_big = n_tiles * tn
    assert m % tm == 0
    resident = dict(pipeline_mode=pl.Buffered(1))
    return pl.pallas_call(
        _in_proj_kernel,
        grid=(m // tm, n_tiles),
        in_specs=[
            pl.BlockSpec((tm, d), lambda i, j: (i, 0)),
            pl.BlockSpec((1, d), lambda i, j: (0, 0), **resident),
            pl.BlockSpec((n_tiles, d, tn), lambda i, j: (0, 0, 0), **resident),
            pl.BlockSpec((d, SMALL_COLS), lambda i, j: (0, 0), **resident),
        ],
        out_specs=[
            pl.BlockSpec((tm, tn), lambda i, j: (i, j)),
            pl.BlockSpec((tm, SMALL_COLS), lambda i, j: (i, 0)),
        ],
        out_shape=[
            jax.ShapeDtypeStruct((m, n_big), F32),
            jax.ShapeDtypeStruct((m, SMALL_COLS), F32),
        ],
        scratch_shapes=[pltpu.VMEM((tm, d), BF16)],
        compiler_params=pltpu.CompilerParams(
            dimension_semantics=("parallel", "arbitrary"), vmem_limit_bytes=VMEM_LIMIT_BYTES),
        name="in_proj",
    )(x2d, norm_w, w_big, w_small)


def _heads(fn, *xs):
    return jnp.stack([fn(*(x[h] for x in xs)) for h in range(xs[0].shape[0])])


def _heads_dot_hi(a, b):
    ah, al = _split2(a)
    bh, bl = _split2(b)
    return _heads(_f32_dot, ah, bh) + _heads(_f32_dot, ah, bl) + _heads(_f32_dot, al, bh)


def _interleave(stage_generators, stages_per_round):
    results = [None] * len(stage_generators)
    live = list(enumerate(stage_generators))
    while live:
        still = []
        for idx, gen in live:
            try:
                for _ in range(stages_per_round[idx]):
                    next(gen)
                still.append((idx, gen))
            except StopIteration as stop:
                results[idx] = stop.value
        live = still
    return results


def _inv_unit_lower(n):
    c = n.shape[-1]
    eye = (lax.broadcasted_iota(jnp.int32, (c, c), 0) == lax.broadcasted_iota(jnp.int32, (c, c), 1)).astype(F32)
    acc = eye - n
    power = n
    span = 2
    while span < c:
        ph, pl_ = _split2(power)
        if span > 2:
            ah, al = _split2(acc)
            acc = acc + (_heads(_f32_dot, ah, ph) + _heads(_f32_dot, ah, pl_) + _heads(_f32_dot, al, ph))
        power = _heads(_f32_dot, ph, ph) + _heads(_f32_dot, ph, pl_) + _heads(_f32_dot, pl_, ph)
        span *= 2
        yield
    if c > 2:
        acc = acc + _heads_dot_hi(acc, power)
    return acc


def _mlstm_chunk(q, k, v, qk, qc, i_col, b_col, i_row, b_row, causal, c_state, n_state, m_state):
    c = q.shape[1]
    a_col = b_col + m_state
    d = jnp.where(causal, b_col - b_row + i_row, NEG)
    mt = jnp.maximum(a_col, jnp.max(d, axis=2, keepdims=True))
    p = qk * jnp.exp(d - mt)
    w_st = jnp.exp(a_col - mt)
    yield
    num = _heads(_dot, p, v) + w_st * qc
    yield
    den = jnp.sum(p, axis=2, keepdims=True) + w_st * jnp.sum(q * n_state, axis=2, keepdims=True)
    h = num / jnp.maximum(jnp.abs(den), jnp.exp(-mt))
    m_new = mt[:, c - 1:c, :]
    b_last = b_col[:, c - 1:c, :]
    ws_col = jnp.exp(b_last - b_col + i_col - m_new)
    wc = jnp.exp(b_last + m_state - m_new)
    kw = k * ws_col
    yield
    c_new = wc * c_state + _heads(_dot_tn, kw, v)
    n_new = wc * n_state + jnp.sum(kw, axis=1, keepdims=True)
    return h, c_new, n_new, m_new


def _gdn_chunk(q, k, v, qk_kk, beta_col, g_col, g_row, incl, strict, s_state):
    c = q.shape[1]
    dh = v.shape[2]
    decay = jnp.exp(jnp.where(incl, g_col - g_row, NEG))
    qk = qk_kk[:, :c]
    kk = qk_kk[:, c:]
    n = jnp.where(strict, beta_col * kk * decay, 0.0)
    yield
    a_inv = yield from _inv_unit_lower(n)
    eg_col = jnp.exp(g_col)
    rhs = jnp.concatenate([beta_col * v, (beta_col * eg_col) * k], axis=2)
    yield
    sol = _heads_dot_hi(a_inv, rhs)
    u = sol[:, :, :dh]
    w = sol[:, :, dh:]
    yield
    wq_s = _heads(_dot, jnp.concatenate([w, q * eg_col], axis=1), s_state)
    v_new = u - wq_s[:, :c]
    yield
    o = wq_s[:, c:] + _heads(_dot, qk * decay, v_new)
    g_last = g_col[:, c - 1:c, :]
    yield
    s_new = jnp.exp(g_last) * s_state + _heads(_dot_tn, k * jnp.exp(g_last - g_col), v_new)
    return o, s_new


def _row_block_bcast(x, block, row):
    nh, c, l = x.shape
    x3 = x.reshape(nh * c // block, block, l)
    return jnp.broadcast_to(x3[:, row:row + 1, :], x3.shape).reshape(nh, c, l)


def _boundary_rows(x, hb):
    if 2 * hb >= SUBLANES:
        return _row_block_bcast(x, 2 * hb, hb - 1)
    sub = lax.broadcasted_iota(jnp.int32, (1, x.shape[1], 1), 1) % SUBLANES
    out = _row_block_bcast(x, SUBLANES, hb - 1)
    for start in range(2 * hb, SUBLANES, 2 * hb):
        out = jnp.where(sub >= start, _row_block_bcast(x, SUBLANES, start + hb - 1), out)
    return out


def _hgrn_levels(c):
    return [hb for hb in (1, 2, 4, 8, 16, 32, 64) if 2 * hb <= c]


def _hgrn_level_masks(c):
    t = lax.broadcasted_iota(jnp.int32, (c, c), 0)
    s = lax.broadcasted_iota(jnp.int32, (c, c), 1)
    masks = []
    for hb in _hgrn_levels(c):
        same = (t // (2 * hb)) == (s // (2 * hb))
        masks.append(same & ((t % (2 * hb)) >= hb) & ((s % (2 * hb)) < hb))
    return masks, t == s


def _hgrn_chunk(q, k, v, bc, level_masks, s_state):
    nh, c, dh = q.shape
    masks, diag = level_masks
    o = _heads(_dot, q * jnp.exp(bc), s_state)
    yield

    a = jnp.where(diag, jnp.sum(q * k, axis=2, keepdims=True), 0.0)
    for hb, mask in zip(_hgrn_levels(c), masks):
        mid = _boundary_rows(bc, hb)
        ql = q * jnp.exp(jnp.minimum(bc - mid, 0.0))
        kl = k * jnp.exp(jnp.minimum(mid - bc, 0.0))
        a = a + jnp.where(mask, _heads(_dot_nt, ql, kl), 0.0)
        yield
    o = o + _heads(_dot, a, v)
    yield

    b_last = bc[:, c - 1:c, :]
    decay_col = _heads(lambda r: jnp.broadcast_to(r, (SUBLANES, dh)).T[:, 0:1], jnp.exp(b_last))
    s_new = decay_col * s_state + _heads(_dot_tn, k * jnp.exp(b_last - bc), v)
    return o, s_new


def _mixers_kernel(*refs, layer, n_valid, has_state):
    if has_state:
        (big_ref, small_ref, bias_ref, alog_ref, mlnw_ref, convw_ref, gdnw_ref, lbl_ref, hgnw_ref,
         c0_ref, n0_ref, m0_ref, sg0_ref, cv0_ref, sh0_ref,
         ys_ref, c_ref, n_ref, m_ref, sg_ref, cv_ref, sh_ref, convbuf) = refs
    else:
        (big_ref, small_ref, bias_ref, alog_ref, mlnw_ref, convw_ref, gdnw_ref, lbl_ref, hgnw_ref,
         ys_ref, c_ref, n_ref, m_ref, sg_ref, cv_ref, sh_ref, convbuf) = refs
    j = pl.program_id(1)
    nj = pl.num_programs(1)
    bb, c = big_ref.shape[0], big_ref.shape[1]
    w = MIX_WIDTH
    dh = HEAD_DIM
    nh = N_HEADS
    tail = CONV_W - 1

    @pl.when(j == 0)
    def _():
        if has_state:
            c_ref[...] = c0_ref[...]
            n_ref[...] = n0_ref[...]
            m_ref[...] = m0_ref[...]
            sg_ref[...] = sg0_ref[...]
            sh_ref[...] = sh0_ref[...]
            convbuf[:, SUBLANES - tail:SUBLANES, :] = cv0_ref[...]
        else:
            c_ref[...] = jnp.zeros_like(c_ref)
            n_ref[...] = jnp.zeros_like(n_ref)
            m_ref[...] = jnp.zeros_like(m_ref)
            sg_ref[...] = jnp.zeros_like(sg_ref)
            sh_ref[...] = jnp.zeros_like(sh_ref)
            convbuf[:, SUBLANES - tail:SUBLANES, :] = jnp.zeros((bb, tail, N_BRANCH * w), F32)

    def seg(i):
        return big_ref[:, :, i * w:(i + 1) * w]

    def per_seq(fn, x):
        return jnp.stack([fn(x[b]) for b in range(bb)])

    def head_major(x):
        return jnp.stack([x[b][:, h * dh:(h + 1) * dh] for b in range(bb) for h in range(nh)])

    def token_major(x):
        return jnp.stack([jnp.concatenate([x[b * nh + h] for h in range(nh)], axis=1) for b in range(bb)])

    def cols(x, first):
        return jnp.stack([x[b][:, first + h:first + h + 1] for b in range(bb) for h in range(nh)])

    def rows(x, first):
        return jnp.stack([x[b][first + h:first + h + 1, :] for b in range(bb) for h in range(nh)])

    lane = lax.broadcasted_iota(jnp.int32, (1, c, SMALL_COLS), 2)
    rowi = lax.broadcasted_iota(jnp.int32, (1, c, SMALL_COLS), 1)
    pre = small_ref[...] + bias_ref[...]
    is_f = (lane >= SM_F) & (lane < SM_B)
    is_b = (lane >= SM_B) & (lane < SM_A)
    is_a = (lane >= SM_A) & (lane < SM_A + nh)
    logg = -jnp.exp(alog_ref[...]) * _softplus(pre)
    gates = jnp.where(is_f, _log_sigmoid(pre), jnp.where(is_b, _sigmoid(pre), jnp.where(is_a, logg, pre)))
    gates = jnp.where(rowi < n_valid, gates, jnp.where(lane < SM_F, NEG, 0.0))
    tt = lax.broadcasted_iota(jnp.int32, (c, c), 0)
    ss = lax.broadcasted_iota(jnp.int32, (c, c), 1)
    incl = tt >= ss
    strict = tt > ss
    ltri = incl.astype(F32)
    utri = (ss >= tt).astype(F32)
    cum_src = jnp.where(is_f | is_a, gates, 0.0)
    cums = per_seq(lambda x: _dot_exact_lhs(ltri, x), cum_src)
    gates_t = per_seq(lambda x: x.T, gates)
    cums_t = per_seq(lambda x: _dot_exact_rhs(x.T, utri), cum_src)

    lbl = lbl_ref[...]
    sm = jnp.exp(lbl - jnp.max(lbl, axis=0, keepdims=True))
    sm = sm / jnp.sum(sm, axis=0, keepdims=True)
    lb = jnp.zeros((1, w), F32)
    for l in range(1, layer + 1):
        lb = lb + sm[l:l + 1, :]
    valid = lax.broadcasted_iota(jnp.int32, (1, c, w), 1) < n_valid
    hf = seg(SEG_HG_F)
    g_f = lb + (1.0 - lb) * _sigmoid(hf)
    k_hg = jnp.where(valid, (1.0 - lb) * _sigmoid(-hf), 0.0)
    f_log = jnp.where(valid, jnp.log(g_f), 0.0)
    bc = per_seq(lambda x: _dot_exact_lhs(ltri, x), f_log)

    convbuf[:, SUBLANES:SUBLANES + c, :] = big_ref[:, :, SEG_GD_Q * w:(SEG_GD_V + 1) * w]
    conv = convw_ref[CONV_W - 1:CONV_W, :] * convbuf[:, SUBLANES:SUBLANES + c, :]
    for jj in range(CONV_W - 1):
        off = SUBLANES - tail + jj
        conv = conv + convw_ref[jj:jj + 1, :] * convbuf[:, off:off + c, :]
    conv = _silu(conv)
    new_tail = convbuf[:, SUBLANES + n_valid - tail:SUBLANES + n_valid, :]
    convbuf[:, SUBLANES - tail:SUBLANES, :] = new_tail
    cq = head_major(conv[:, :, 0:w])
    ck = head_major(conv[:, :, w:2 * w])
    cv = head_major(conv[:, :, 2 * w:3 * w])
    qn = cq * lax.rsqrt(jnp.sum(cq * cq, axis=2, keepdims=True) + EPS) * (dh ** -0.5)
    kn = ck * lax.rsqrt(jnp.sum(ck * ck, axis=2, keepdims=True) + EPS)
    qk_kk = _heads(_dot_nt, jnp.concatenate([qn, kn], axis=1), kn)

    ml_q = head_major(seg(SEG_ML_Q))
    ml_k = head_major(seg(SEG_ML_K)) * (dh ** -0.5)
    c_old = c_ref[...].reshape(bb * nh, dh, dh)
    ml_qk = _heads(_dot_nt, ml_q, ml_k)
    ml_qc = _heads(_dot, ml_q, c_old)
    m_old = jnp.stack([m_ref[b][:, h:h + 1] for b in range(bb) for h in range(nh)])

    mlstm = _mlstm_chunk(
        ml_q, ml_k, head_major(seg(SEG_ML_V)), ml_qk, ml_qc,
        cols(gates, SM_I), cols(cums, SM_F), rows(gates_t, SM_I), rows(cums_t, SM_F), incl,
        c_old, n_ref[...].reshape(bb * nh, 1, dh), m_old)
    gdn = _gdn_chunk(qn, kn, cv, qk_kk, cols(gates, SM_B), cols(cums, SM_A), rows(cums_t, SM_A),
                     incl, strict, sg_ref[...].reshape(bb * nh, dh, dh))
    hgrn = _hgrn_chunk(head_major(_silu(seg(SEG_HG_Q))), head_major(k_hg), head_major(seg(SEG_HG_I)),
                       head_major(bc), _hgrn_level_masks(c), sh_ref[...].reshape(bb * nh, dh, dh))
    (hh, c_new, n_new, m_new), (o_gd, sg_new), (o_hg, sh_new) = _interleave((mlstm, gdn, hgrn), (1, 3, 2))

    c_ref[...] = c_new.reshape(c_ref.shape)
    n_ref[...] = n_new.reshape(n_ref.shape)
    m_lane = lax.broadcasted_iota(jnp.int32, (1, SMALL_COLS), 1)
    m_rows = []
    for b in range(bb):
        m_row = jnp.zeros((1, SMALL_COLS), F32)
        for h in range(nh):
            m_row = jnp.where(m_lane == h, m_new[b * nh + h], m_row)
        m_rows.append(m_row)
    m_ref[...] = jnp.stack(m_rows)
    y_ml = token_major(_rms(hh)) * mlnw_ref[...] * _sigmoid(seg(SEG_ML_O))
    ys_ref[:, :, 0:w] = y_ml.astype(ys_ref.dtype)

    sg_ref[...] = sg_new.reshape(sg_ref.shape)
    y_gd = token_major(_rms(o_gd) * gdnw_ref[...]) * _silu(seg(SEG_GD_Z))
    ys_ref[:, :, w:2 * w] = y_gd.astype(ys_ref.dtype)

    sh_ref[...] = sh_new.reshape(sh_ref.shape)
    y_hg = _rms(token_major(o_hg)) * hgnw_ref[...] * _silu(seg(SEG_HG_G))
    ys_ref[:, :, 2 * w:3 * w] = y_hg.astype(ys_ref.dtype)

    @pl.when(j == nj - 1)
    def _():
        cv_ref[...] = convbuf[:, SUBLANES - tail:SUBLANES, :]


def _mixers(big3, small3, params, states_in, states_prev, *, layer, depth, chunk, n_valid, bb):
    b, t, n_big = big3.shape
    assert t % chunk == 0 and chunk % SUBLANES == 0 and CONV_W - 1 <= n_valid <= chunk and b % bb == 0
    assert n_valid == chunk or t == chunk
    w, dh, nh = MIX_WIDTH, HEAD_DIM, N_HEADS
    has_state = states_in is not None
    bias_row, alog_row, ml_norm_w, conv_w, gd_norm_w, lb_logits, hg_norm_w = params

    def const(shape):
        return pl.BlockSpec(shape, lambda i, j: (0,) * len(shape))

    def st(shape):
        return pl.BlockSpec((None, bb) + shape, lambda i, j: (layer, i) + (0,) * len(shape))

    state_shapes = [(nh, dh, dh), (nh, 1, dh), (1, SMALL_COLS), (nh, dh, dh), (CONV_W - 1, N_BRANCH * w), (nh, dh, dh)]
    in_specs = [
        pl.BlockSpec((bb, chunk, N_MIX_SEGS * w), lambda i, j: (i, j, 0)),
        pl.BlockSpec((bb, chunk, SMALL_COLS), lambda i, j: (i, j, 0)),
        const((1, SMALL_COLS)), const((1, SMALL_COLS)), const((1, w)), const((CONV_W, N_BRANCH * w)),
        const((1, dh)), const((depth, w)), const((1, w)),
    ]
    args = [big3, small3, bias_row, alog_row, ml_norm_w, conv_w, gd_norm_w, lb_logits, hg_norm_w]
    if has_state:
        in_specs += [st(s) for s in state_shapes]
        args += list(states_in)
    aliases = {}
    if states_prev is not None:
        for k_out, arr in enumerate(states_prev):
            aliases[len(args)] = 1 + k_out
            in_specs.append(pl.BlockSpec(memory_space=pl.ANY))
            args.append(arr)
    out_specs = [pl.BlockSpec((bb, chunk, N_BRANCH * w), lambda i, j: (i, j, 0))] + [st(s) for s in state_shapes]
    out_shape = [jax.ShapeDtypeStruct((b, t, N_BRANCH * w), BF16)] + [
        jax.ShapeDtypeStruct((depth, b) + s, F32) for s in state_shapes]

    def body(*refs):
        n_in = len(args) - (len(states_prev) if states_prev is not None else 0)
        kept = refs[:n_in] + refs[len(args):]
        _mixers_kernel(*kept, layer=layer, n_valid=n_valid, has_state=has_state)

    outs = pl.pallas_call(
        body,
        grid=(b // bb, t // chunk),
        in_specs=in_specs,
        out_specs=out_specs,
        out_shape=out_shape,
        scratch_shapes=[pltpu.VMEM((bb, SUBLANES + chunk, N_BRANCH * w), F32)],
        input_output_aliases=aliases,
        compiler_params=pltpu.CompilerParams(
            dimension_semantics=("parallel", "arbitrary"), vmem_limit_bytes=VMEM_LIMIT_BYTES),
        name="mixers",
    )(*args)
    return outs[0], tuple(outs[1:])


def _merge_ffn_kernel(x_ref, ys_ref, gate_ref, wbr_ref, wout_ref, n2_ref, wup_ref, wdown_ref, fin_ref,
                      out_ref, *, last):
    w = MIX_WIDTH
    d = x_ref.shape[1]
    merged = jnp.zeros(x_ref.shape, F32)
    for n in range(N_BRANCH):
        br = jnp.dot(ys_ref[:, n * w:(n + 1) * w], wbr_ref[n], preferred_element_type=F32)
        merged = merged + _sigmoid(gate_ref[:, n * d:(n + 1) * d]) * br
    x1 = x_ref[...] + jnp.dot(merged.astype(BF16), wout_ref[...], preferred_element_type=F32)
    xn = (_rms(x1) * n2_ref[...]).astype(BF16)
    hid = jnp.square(jnp.maximum(jnp.dot(xn, wup_ref[...], preferred_element_type=F32), 0.0))
    x2 = x1 + jnp.dot(hid.astype(BF16), wdown_ref[...], preferred_element_type=F32)
    if last:
        x2 = _rms(x2) * fin_ref[...]
    out_ref[...] = x2


def _merge_ffn(x2d, ys2d, big2d, w_branch, w_out, norm2_w, w_up, w_down, final_w, *, tm, last):
    m, d = x2d.shape
    w = MIX_WIDTH
    d_ff = w_up.shape[1]
    gate_block = N_BRANCH * d
    assert m % tm == 0 and (N_MIX_SEGS * w) % gate_block == 0
    gate_idx = (N_MIX_SEGS * w) // gate_block

    def const(shape):
        return pl.BlockSpec(shape, lambda i: (0,) * len(shape), pipeline_mode=pl.Buffered(1))

    return pl.pallas_call(
        functools.partial(_merge_ffn_kernel, last=last),
        grid=(m // tm,),
        in_specs=[
            pl.BlockSpec((tm, d), lambda i: (i, 0)),
            pl.BlockSpec((tm, N_BRANCH * w), lambda i: (i, 0)),
            pl.BlockSpec((tm, gate_block), lambda i: (i, gate_idx)),
            const((N_BRANCH, w, d)), const((d, d)), const((1, d)), const((d, d_ff)), const((d_ff, d)),
            const((1, d)),
        ],
        out_specs=pl.BlockSpec((tm, d), lambda i: (i, 0)),
        out_shape=jax.ShapeDtypeStruct((m, d), F32),
        compiler_params=pltpu.CompilerParams(
            dimension_semantics=("parallel",), vmem_limit_bytes=VMEM_LIMIT_BYTES),
        name="merge_ffn",
    )(x2d, ys2d, big2d, w_branch, w_out, norm2_w, w_up, w_down, final_w)


def _split_w_in(w_in_l):
    w, nh, d = MIX_WIDTH, N_HEADS, w_in_l.shape[0]
    sizes = [w] * 4 + [nh] * 2 + [w] * 4 + [nh] * 2 + [w] * 4 + [d] * N_BRANCH
    offs = [0]
    for s in sizes:
        offs.append(offs[-1] + s)
    cols = [w_in_l[:, offs[i]:offs[i + 1]] for i in range(len(sizes))]
    wide = [cols[i] for i in (0, 1, 2, 3, 6, 7, 8, 9, 12, 13, 14, 15, 16, 17, 18)]
    small = [cols[i] for i in (4, 5, 10, 11)]
    w_big = jnp.concatenate(wide, axis=1).astype(BF16)
    assert w_big.shape[1] % PROJ_TN == 0
    w_big = w_big.reshape(d, w_big.shape[1] // PROJ_TN, PROJ_TN).transpose(1, 0, 2)
    w_small = jnp.concatenate(small + [jnp.zeros((d, SMALL_COLS - 4 * nh), w_in_l.dtype)], axis=1).astype(BF16)
    return w_big, w_small


def _pad_lanes(parts, total):
    row = jnp.concatenate([p.reshape(1, -1).astype(F32) for p in parts], axis=1)
    return jnp.pad(row, ((0, 0), (0, total - row.shape[1])))


def _tiles(m):
    return math.gcd(m, 512), math.gcd(m, 256)


def _run_group(x3, states_in, n_valid, weights, final_norm_w, seqs_per_step):
    b, t, d = x3.shape
    depth = len(weights)
    chunk = min(t, CHUNK)
    x2d = x3.reshape(b * t, d)
    tm, tm_merge = _tiles(b * t)
    states_prev = None
    for l, wl in enumerate(weights):
        big, small = _in_proj(x2d, wl["norm1_w"], wl["w_big"], wl["w_small"], tm=tm)
        ys, states_prev = _mixers(
            big.reshape(b, t, -1), small.reshape(b, t, -1), wl["mix_params"], states_in, states_prev,
            layer=l, depth=depth, chunk=chunk, n_valid=n_valid, bb=math.gcd(b, seqs_per_step))
        x2d = _merge_ffn(x2d, ys.reshape(b * t, -1), big, wl["w_branch"], wl["w_out"], wl["norm2_w"],
                         wl["w_up"], wl["w_down"], final_norm_w, tm=tm_merge, last=(l == depth - 1))
    return x2d.reshape(b, t, d), states_prev


def kernel(x_prompt, x_sample, state_mlstm_C, state_mlstm_n, state_mlstm_m, state_gdn_S, state_gdn_conv, state_hgrn_S, norm1_w, w_in, ml_i_bias, ml_f_bias, ml_norm_w, gd_conv_w, gd_A_log, gd_dt_bias, gd_norm_w, hg_lb_logits, hg_norm_w, w_branch, w_out, norm2_w, w_up, w_down, final_norm_w):
    depth, d = norm1_w.shape
    nh = N_HEADS
    weights = []
    for l in range(depth):
        w_big, w_small = _split_w_in(w_in[l])
        zeros_h = jnp.zeros((nh,), F32)
        bias_row = _pad_lanes([ml_i_bias[l], ml_f_bias[l], zeros_h, gd_dt_bias[l]], SMALL_COLS)
        alog_row = _pad_lanes([zeros_h, zeros_h, zeros_h, gd_A_log[l]], SMALL_COLS)
        mix_params = (bias_row, alog_row, ml_norm_w[l].reshape(1, -1), gd_conv_w[l],
                      gd_norm_w[l].reshape(1, -1), hg_lb_logits, hg_norm_w[l].reshape(1, -1))
        weights.append(dict(
            norm1_w=norm1_w[l].reshape(1, d), w_big=w_big, w_small=w_small, mix_params=mix_params,
            w_branch=w_branch[l].astype(BF16), w_out=w_out[l].astype(BF16), norm2_w=norm2_w[l].reshape(1, d),
            w_up=w_up[l].astype(BF16), w_down=w_down[l].astype(BF16)))
    fin = final_norm_w.reshape(1, d)

    def unpack(states):
        s_c, s_n, s_m, s_g, s_cv, s_h = states
        return s_c, s_n.reshape(s_n.shape[:3] + s_n.shape[4:]), s_m[:, :, 0, :nh], s_g, s_cv, s_h

    assert x_prompt.shape[1] % CHUNK == 0
    y_prompt, p_states = _run_group(x_prompt, None, CHUNK, weights, fin, PROMPT_SEQS_PER_STEP)

    bs, ts, _ = x_sample.shape
    assert ts <= BF16_ROWS
    xs = jnp.pad(x_sample, ((0, 0), (0, BF16_ROWS - ts), (0, 0)))
    m_in = jnp.pad(state_mlstm_m, ((0, 0), (0, 0), (0, SMALL_COLS - nh))).reshape(depth, bs, 1, SMALL_COLS)
    n_in = state_mlstm_n.reshape(depth, bs, nh, 1, HEAD_DIM)
    s_in = (state_mlstm_C, n_in, m_in, state_gdn_S, state_gdn_conv, state_hgrn_S)
    y_s, s_states = _run_group(xs, s_in, ts, weights, fin, SAMPLE_SEQS_PER_STEP)
    y_sample = y_s[:, :ts, :]

    return (y_prompt, y_sample) + unpack(p_states) + unpack(s_states)
```

```python
import functools
import math

import jax
import jax.numpy as jnp
from jax import lax
from jax.experimental import pallas as pl
from jax.experimental.pallas import tpu as pltpu

F32 = jnp.float32
BF16 = jnp.bfloat16

HEAD_DIM = 128
N_HEADS = 4
MIX_WIDTH = N_HEADS * HEAD_DIM
N_BRANCH = 3
CONV_W = 4
EPS = 1e-6
CHUNK = 128
SUBLANES = 8
BF16_ROWS = 16
SMALL_COLS = 128
NEG = -1e30
VMEM_LIMIT_BYTES = 56 * 1024 * 1024
PROJ_TN = 1536
PROMPT_SEQS_PER_STEP = 2
SAMPLE_SEQS_PER_STEP = 8
STAGES_PER_ROUND = (1, 4, 2)

SEG_ML_Q, SEG_ML_K, SEG_ML_V, SEG_ML_O = 0, 1, 2, 3
SEG_GD_Q, SEG_GD_K, SEG_GD_V, SEG_GD_Z = 4, 5, 6, 7
SEG_HG_Q, SEG_HG_F, SEG_HG_I, SEG_HG_G = 8, 9, 10, 11
N_MIX_SEGS = 12
SM_I, SM_F, SM_B, SM_A = 0, N_HEADS, 2 * N_HEADS, 3 * N_HEADS


def _dot(a, b):
    return jnp.dot(a.astype(BF16), b.astype(BF16), preferred_element_type=F32)


def _dot_nt(a, b):
    return lax.dot_general(a.astype(BF16), b.astype(BF16), (((1,), (1,)), ((), ())),
                           preferred_element_type=F32)


def _dot_tn(a, b):
    return _dot(a.T, b)


def _f32_dot(a, b):
    return jnp.dot(a, b, preferred_element_type=F32)


def _split2(a):
    hi = a.astype(BF16)
    lo = (a - hi.astype(F32)).astype(BF16)
    return hi, lo


def _split3(a):
    hi = a.astype(BF16)
    r = a - hi.astype(F32)
    mid = r.astype(BF16)
    lo = (r - mid.astype(F32)).astype(BF16)
    return hi, mid, lo


def _dot_exact_lhs(a01, b):
    a = a01.astype(BF16)
    return sum(_f32_dot(a, part) for part in _split3(b))


def _dot_exact_rhs(a, b01):
    b = b01.astype(BF16)
    return sum(_f32_dot(part, b) for part in _split3(a))


def _sigmoid(x):
    return 1.0 / (1.0 + jnp.exp(-x))


def _silu(x):
    return x * _sigmoid(x)


def _softplus(x):
    return jnp.maximum(x, 0.0) + jnp.log(1.0 + jnp.exp(-jnp.abs(x)))


def _log_sigmoid(x):
    return -_softplus(-x)


def _rms(x, axis=-1):
    return x * lax.rsqrt(jnp.mean(x * x, axis=axis, keepdims=True) + EPS)


def _in_proj_kernel(x_ref, nw_ref, wbig_ref, wsmall_ref, big_ref, small_ref):
    xn = (_rms(x_ref[...]) * nw_ref[...]).astype(BF16)
    small_ref[...] = jnp.dot(xn, wsmall_ref[...], preferred_element_type=F32)
    n_big = wbig_ref.shape[1]
    for n0 in range(0, n_big, PROJ_TN):
        big_ref[:, n0:n0 + PROJ_TN] = jnp.dot(xn, wbig_ref[:, n0:n0 + PROJ_TN], preferred_element_type=F32)


def _in_proj(x2d, norm_w, w_big, w_small, *, layer, tm):
    m, d = x2d.shape
    n_big = w_big.shape[2]
    assert m % tm == 0 and n_big % PROJ_TN == 0
    resident = dict(pipeline_mode=pl.Buffered(1))
    return pl.pallas_call(
        _in_proj_kernel,
        grid=(m // tm,),
        in_specs=[
            pl.BlockSpec((tm, d), lambda i: (i, 0)),
            pl.BlockSpec((None, 1, d), lambda i: (layer, 0, 0), **resident),
            pl.BlockSpec((None, d, n_big), lambda i: (layer, 0, 0), **resident),
            pl.BlockSpec((None, d, SMALL_COLS), lambda i: (layer, 0, 0), **resident),
        ],
        out_specs=[
            pl.BlockSpec((tm, n_big), lambda i: (i, 0)),
            pl.BlockSpec((tm, SMALL_COLS), lambda i: (i, 0)),
        ],
        out_shape=[
            jax.ShapeDtypeStruct((m, n_big), F32),
            jax.ShapeDtypeStruct((m, SMALL_COLS), F32),
        ],
        compiler_params=pltpu.CompilerParams(
            dimension_semantics=("parallel",), vmem_limit_bytes=VMEM_LIMIT_BYTES),
        name="in_proj",
    )(x2d, norm_w, w_big, w_small)


def _heads(fn, *xs):
    return jnp.stack([fn(*(x[h] for x in xs)) for h in range(xs[0].shape[0])])


def _heads_dot_hi(a, b):
    ah, al = _split2(a)
    bh, bl = _split2(b)
    return _heads(_f32_dot, ah, bh) + _heads(_f32_dot, ah, bl) + _heads(_f32_dot, al, bh)


def _interleave(stage_generators, stages_per_round):
    results = [None] * len(stage_generators)
    live = list(enumerate(stage_generators))
    while live:
        still = []
        for idx, gen in live:
            try:
                for _ in range(stages_per_round[idx]):
                    next(gen)
                still.append((idx, gen))
            except StopIteration as stop:
                results[idx] = stop.value
        live = still
    return results


def _inv_unit_lower(n, level_masks):
    c = n.shape[-1]
    eye = (lax.broadcasted_iota(jnp.int32, (c, c), 0) == lax.broadcasted_iota(jnp.int32, (c, c), 1)).astype(F32)
    n_hi, n_lo = _split2(n)
    zero = jnp.zeros((), BF16)
    d = eye - jnp.where(level_masks[0], n, 0.0)
    for mask in level_masks[1:]:
        c_hi = jnp.where(mask, n_hi, zero)
        c_lo = jnp.where(mask, n_lo, zero)
        d_hi, d_lo = _split2(d)
        t = _heads(_f32_dot, d_hi, c_hi) + _heads(_f32_dot, d_hi, c_lo) + _heads(_f32_dot, d_lo, c_hi)
        yield
        t_hi, t_lo = _split2(t)
        d = d - (_heads(_f32_dot, t_hi, d_hi) + _heads(_f32_dot, t_hi, d_lo) + _heads(_f32_dot, t_lo, d_hi))
        yield
    return d


def _mlstm_chunk(q, k, v, qk, qc, i_col, b_col, i_row, b_row, causal, c_state, n_state, m_state):
    c = q.shape[1]
    a_col = b_col + m_state
    d = jnp.where(causal, b_col - b_row + i_row, NEG)
    mt = jnp.maximum(a_col, jnp.max(d, axis=2, keepdims=True))
    p = qk * jnp.exp(d - mt)
    w_st = jnp.exp(a_col - mt)
    yield
    num = _heads(_dot, p, v) + w_st * qc
    yield
    den = jnp.sum(p, axis=2, keepdims=True) + w_st * jnp.sum(q * n_state, axis=2, keepdims=True)
    h = num / jnp.maximum(jnp.abs(den), jnp.exp(-mt))
    m_new = mt[:, c - 1:c, :]
    b_last = b_col[:, c - 1:c, :]
    ws_col = jnp.exp(b_last - b_col + i_col - m_new)
    wc = jnp.exp(b_last + m_state - m_new)
    kw = k * ws_col
    yield
    c_new = wc * c_state + _heads(_dot_tn, kw, v)
    n_new = wc * n_state + jnp.sum(kw, axis=1, keepdims=True)
    return h, c_new, n_new, m_new


def _gdn_chunk(q, k, v, qk_kk, beta_col, g_col, g_row, incl, strict, level_masks, s_state):
    c = q.shape[1]
    dh = v.shape[2]
    decay = jnp.exp(jnp.where(incl, g_col - g_row, NEG))
    qk = qk_kk[:, :c]
    kk = qk_kk[:, c:]
    n = jnp.where(strict, beta_col * kk * decay, 0.0)
    yield
    a_inv = yield from _inv_unit_lower(n, level_masks)
    eg_col = jnp.exp(g_col)
    rhs = jnp.concatenate([beta_col * v, (beta_col * eg_col) * k], axis=2)
    yield
    sol = _heads_dot_hi(a_inv, rhs)
    u = sol[:, :, :dh]
    w = sol[:, :, dh:]
    yield
    wq_s = _heads(_dot, jnp.concatenate([w, q * eg_col], axis=1), s_state)
    v_new = u - wq_s[:, :c]
    yield
    o = wq_s[:, c:] + _heads(_dot, qk * decay, v_new)
    g_last = g_col[:, c - 1:c, :]
    yield
    s_new = jnp.exp(g_last) * s_state + _heads(_dot_tn, k * jnp.exp(g_last - g_col), v_new)
    return o, s_new


def _row_block_bcast(x, block, row):
    nh, c, l = x.shape
    x3 = x.reshape(nh * c // block, block, l)
    return jnp.broadcast_to(x3[:, row:row + 1, :], x3.shape).reshape(nh, c, l)


def _boundary_rows(x, hb):
    if 2 * hb >= SUBLANES:
        return _row_block_bcast(x, 2 * hb, hb - 1)
    sub = lax.broadcasted_iota(jnp.int32, (1, x.shape[1], 1), 1) % SUBLANES
    out = _row_block_bcast(x, SUBLANES, hb - 1)
    for start in range(2 * hb, SUBLANES, 2 * hb):
        out = jnp.where(sub >= start, _row_block_bcast(x, SUBLANES, start + hb - 1), out)
    return out


def _half_block_sizes(c):
    assert c & (c - 1) == 0
    return [1 << e for e in range(c.bit_length() - 1)]


def _level_masks(c):
    t = lax.broadcasted_iota(jnp.int32, (c, c), 0)
    s = lax.broadcasted_iota(jnp.int32, (c, c), 1)
    masks = []
    for hb in _half_block_sizes(c):
        same = (t // (2 * hb)) == (s // (2 * hb))
        masks.append(same & ((t % (2 * hb)) >= hb) & ((s % (2 * hb)) < hb))
    return masks


def _hgrn_chunk(q, k, v, bc, level_masks, s_state):
    nh, c, dh = q.shape
    diag = lax.broadcasted_iota(jnp.int32, (c, c), 0) == lax.broadcasted_iota(jnp.int32, (c, c), 1)
    o = _heads(_dot, q * jnp.exp(bc), s_state)
    yield

    a = jnp.where(diag, jnp.sum(q * k, axis=2, keepdims=True), 0.0)
    for hb, mask in zip(_half_block_sizes(c), level_masks):
        e = jnp.exp(-jnp.abs(bc - _boundary_rows(bc, hb)))
        a = a + jnp.where(mask, _heads(_dot_nt, q * e, k * e), 0.0)
        yield
    o = o + _heads(_dot, a, v)
    yield

    b_last = bc[:, c - 1:c, :]
    decay_col = _heads(lambda r: jnp.broadcast_to(r, (SUBLANES, dh)).T[:, 0:1], jnp.exp(b_last))
    s_new = decay_col * s_state + _heads(_dot_tn, k * jnp.exp(b_last - bc), v)
    return o, s_new


def _mixers_kernel(*refs, layer, n_valid, has_state):
    if has_state:
        (big_ref, small_ref, bias_ref, alog_ref, mlnw_ref, convw_ref, gdnw_ref, lbl_ref, hgnw_ref,
         c0_ref, n0_ref, m0_ref, sg0_ref, cv0_ref, sh0_ref,
         ys_ref, c_ref, n_ref, m_ref, sg_ref, cv_ref, sh_ref, convbuf) = refs
    else:
        (big_ref, small_ref, bias_ref, alog_ref, mlnw_ref, convw_ref, gdnw_ref, lbl_ref, hgnw_ref,
         ys_ref, c_ref, n_ref, m_ref, sg_ref, cv_ref, sh_ref, convbuf) = refs
    j = pl.program_id(1)
    nj = pl.num_programs(1)
    bb, c = big_ref.shape[0], big_ref.shape[1]
    w = MIX_WIDTH
    dh = HEAD_DIM
    nh = N_HEADS
    tail = CONV_W - 1

    @pl.when(j == 0)
    def _():
        if has_state:
            c_ref[...] = c0_ref[...]
            n_ref[...] = n0_ref[...]
            m_ref[...] = m0_ref[...]
            sg_ref[...] = sg0_ref[...]
            sh_ref[...] = sh0_ref[...]
            convbuf[:, SUBLANES - tail:SUBLANES, :] = cv0_ref[...]
        else:
            c_ref[...] = jnp.zeros_like(c_ref)
            n_ref[...] = jnp.zeros_like(n_ref)
            m_ref[...] = jnp.zeros_like(m_ref)
            sg_ref[...] = jnp.zeros_like(sg_ref)
            sh_ref[...] = jnp.zeros_like(sh_ref)
            convbuf[:, SUBLANES - tail:SUBLANES, :] = jnp.zeros((bb, tail, N_BRANCH * w), F32)

    def seg(i):
        return big_ref[:, :, i * w:(i + 1) * w]

    def per_seq(fn, x):
        return jnp.stack([fn(x[b]) for b in range(bb)])

    def head_major(x):
        return jnp.stack([x[b][:, h * dh:(h + 1) * dh] for b in range(bb) for h in range(nh)])

    def token_major(x):
        return jnp.stack([jnp.concatenate([x[b * nh + h] for h in range(nh)], axis=1) for b in range(bb)])

    def cols(x, first):
        return jnp.stack([x[b][:, first + h:first + h + 1] for b in range(bb) for h in range(nh)])

    def rows(x, first):
        return jnp.stack([x[b][first + h:first + h + 1, :] for b in range(bb) for h in range(nh)])

    lane = lax.broadcasted_iota(jnp.int32, (1, c, SMALL_COLS), 2)
    rowi = lax.broadcasted_iota(jnp.int32, (1, c, SMALL_COLS), 1)
    pre = small_ref[...] + bias_ref[...]
    is_f = (lane >= SM_F) & (lane < SM_B)
    is_b = (lane >= SM_B) & (lane < SM_A)
    is_a = (lane >= SM_A) & (lane < SM_A + nh)
    logg = -jnp.exp(alog_ref[...]) * _softplus(pre)
    gates = jnp.where(is_f, _log_sigmoid(pre), jnp.where(is_b, _sigmoid(pre), jnp.where(is_a, logg, pre)))
    gates = jnp.where(rowi < n_valid, gates, jnp.where(lane < SM_F, NEG, 0.0))
    tt = lax.broadcasted_iota(jnp.int32, (c, c), 0)
    ss = lax.broadcasted_iota(jnp.int32, (c, c), 1)
    incl = tt >= ss
    strict = tt > ss
    ltri = incl.astype(F32)
    utri = (ss >= tt).astype(F32)
    cum_src = jnp.where(is_f | is_a, gates, 0.0)
    cums = per_seq(lambda x: _dot_exact_lhs(ltri, x), cum_src)
    gates_t = per_seq(lambda x: x.T, gates)
    cums_t = per_seq(lambda x: _dot_exact_rhs(x.T, utri), cum_src)

    lbl = lbl_ref[...]
    sm = jnp.exp(lbl - jnp.max(lbl, axis=0, keepdims=True))
    sm = sm / jnp.sum(sm, axis=0, keepdims=True)
    lb = jnp.zeros((1, w), F32)
    for l in range(1, layer + 1):
        lb = lb + sm[l:l + 1, :]
    valid = lax.broadcasted_iota(jnp.int32, (1, c, w), 1) < n_valid
    hf = seg(SEG_HG_F)
    g_f = lb + (1.0 - lb) * _sigmoid(hf)
    k_hg = jnp.where(valid, (1.0 - lb) * _sigmoid(-hf), 0.0)
    f_log = jnp.where(valid, jnp.log(g_f), 0.0)
    bc = per_seq(lambda x: _dot_exact_lhs(ltri, x), f_log)

    convbuf[:, SUBLANES:SUBLANES + c, :] = big_ref[:, :, SEG_GD_Q * w:(SEG_GD_V + 1) * w]
    conv = convw_ref[CONV_W - 1:CONV_W, :] * convbuf[:, SUBLANES:SUBLANES + c, :]
    for jj in range(CONV_W - 1):
        off = SUBLANES - tail + jj
        conv = conv + convw_ref[jj:jj + 1, :] * convbuf[:, off:off + c, :]
    conv = _silu(conv)
    new_tail = convbuf[:, SUBLANES + n_valid - tail:SUBLANES + n_valid, :]
    convbuf[:, SUBLANES - tail:SUBLANES, :] = new_tail
    cq = head_major(conv[:, :, 0:w])
    ck = head_major(conv[:, :, w:2 * w])
    cv = head_major(conv[:, :, 2 * w:3 * w])
    qn = cq * lax.rsqrt(jnp.sum(cq * cq, axis=2, keepdims=True) + EPS) * (dh ** -0.5)
    kn = ck * lax.rsqrt(jnp.sum(ck * ck, axis=2, keepdims=True) + EPS)
    qk_kk = _heads(_dot_nt, jnp.concatenate([qn, kn], axis=1), kn)

    ml_q = head_major(seg(SEG_ML_Q))
    ml_k = head_major(seg(SEG_ML_K)) * (dh ** -0.5)
    c_old = c_ref[...].reshape(bb * nh, dh, dh)
    ml_qk = _heads(_dot_nt, ml_q, ml_k)
    ml_qc = _heads(_dot, ml_q, c_old)
    m_old = jnp.stack([m_ref[b][:, h:h + 1] for b in range(bb) for h in range(nh)])

    mlstm = _mlstm_chunk(
        ml_q, ml_k, head_major(seg(SEG_ML_V)), ml_qk, ml_qc,
        cols(gates, SM_I), cols(cums, SM_F), rows(gates_t, SM_I), rows(cums_t, SM_F), incl,
        c_old, n_ref[...].reshape(bb * nh, 1, dh), m_old)
    level_masks = _level_masks(c)
    gdn = _gdn_chunk(qn, kn, cv, qk_kk, cols(gates, SM_B), cols(cums, SM_A), rows(cums_t, SM_A),
                     incl, strict, level_masks, sg_ref[...].reshape(bb * nh, dh, dh))
    hgrn = _hgrn_chunk(head_major(_silu(seg(SEG_HG_Q))), head_major(k_hg), head_major(seg(SEG_HG_I)),
                       head_major(bc), level_masks, sh_ref[...].reshape(bb * nh, dh, dh))
    (hh, c_new, n_new, m_new), (o_gd, sg_new), (o_hg, sh_new) = _interleave(
        (mlstm, gdn, hgrn), STAGES_PER_ROUND)

    c_ref[...] = c_new.reshape(c_ref.shape)
    n_ref[...] = n_new.reshape(n_ref.shape)
    m_lane = lax.broadcasted_iota(jnp.int32, (1, SMALL_COLS), 1)
    m_rows = []
    for b in range(bb):
        m_row = jnp.zeros((1, SMALL_COLS), F32)
        for h in range(nh):
            m_row = jnp.where(m_lane == h, m_new[b * nh + h], m_row)
        m_rows.append(m_row)
    m_ref[...] = jnp.stack(m_rows)
    y_ml = token_major(_rms(hh)) * mlnw_ref[...] * _sigmoid(seg(SEG_ML_O))
    ys_ref[:, :, 0:w] = y_ml.astype(ys_ref.dtype)

    sg_ref[...] = sg_new.reshape(sg_ref.shape)
    y_gd = token_major(_rms(o_gd) * gdnw_ref[...]) * _silu(seg(SEG_GD_Z))
    ys_ref[:, :, w:2 * w] = y_gd.astype(ys_ref.dtype)

    sh_ref[...] = sh_new.reshape(sh_ref.shape)
    y_hg = _rms(token_major(o_hg)) * hgnw_ref[...] * _silu(seg(SEG_HG_G))
    ys_ref[:, :, 2 * w:3 * w] = y_hg.astype(ys_ref.dtype)

    @pl.when(j == nj - 1)
    def _():
        cv_ref[...] = convbuf[:, SUBLANES - tail:SUBLANES, :]


def _mixers(big3, small3, params, states_in, states_prev, *, layer, depth, chunk, n_valid, bb):
    b, t, n_big = big3.shape
    assert t % chunk == 0 and chunk % SUBLANES == 0 and CONV_W - 1 <= n_valid <= chunk and b % bb == 0
    assert n_valid == chunk or t == chunk
    w, dh, nh = MIX_WIDTH, HEAD_DIM, N_HEADS
    has_state = states_in is not None
    bias_row, alog_row, ml_norm_w, conv_w, gd_norm_w, lb_logits, hg_norm_w = params

    def const(shape):
        return pl.BlockSpec(shape, lambda i, j: (0,) * len(shape))

    def st(shape):
        return pl.BlockSpec((None, bb) + shape, lambda i, j: (layer, i) + (0,) * len(shape))

    state_shapes = [(nh, dh, dh), (nh, 1, dh), (1, SMALL_COLS), (nh, dh, dh), (CONV_W - 1, N_BRANCH * w), (nh, dh, dh)]
    in_specs = [
        pl.BlockSpec((bb, chunk, N_MIX_SEGS * w), lambda i, j: (i, j, 0)),
        pl.BlockSpec((bb, chunk, SMALL_COLS), lambda i, j: (i, j, 0)),
        const((1, SMALL_COLS)), const((1, SMALL_COLS)), const((1, w)), const((CONV_W, N_BRANCH * w)),
        const((1, dh)), const((depth, w)), const((1, w)),
    ]
    args = [big3, small3, bias_row, alog_row, ml_norm_w, conv_w, gd_norm_w, lb_logits, hg_norm_w]
    if has_state:
        in_specs += [st(s) for s in state_shapes]
        args += list(states_in)
    aliases = {}
    if states_prev is not None:
        for k_out, arr in enumerate(states_prev):
            aliases[len(args)] = 1 + k_out
            in_specs.append(pl.BlockSpec(memory_space=pl.ANY))
            args.append(arr)
    out_specs = [pl.BlockSpec((bb, chunk, N_BRANCH * w), lambda i, j: (i, j, 0))] + [st(s) for s in state_shapes]
    out_shape = [jax.ShapeDtypeStruct((b, t, N_BRANCH * w), BF16)] + [
        jax.ShapeDtypeStruct((depth, b) + s, F32) for s in state_shapes]

    def body(*refs):
        n_in = len(args) - (len(states_prev) if states_prev is not None else 0)
        kept = refs[:n_in] + refs[len(args):]
        _mixers_kernel(*kept, layer=layer, n_valid=n_valid, has_state=has_state)

    outs = pl.pallas_call(
        body,
        grid=(b // bb, t // chunk),
        in_specs=in_specs,
        out_specs=out_specs,
        out_shape=out_shape,
        scratch_shapes=[pltpu.VMEM((bb, SUBLANES + chunk, N_BRANCH * w), F32)],
        input_output_aliases=aliases,
        compiler_params=pltpu.CompilerParams(
            dimension_semantics=("parallel", "arbitrary"), vmem_limit_bytes=VMEM_LIMIT_BYTES),
        name="mixers",
    )(*args)
    return outs[0], tuple(outs[1:])


def _merge_ffn_kernel(x_ref, ys_ref, gate_ref, wbr_ref, wout_ref, n2_ref, wup_ref, wdown_ref, fin_ref,
                      out_ref, *, last):
    w = MIX_WIDTH
    d = x_ref.shape[1]
    merged = jnp.zeros(x_ref.shape, F32)
    for n in range(N_BRANCH):
        br = jnp.dot(ys_ref[:, n * w:(n + 1) * w], wbr_ref[n], preferred_element_type=F32)
        merged = merged + _sigmoid(gate_ref[:, n * d:(n + 1) * d]) * br
    x1 = x_ref[...] + jnp.dot(merged.astype(BF16), wout_ref[...], preferred_element_type=F32)
    xn = (_rms(x1) * n2_ref[...]).astype(BF16)
    hid = jnp.square(jnp.maximum(jnp.dot(xn, wup_ref[...], preferred_element_type=F32), 0.0))
    x2 = x1 + jnp.dot(hid.astype(BF16), wdown_ref[...], preferred_element_type=F32)
    if last:
        x2 = _rms(x2) * fin_ref[...]
    out_ref[...] = x2


def _merge_ffn(x2d, ys2d, big2d, w_branch, w_out, norm2_w, w_up, w_down, final_w, *, layer, tm, last):
    m, d = x2d.shape
    w = MIX_WIDTH
    d_ff = w_up.shape[2]
    gate_block = N_BRANCH * d
    assert m % tm == 0 and (N_MIX_SEGS * w) % gate_block == 0
    gate_idx = (N_MIX_SEGS * w) // gate_block

    def per_layer(shape):
        return pl.BlockSpec((None,) + shape, lambda i: (layer,) + (0,) * len(shape), pipeline_mode=pl.Buffered(1))

    return pl.pallas_call(
        functools.partial(_merge_ffn_kernel, last=last),
        grid=(m // tm,),
        in_specs=[
            pl.BlockSpec((tm, d), lambda i: (i, 0)),
            pl.BlockSpec((tm, N_BRANCH * w), lambda i: (i, 0)),
            pl.BlockSpec((tm, gate_block), lambda i: (i, gate_idx)),
            per_layer((N_BRANCH, w, d)), per_layer((d, d)), per_layer((1, d)), per_layer((d, d_ff)),
            per_layer((d_ff, d)),
            pl.BlockSpec((1, d), lambda i: (0, 0), pipeline_mode=pl.Buffered(1)),
        ],
        out_specs=pl.BlockSpec((tm, d), lambda i: (i, 0)),
        out_shape=jax.ShapeDtypeStruct((m, d), F32),
        compiler_params=pltpu.CompilerParams(
            dimension_semantics=("parallel",), vmem_limit_bytes=VMEM_LIMIT_BYTES),
        name="merge_ffn",
    )(x2d, ys2d, big2d, w_branch, w_out, norm2_w, w_up, w_down, final_w)


def _split_w_in(w_in):
    w, nh = MIX_WIDTH, N_HEADS
    a0, a1 = 4 * w, 4 * w + 2 * nh
    b0, b1 = a1 + 4 * w, a1 + 4 * w + 2 * nh
    w_big = jnp.concatenate([w_in[:, :, :a0], w_in[:, :, a1:b0], w_in[:, :, b1:]], axis=2).astype(BF16)
    small = jnp.concatenate([w_in[:, :, a0:a1], w_in[:, :, b0:b1]], axis=2)
    w_small = jnp.pad(small, ((0, 0), (0, 0), (0, SMALL_COLS - 4 * nh))).astype(BF16)
    return w_big, w_small


def _pad_lanes(parts, total):
    row = jnp.concatenate([p.reshape(1, -1).astype(F32) for p in parts], axis=1)
    return jnp.pad(row, ((0, 0), (0, total - row.shape[1])))


def _tiles(m):
    return math.gcd(m, 256), math.gcd(m, 256)


def _run_group(x3, states_in, n_valid, weights, mix_params, seqs_per_step):
    b, t, d = x3.shape
    depth = len(mix_params)
    chunk = min(t, CHUNK)
    x2d = x3.reshape(b * t, d)
    tm, tm_merge = _tiles(b * t)
    states_prev = None
    for l in range(depth):
        big, small = _in_proj(x2d, weights["norm1_w"], weights["w_big"], weights["w_small"], layer=l, tm=tm)
        ys, states_prev = _mixers(
            big.reshape(b, t, -1), small.reshape(b, t, -1), mix_params[l], states_in, states_prev,
            layer=l, depth=depth, chunk=chunk, n_valid=n_valid, bb=math.gcd(b, seqs_per_step))
        x2d = _merge_ffn(x2d, ys.reshape(b * t, -1), big, weights["w_branch"], weights["w_out"],
                         weights["norm2_w"], weights["w_up"], weights["w_down"], weights["final_norm_w"],
                         layer=l, tm=tm_merge, last=(l == depth - 1))
    return x2d.reshape(b, t, d), states_prev


def kernel(x_prompt, x_sample, state_mlstm_C, state_mlstm_n, state_mlstm_m, state_gdn_S, state_gdn_conv, state_hgrn_S, norm1_w, w_in, ml_i_bias, ml_f_bias, ml_norm_w, gd_conv_w, gd_A_log, gd_dt_bias, gd_norm_w, hg_lb_logits, hg_norm_w, w_branch, w_out, norm2_w, w_up, w_down, final_norm_w):
    depth, d = norm1_w.shape
    nh = N_HEADS
    w_big, w_small = _split_w_in(w_in)
    weights = dict(
        norm1_w=norm1_w.reshape(depth, 1, d), w_big=w_big, w_small=w_small,
        w_branch=w_branch.astype(BF16), w_out=w_out.astype(BF16), norm2_w=norm2_w.reshape(depth, 1, d),
        w_up=w_up.astype(BF16), w_down=w_down.astype(BF16), final_norm_w=final_norm_w.reshape(1, d))
    mix_params = []
    for l in range(depth):
        zeros_h = jnp.zeros((nh,), F32)
        bias_row = _pad_lanes([ml_i_bias[l], ml_f_bias[l], zeros_h, gd_dt_bias[l]], SMALL_COLS)
        alog_row = _pad_lanes([zeros_h, zeros_h, zeros_h, gd_A_log[l]], SMALL_COLS)
        mix_params.append((bias_row, alog_row, ml_norm_w[l].reshape(1, -1), gd_conv_w[l],
                           gd_norm_w[l].reshape(1, -1), hg_lb_logits, hg_norm_w[l].reshape(1, -1)))

    def unpack(states):
        s_c, s_n, s_m, s_g, s_cv, s_h = states
        return s_c, s_n.reshape(s_n.shape[:3] + s_n.shape[4:]), s_m[:, :, 0, :nh], s_g, s_cv, s_h

    assert x_prompt.shape[1] % CHUNK == 0
    y_prompt, p_states = _run_group(x_prompt, None, CHUNK, weights, mix_params, PROMPT_SEQS_PER_STEP)

    bs, ts, _ = x_sample.shape
    assert ts <= BF16_ROWS
    xs = jnp.pad(x_sample, ((0, 0), (0, BF16_ROWS - ts), (0, 0)))
    m_in = jnp.pad(state_mlstm_m, ((0, 0), (0, 0), (0, SMALL_COLS - nh))).reshape(depth, bs, 1, SMALL_COLS)
    n_in = state_mlstm_n.reshape(depth, bs, nh, 1, HEAD_DIM)
    s_in = (state_mlstm_C, n_in, m_in, state_gdn_S, state_gdn_conv, state_hgrn_S)
    y_s, s_states = _run_group(xs, s_in, ts, weights, mix_params, SAMPLE_SEQS_PER_STEP)
    y_sample = y_s[:, :ts, :]

    return (y_prompt, y_sample) + unpack(p_states) + unpack(s_states)
```

```python
import functools
import math

import jax
import jax.numpy as jnp
from jax import lax
from jax.experimental import pallas as pl
from jax.experimental.pallas import tpu as pltpu

F32 = jnp.float32
BF16 = jnp.bfloat16

HEAD_DIM = 128
N_HEADS = 4
MIX_WIDTH = N_HEADS * HEAD_DIM
N_BRANCH = 3
CONV_W = 4
EPS = 1e-6
CHUNK = 128
SUBLANES = 8
BF16_ROWS = 16
SMALL_COLS = 128
NEG = -1e30
VMEM_LIMIT_BYTES = 56 * 1024 * 1024
PROJ_TN = 1536
PROMPT_SEQS_PER_STEP = 2
SAMPLE_SEQS_PER_STEP = 8
STAGES_PER_ROUND = (1, 4, 2)

SEG_ML_Q, SEG_ML_K, SEG_ML_V, SEG_ML_O = 0, 1, 2, 3
SEG_GD_Q, SEG_GD_K, SEG_GD_V, SEG_GD_Z = 4, 5, 6, 7
SEG_HG_Q, SEG_HG_F, SEG_HG_I, SEG_HG_G = 8, 9, 10, 11
N_MIX_SEGS = 12
SM_I, SM_F, SM_B, SM_A = 0, N_HEADS, 2 * N_HEADS, 3 * N_HEADS


def _dot(a, b):
    return jnp.dot(a.astype(BF16), b.astype(BF16), preferred_element_type=F32)


def _dot_nt(a, b):
    return lax.dot_general(a.astype(BF16), b.astype(BF16), (((1,), (1,)), ((), ())),
                           preferred_element_type=F32)


def _dot_tn(a, b):
    return _dot(a.T, b)


def _f32_dot(a, b):
    return jnp.dot(a, b, preferred_element_type=F32)


def _split2(a):
    hi = a.astype(BF16)
    lo = (a - hi.astype(F32)).astype(BF16)
    return hi, lo


def _split3(a):
    hi = a.astype(BF16)
    r = a - hi.astype(F32)
    mid = r.astype(BF16)
    lo = (r - mid.astype(F32)).astype(BF16)
    return hi, mid, lo


def _dot_exact_lhs(a01, b):
    a = a01.astype(BF16)
    return sum(_f32_dot(a, part) for part in _split3(b))


def _dot_exact_rhs(a, b01):
    b = b01.astype(BF16)
    return sum(_f32_dot(part, b) for part in _split3(a))


def _sigmoid(x):
    return 1.0 / (1.0 + jnp.exp(-x))


def _silu(x):
    return x * _sigmoid(x)


def _softplus(x):
    return jnp.maximum(x, 0.0) + jnp.log(1.0 + jnp.exp(-jnp.abs(x)))


def _log_sigmoid(x):
    return -_softplus(-x)


def _rms(x, axis=-1):
    return x * lax.rsqrt(jnp.mean(x * x, axis=axis, keepdims=True) + EPS)


def _in_proj_kernel(x_ref, nw_ref, wbig_ref, wsmall_ref, big_ref, small_ref):
    xn = (_rms(x_ref[...]) * nw_ref[...]).astype(BF16)
    small_ref[...] = jnp.dot(xn, wsmall_ref[...], preferred_element_type=F32)
    n_big = wbig_ref.shape[1]
    for n0 in range(0, n_big, PROJ_TN):
        big_ref[:, n0:n0 + PROJ_TN] = jnp.dot(xn, wbig_ref[:, n0:n0 + PROJ_TN], preferred_element_type=F32)


def _in_proj(x2d, norm_w, w_big, w_small, *, layer, tm):
    m, d = x2d.shape
    n_big = w_big.shape[2]
    assert m % tm == 0 and n_big % PROJ_TN == 0
    resident = dict(pipeline_mode=pl.Buffered(1))
    return pl.pallas_call(
        _in_proj_kernel,
        grid=(m // tm,),
        in_specs=[
            pl.BlockSpec((tm, d), lambda i: (i, 0)),
            pl.BlockSpec((None, 1, d), lambda i: (layer, 0, 0), **resident),
            pl.BlockSpec((None, d, n_big), lambda i: (layer, 0, 0), **resident),
            pl.BlockSpec((None, d, SMALL_COLS), lambda i: (layer, 0, 0), **resident),
        ],
        out_specs=[
            pl.BlockSpec((tm, n_big), lambda i: (i, 0)),
            pl.BlockSpec((tm, SMALL_COLS), lambda i: (i, 0)),
        ],
        out_shape=[
            jax.ShapeDtypeStruct((m, n_big), F32),
            jax.ShapeDtypeStruct((m, SMALL_COLS), F32),
        ],
        compiler_params=pltpu.CompilerParams(
            dimension_semantics=("parallel",), vmem_limit_bytes=VMEM_LIMIT_BYTES),
        name="in_proj",
    )(x2d, norm_w, w_big, w_small)


def _heads(fn, *xs):
    return jnp.stack([fn(*(x[h] for x in xs)) for h in range(xs[0].shape[0])])


def _heads_dot_hi(a, b):
    ah, al = _split2(a)
    bh, bl = _split2(b)
    return _heads(_f32_dot, ah, bh) + _heads(_f32_dot, ah, bl) + _heads(_f32_dot, al, bh)


def _interleave(stage_generators, stages_per_round):
    results = [None] * len(stage_generators)
    live = list(enumerate(stage_generators))
    while live:
        still = []
        for idx, gen in live:
            try:
                for _ in range(stages_per_round[idx]):
                    next(gen)
                still.append((idx, gen))
            except StopIteration as stop:
                results[idx] = stop.value
        live = still
    return results


def _inv_unit_lower(n, level_masks):
    c = n.shape[-1]
    eye = (lax.broadcasted_iota(jnp.int32, (c, c), 0) == lax.broadcasted_iota(jnp.int32, (c, c), 1)).astype(F32)
    n_hi, n_lo = _split2(n)
    zero = jnp.zeros((), BF16)
    d = eye - jnp.where(level_masks[0], n, 0.0)
    for mask in level_masks[1:]:
        c_hi = jnp.where(mask, n_hi, zero)
        c_lo = jnp.where(mask, n_lo, zero)
        d_hi, d_lo = _split2(d)
        t = _heads(_f32_dot, d_hi, c_hi) + _heads(_f32_dot, d_hi, c_lo) + _heads(_f32_dot, d_lo, c_hi)
        yield
        t_hi, t_lo = _split2(t)
        d = d - (_heads(_f32_dot, t_hi, d_hi) + _heads(_f32_dot, t_hi, d_lo) + _heads(_f32_dot, t_lo, d_hi))
        yield
    return d


def _mlstm_chunk(q, k, v, qk, qc, i_col, b_col, i_row, b_row, causal, c_state, n_state, m_state):
    c = q.shape[1]
    a_col = b_col + m_state
    d = jnp.where(causal, b_col - b_row + i_row, NEG)
    mt = jnp.maximum(a_col, jnp.max(d, axis=2, keepdims=True))
    p = qk * jnp.exp(d - mt)
    w_st = jnp.exp(a_col - mt)
    yield
    num = _heads(_dot, p, v) + w_st * qc
    yield
    den = jnp.sum(p, axis=2, keepdims=True) + w_st * jnp.sum(q * n_state, axis=2, keepdims=True)
    h = num / jnp.maximum(jnp.abs(den), jnp.exp(-mt))
    m_new = mt[:, c - 1:c, :]
    b_last = b_col[:, c - 1:c, :]
    ws_col = jnp.exp(b_last - b_col + i_col - m_new)
    wc = jnp.exp(b_last + m_state - m_new)
    kw = k * ws_col
    yield
    c_new = wc * c_state + _heads(_dot_tn, kw, v)
    n_new = wc * n_state + jnp.sum(kw, axis=1, keepdims=True)
    return h, c_new, n_new, m_new


def _gdn_chunk(q, k, v, qk_kk, beta_col, g_col, g_row, incl, strict, level_masks, s_state):
    c = q.shape[1]
    dh = v.shape[2]
    decay = jnp.exp(jnp.where(incl, g_col - g_row, NEG))
    qk = qk_kk[:, :c]
    kk = qk_kk[:, c:]
    n = jnp.where(strict, beta_col * kk * decay, 0.0)
    yield
    a_inv = yield from _inv_unit_lower(n, level_masks)
    eg_col = jnp.exp(g_col)
    rhs = jnp.concatenate([beta_col * v, (beta_col * eg_col) * k], axis=2)
    yield
    sol = _heads_dot_hi(a_inv, rhs)
    u = sol[:, :, :dh]
    w = sol[:, :, dh:]
    yield
    wq_s = _heads(_dot, jnp.concatenate([w, q * eg_col], axis=1), s_state)
    v_new = u - wq_s[:, :c]
    yield
    o = wq_s[:, c:] + _heads(_dot, qk * decay, v_new)
    g_last = g_col[:, c - 1:c, :]
    yield
    s_new = jnp.exp(g_last) * s_state + _heads(_dot_tn, k * jnp.exp(g_last - g_col), v_new)
    return o, s_new


def _row_block_bcast(x, block, row):
    nh, c, l = x.shape
    x3 = x.reshape(nh * c // block, block, l)
    return jnp.broadcast_to(x3[:, row:row + 1, :], x3.shape).reshape(nh, c, l)


def _boundary_rows(x, hb):
    if 2 * hb >= SUBLANES:
        return _row_block_bcast(x, 2 * hb, hb - 1)
    sub = lax.broadcasted_iota(jnp.int32, (1, x.shape[1], 1), 1) % SUBLANES
    out = _row_block_bcast(x, SUBLANES, hb - 1)
    for start in range(2 * hb, SUBLANES, 2 * hb):
        out = jnp.where(sub >= start, _row_block_bcast(x, SUBLANES, start + hb - 1), out)
    return out


def _half_block_sizes(c):
    assert c & (c - 1) == 0
    return [1 << e for e in range(c.bit_length() - 1)]


def _level_masks(c):
    t = lax.broadcasted_iota(jnp.int32, (c, c), 0)
    s = lax.broadcasted_iota(jnp.int32, (c, c), 1)
    masks = []
    for hb in _half_block_sizes(c):
        same = (t // (2 * hb)) == (s // (2 * hb))
        masks.append(same & ((t % (2 * hb)) >= hb) & ((s % (2 * hb)) < hb))
    return masks


def _hgrn_chunk(q, k, v, bc, level_masks, s_state):
    nh, c, dh = q.shape
    diag = lax.broadcasted_iota(jnp.int32, (c, c), 0) == lax.broadcasted_iota(jnp.int32, (c, c), 1)
    o = _heads(_dot, q * jnp.exp(bc), s_state)
    yield

    a = jnp.where(diag, jnp.sum(q * k, axis=2, keepdims=True), 0.0)
    for hb, mask in zip(_half_block_sizes(c), level_masks):
        e = jnp.exp(-jnp.abs(bc - _boundary_rows(bc, hb)))
        a = a + jnp.where(mask, _heads(_dot_nt, q * e, k * e), 0.0)
        yield
    o = o + _heads(_dot, a, v)
    yield

    b_last = bc[:, c - 1:c, :]
    decay_col = _heads(lambda r: jnp.broadcast_to(r, (SUBLANES, dh)).T[:, 0:1], jnp.exp(b_last))
    s_new = decay_col * s_state + _heads(_dot_tn, k * jnp.exp(b_last - bc), v)
    return o, s_new


def _mixers_kernel(*refs, layer, chunk, has_state):
    if has_state:
        (big_ref, small_ref, bias_ref, alog_ref, mlnw_ref, convw_ref, gdnw_ref, lbl_ref, hgnw_ref,
         c0_ref, n0_ref, m0_ref, sg0_ref, cv0_ref, sh0_ref,
         ys_ref, c_ref, n_ref, m_ref, sg_ref, cv_ref, sh_ref, convbuf) = refs
    else:
        (big_ref, small_ref, bias_ref, alog_ref, mlnw_ref, convw_ref, gdnw_ref, lbl_ref, hgnw_ref,
         ys_ref, c_ref, n_ref, m_ref, sg_ref, cv_ref, sh_ref, convbuf) = refs
    j = pl.program_id(1)
    nj = pl.num_programs(1)
    bb, n_valid = big_ref.shape[0], big_ref.shape[1]
    c = chunk
    w = MIX_WIDTH
    dh = HEAD_DIM
    nh = N_HEADS
    tail = CONV_W - 1

    @pl.when(j == 0)
    def _():
        if has_state:
            c_ref[...] = c0_ref[...]
            n_ref[...] = n0_ref[...]
            m_ref[...] = m0_ref[...]
            sg_ref[...] = sg0_ref[...]
            sh_ref[...] = sh0_ref[...]
            convbuf[:, SUBLANES - tail:SUBLANES, :] = cv0_ref[...]
        else:
            c_ref[...] = jnp.zeros_like(c_ref)
            n_ref[...] = jnp.zeros_like(n_ref)
            m_ref[...] = jnp.zeros_like(m_ref)
            sg_ref[...] = jnp.zeros_like(sg_ref)
            sh_ref[...] = jnp.zeros_like(sh_ref)
            convbuf[:, SUBLANES - tail:SUBLANES, :] = jnp.zeros((bb, tail, N_BRANCH * w), F32)

    def padded(x):
        if n_valid == c:
            return x
        return jnp.concatenate([x, jnp.zeros((bb, c - n_valid, x.shape[2]), x.dtype)], axis=1)

    def seg(i):
        return padded(big_ref[:, :, i * w:(i + 1) * w])

    def per_seq(fn, x):
        return jnp.stack([fn(x[b]) for b in range(bb)])

    def head_major(x):
        return jnp.stack([x[b][:, h * dh:(h + 1) * dh] for b in range(bb) for h in range(nh)])

    def token_major(x):
        return jnp.stack([jnp.concatenate([x[b * nh + h] for h in range(nh)], axis=1) for b in range(bb)])

    def cols(x, first):
        return jnp.stack([x[b][:, first + h:first + h + 1] for b in range(bb) for h in range(nh)])

    def rows(x, first):
        return jnp.stack([x[b][first + h:first + h + 1, :] for b in range(bb) for h in range(nh)])

    lane = lax.broadcasted_iota(jnp.int32, (1, c, SMALL_COLS), 2)
    rowi = lax.broadcasted_iota(jnp.int32, (1, c, SMALL_COLS), 1)
    pre = padded(small_ref[...]) + bias_ref[...]
    is_f = (lane >= SM_F) & (lane < SM_B)
    is_b = (lane >= SM_B) & (lane < SM_A)
    is_a = (lane >= SM_A) & (lane < SM_A + nh)
    logg = -jnp.exp(alog_ref[...]) * _softplus(pre)
    gates = jnp.where(is_f, _log_sigmoid(pre), jnp.where(is_b, _sigmoid(pre), jnp.where(is_a, logg, pre)))
    gates = jnp.where(rowi < n_valid, gates, jnp.where(lane < SM_F, NEG, 0.0))
    tt = lax.broadcasted_iota(jnp.int32, (c, c), 0)
    ss = lax.broadcasted_iota(jnp.int32, (c, c), 1)
    incl = tt >= ss
    strict = tt > ss
    ltri = incl.astype(F32)
    utri = (ss >= tt).astype(F32)
    cum_src = jnp.where(is_f | is_a, gates, 0.0)
    cums = per_seq(lambda x: _dot_exact_lhs(ltri, x), cum_src)
    gates_t = per_seq(lambda x: x.T, gates)
    cums_t = per_seq(lambda x: _dot_exact_rhs(x.T, utri), cum_src)

    lbl = lbl_ref[...]
    sm = jnp.exp(lbl - jnp.max(lbl, axis=0, keepdims=True))
    sm = sm / jnp.sum(sm, axis=0, keepdims=True)
    lb = jnp.zeros((1, w), F32)
    for l in range(1, layer + 1):
        lb = lb + sm[l:l + 1, :]
    valid = lax.broadcasted_iota(jnp.int32, (1, c, w), 1) < n_valid
    hf = seg(SEG_HG_F)
    g_f = lb + (1.0 - lb) * _sigmoid(hf)
    k_hg = jnp.where(valid, (1.0 - lb) * _sigmoid(-hf), 0.0)
    f_log = jnp.where(valid, jnp.log(g_f), 0.0)
    bc = per_seq(lambda x: _dot_exact_lhs(ltri, x), f_log)

    convbuf[:, SUBLANES:SUBLANES + c, :] = padded(big_ref[:, :, SEG_GD_Q * w:(SEG_GD_V + 1) * w])
    conv = convw_ref[CONV_W - 1:CONV_W, :] * convbuf[:, SUBLANES:SUBLANES + c, :]
    for jj in range(CONV_W - 1):
        off = SUBLANES - tail + jj
        conv = conv + convw_ref[jj:jj + 1, :] * convbuf[:, off:off + c, :]
    conv = _silu(conv)
    new_tail = convbuf[:, SUBLANES + n_valid - tail:SUBLANES + n_valid, :]
    convbuf[:, SUBLANES - tail:SUBLANES, :] = new_tail
    cq = head_major(conv[:, :, 0:w])
    ck = head_major(conv[:, :, w:2 * w])
    cv = head_major(conv[:, :, 2 * w:3 * w])
    qn = cq * lax.rsqrt(jnp.sum(cq * cq, axis=2, keepdims=True) + EPS) * (dh ** -0.5)
    kn = ck * lax.rsqrt(jnp.sum(ck * ck, axis=2, keepdims=True) + EPS)
    qk_kk = _heads(_dot_nt, jnp.concatenate([qn, kn], axis=1), kn)

    ml_q = head_major(seg(SEG_ML_Q))
    ml_k = head_major(seg(SEG_ML_K)) * (dh ** -0.5)
    c_old = c_ref[...].reshape(bb * nh, dh, dh)
    ml_qk = _heads(_dot_nt, ml_q, ml_k)
    ml_qc = _heads(_dot, ml_q, c_old)
    m_old = jnp.stack([m_ref[b][:, h:h + 1] for b in range(bb) for h in range(nh)])

    mlstm = _mlstm_chunk(
        ml_q, ml_k, head_major(seg(SEG_ML_V)), ml_qk, ml_qc,
        cols(gates, SM_I), cols(cums, SM_F), rows(gates_t, SM_I), rows(cums_t, SM_F), incl,
        c_old, n_ref[...].reshape(bb * nh, 1, dh), m_old)
    level_masks = _level_masks(c)
    gdn = _gdn_chunk(qn, kn, cv, qk_kk, cols(gates, SM_B), cols(cums, SM_A), rows(cums_t, SM_A),
                     incl, strict, level_masks, sg_ref[...].reshape(bb * nh, dh, dh))
    hgrn = _hgrn_chunk(head_major(_silu(seg(SEG_HG_Q))), head_major(k_hg), head_major(seg(SEG_HG_I)),
                       head_major(bc), level_masks, sh_ref[...].reshape(bb * nh, dh, dh))
    (hh, c_new, n_new, m_new), (o_gd, sg_new), (o_hg, sh_new) = _interleave(
        (mlstm, gdn, hgrn), STAGES_PER_ROUND)

    c_ref[...] = c_new.reshape(c_ref.shape)
    n_ref[...] = n_new.reshape(n_ref.shape)
    m_lane = lax.broadcasted_iota(jnp.int32, (1, SMALL_COLS), 1)
    m_rows = []
    for b in range(bb):
        m_row = jnp.zeros((1, SMALL_COLS), F32)
        for h in range(nh):
            m_row = jnp.where(m_lane == h, m_new[b * nh + h], m_row)
        m_rows.append(m_row)
    m_ref[...] = jnp.stack(m_rows)
    y_ml = token_major(_rms(hh)) * mlnw_ref[...] * _sigmoid(seg(SEG_ML_O))
    ys_ref[:, :, 0:w] = y_ml[:, :n_valid].astype(ys_ref.dtype)

    sg_ref[...] = sg_new.reshape(sg_ref.shape)
    y_gd = token_major(_rms(o_gd) * gdnw_ref[...]) * _silu(seg(SEG_GD_Z))
    ys_ref[:, :, w:2 * w] = y_gd[:, :n_valid].astype(ys_ref.dtype)

    sh_ref[...] = sh_new.reshape(sh_ref.shape)
    y_hg = _rms(token_major(o_hg)) * hgnw_ref[...] * _silu(seg(SEG_HG_G))
    ys_ref[:, :, 2 * w:3 * w] = y_hg[:, :n_valid].astype(ys_ref.dtype)

    @pl.when(j == nj - 1)
    def _():
        cv_ref[...] = convbuf[:, SUBLANES - tail:SUBLANES, :]


def _mixers(big3, small3, params, states_in, states_prev, *, layer, depth, chunk, bb):
    b, t, n_big = big3.shape
    rows = min(t, chunk)
    assert t % rows == 0 and chunk % BF16_ROWS == 0 and CONV_W - 1 <= rows and b % bb == 0
    w, dh, nh = MIX_WIDTH, HEAD_DIM, N_HEADS
    has_state = states_in is not None
    bias_row, alog_row, ml_norm_w, conv_w, gd_norm_w, lb_logits, hg_norm_w = params

    def const(shape):
        return pl.BlockSpec(shape, lambda i, j: (0,) * len(shape))

    def st(shape):
        return pl.BlockSpec((None, bb) + shape, lambda i, j: (layer, i) + (0,) * len(shape))

    state_shapes = [(nh, dh, dh), (nh, 1, dh), (1, SMALL_COLS), (nh, dh, dh), (CONV_W - 1, N_BRANCH * w), (nh, dh, dh)]
    in_specs = [
        pl.BlockSpec((bb, rows, N_MIX_SEGS * w), lambda i, j: (i, j, 0)),
        pl.BlockSpec((bb, rows, SMALL_COLS), lambda i, j: (i, j, 0)),
        const((1, SMALL_COLS)), const((1, SMALL_COLS)), const((1, w)), const((CONV_W, N_BRANCH * w)),
        const((1, dh)), const((depth, w)), const((1, w)),
    ]
    args = [big3, small3, bias_row, alog_row, ml_norm_w, conv_w, gd_norm_w, lb_logits, hg_norm_w]
    if has_state:
        in_specs += [st(s) for s in state_shapes]
        args += list(states_in)
    aliases = {}
    if states_prev is not None:
        for k_out, arr in enumerate(states_prev):
            aliases[len(args)] = 1 + k_out
            in_specs.append(pl.BlockSpec(memory_space=pl.ANY))
            args.append(arr)
    out_specs = [pl.BlockSpec((bb, rows, N_BRANCH * w), lambda i, j: (i, j, 0))] + [st(s) for s in state_shapes]
    out_shape = [jax.ShapeDtypeStruct((b, t, N_BRANCH * w), BF16)] + [
        jax.ShapeDtypeStruct((depth, b) + s, F32) for s in state_shapes]

    def body(*refs):
        n_in = len(args) - (len(states_prev) if states_prev is not None else 0)
        kept = refs[:n_in] + refs[len(args):]
        _mixers_kernel(*kept, layer=layer, chunk=chunk, has_state=has_state)

    outs = pl.pallas_call(
        body,
        grid=(b // bb, t // rows),
        in_specs=in_specs,
        out_specs=out_specs,
        out_shape=out_shape,
        scratch_shapes=[pltpu.VMEM((bb, SUBLANES + chunk, N_BRANCH * w), F32)],
        input_output_aliases=aliases,
        compiler_params=pltpu.CompilerParams(
            dimension_semantics=("parallel", "arbitrary"), vmem_limit_bytes=VMEM_LIMIT_BYTES),
        name="mixers",
    )(*args)
    return outs[0], tuple(outs[1:])


def _merge_ffn_kernel(x_ref, ys_ref, gate_ref, wbr_ref, wout_ref, n2_ref, wup_ref, wdown_ref, fin_ref,
                      out_ref, *, last):
    w = MIX_WIDTH
    d = x_ref.shape[1]
    merged = jnp.zeros(x_ref.shape, F32)
    for n in range(N_BRANCH):
        br = jnp.dot(ys_ref[:, n * w:(n + 1) * w], wbr_ref[n], preferred_element_type=F32)
        merged = merged + _sigmoid(gate_ref[:, n * d:(n + 1) * d]) * br
    x1 = x_ref[...] + jnp.dot(merged.astype(BF16), wout_ref[...], preferred_element_type=F32)
    xn = (_rms(x1) * n2_ref[...]).astype(BF16)
    hid = jnp.square(jnp.maximum(jnp.dot(xn, wup_ref[...], preferred_element_type=F32), 0.0))
    x2 = x1 + jnp.dot(hid.astype(BF16), wdown_ref[...], preferred_element_type=F32)
    if last:
        x2 = _rms(x2) * fin_ref[...]
    out_ref[...] = x2


def _merge_ffn(x2d, ys2d, big2d, w_branch, w_out, norm2_w, w_up, w_down, final_w, *, layer, tm, last):
    m, d = x2d.shape
    w = MIX_WIDTH
    d_ff = w_up.shape[2]
    gate_block = N_BRANCH * d
    assert m % tm == 0 and (N_MIX_SEGS * w) % gate_block == 0
    gate_idx = (N_MIX_SEGS * w) // gate_block

    def per_layer(shape):
        return pl.BlockSpec((None,) + shape, lambda i: (layer,) + (0,) * len(shape), pipeline_mode=pl.Buffered(1))

    return pl.pallas_call(
        functools.partial(_merge_ffn_kernel, last=last),
        grid=(m // tm,),
        in_specs=[
            pl.BlockSpec((tm, d), lambda i: (i, 0)),
            pl.BlockSpec((tm, N_BRANCH * w), lambda i: (i, 0)),
            pl.BlockSpec((tm, gate_block), lambda i: (i, gate_idx)),
            per_layer((N_BRANCH, w, d)), per_layer((d, d)), per_layer((1, d)), per_layer((d, d_ff)),
            per_layer((d_ff, d)),
            pl.BlockSpec((1, d), lambda i: (0, 0), pipeline_mode=pl.Buffered(1)),
        ],
        out_specs=pl.BlockSpec((tm, d), lambda i: (i, 0)),
        out_shape=jax.ShapeDtypeStruct((m, d), F32),
        compiler_params=pltpu.CompilerParams(
            dimension_semantics=("parallel",), vmem_limit_bytes=VMEM_LIMIT_BYTES),
        name="merge_ffn",
    )(x2d, ys2d, big2d, w_branch, w_out, norm2_w, w_up, w_down, final_w)


def _split_w_in(w_in):
    w, nh = MIX_WIDTH, N_HEADS
    a0, a1 = 4 * w, 4 * w + 2 * nh
    b0, b1 = a1 + 4 * w, a1 + 4 * w + 2 * nh
    w_big = jnp.concatenate([w_in[:, :, :a0], w_in[:, :, a1:b0], w_in[:, :, b1:]], axis=2).astype(BF16)
    small = jnp.concatenate([w_in[:, :, a0:a1], w_in[:, :, b0:b1]], axis=2)
    w_small = jnp.pad(small, ((0, 0), (0, 0), (0, SMALL_COLS - 4 * nh))).astype(BF16)
    return w_big, w_small


def _pad_lanes(parts, total):
    row = jnp.concatenate([p.reshape(1, -1).astype(F32) for p in parts], axis=1)
    return jnp.pad(row, ((0, 0), (0, total - row.shape[1])))


def _tiles(m):
    return math.gcd(m, 256), math.gcd(m, 256)


def _run_group(x3, states_in, weights, mix_params, seqs_per_step):
    b, t, d = x3.shape
    depth = len(mix_params)
    chunk = CHUNK if t % CHUNK == 0 else BF16_ROWS
    assert t % CHUNK == 0 or t <= BF16_ROWS
    x2d = x3.reshape(b * t, d)
    tm, tm_merge = _tiles(b * t)
    states_prev = None
    for l in range(depth):
        big, small = _in_proj(x2d, weights["norm1_w"], weights["w_big"], weights["w_small"], layer=l, tm=tm)
        ys, states_prev = _mixers(
            big.reshape(b, t, -1), small.reshape(b, t, -1), mix_params[l], states_in, states_prev,
            layer=l, depth=depth, chunk=chunk, bb=math.gcd(b, seqs_per_step))
        x2d = _merge_ffn(x2d, ys.reshape(b * t, -1), big, weights["w_branch"], weights["w_out"],
                         weights["norm2_w"], weights["w_up"], weights["w_down"], weights["final_norm_w"],
                         layer=l, tm=tm_merge, last=(l == depth - 1))
    return x2d.reshape(b, t, d), states_prev


def kernel(x_prompt, x_sample, state_mlstm_C, state_mlstm_n, state_mlstm_m, state_gdn_S, state_gdn_conv, state_hgrn_S, norm1_w, w_in, ml_i_bias, ml_f_bias, ml_norm_w, gd_conv_w, gd_A_log, gd_dt_bias, gd_norm_w, hg_lb_logits, hg_norm_w, w_branch, w_out, norm2_w, w_up, w_down, final_norm_w):
    depth, d = norm1_w.shape
    nh = N_HEADS
    w_big, w_small = _split_w_in(w_in)
    weights = dict(
        norm1_w=norm1_w.reshape(depth, 1, d), w_big=w_big, w_small=w_small,
        w_branch=w_branch.astype(BF16), w_out=w_out.astype(BF16), norm2_w=norm2_w.reshape(depth, 1, d),
        w_up=w_up.astype(BF16), w_down=w_down.astype(BF16), final_norm_w=final_norm_w.reshape(1, d))
    mix_params = []
    for l in range(depth):
        zeros_h = jnp.zeros((nh,), F32)
        bias_row = _pad_lanes([ml_i_bias[l], ml_f_bias[l], zeros_h, gd_dt_bias[l]], SMALL_COLS)
        alog_row = _pad_lanes([zeros_h, zeros_h, zeros_h, gd_A_log[l]], SMALL_COLS)
        mix_params.append((bias_row, alog_row, ml_norm_w[l].reshape(1, -1), gd_conv_w[l],
                           gd_norm_w[l].reshape(1, -1), hg_lb_logits, hg_norm_w[l].reshape(1, -1)))

    def unpack(states):
        s_c, s_n, s_m, s_g, s_cv, s_h = states
        return s_c, s_n.reshape(s_n.shape[:3] + s_n.shape[4:]), s_m[:, :, 0, :nh], s_g, s_cv, s_h

    y_prompt, p_states = _run_group(x_prompt, None, weights, mix_params, PROMPT_SEQS_PER_STEP)

    bs = x_sample.shape[0]
    m_in = jnp.pad(state_mlstm_m, ((0, 0), (0, 0), (0, SMALL_COLS - nh))).reshape(depth, bs, 1, SMALL_COLS)
    n_in = state_mlstm_n.reshape(depth, bs, nh, 1, HEAD_DIM)
    s_in = (state_mlstm_C, n_in, m_in, state_gdn_S, state_gdn_conv, state_hgrn_S)
    y_sample, s_states = _run_group(x_sample, s_in, weights, mix_params, SAMPLE_SEQS_PER_STEP)

    return (y_prompt, y_sample) + unpack(p_states) + unpack(s_states)
```

```python
import functools
import math

import jax
import jax.numpy as jnp
from jax import lax
from jax.experimental import pallas as pl
from jax.experimental.pallas import tpu as pltpu

F32 = jnp.float32
BF16 = jnp.bfloat16

HEAD_DIM = 128
N_HEADS = 4
MIX_WIDTH = N_HEADS * HEAD_DIM
N_BRANCH = 3
CONV_W = 4
EPS = 1e-6
CHUNK = 128
SUBLANES = 8
BF16_ROWS = 16
SMALL_COLS = 128
NEG = -1e30
VMEM_LIMIT_BYTES = 56 * 1024 * 1024
PROJ_TN = 1024
PROMPT_SEQS_PER_STEP = 2
SAMPLE_SEQS_PER_STEP = 8
STAGES_PER_ROUND = (1, 4, 2)

SEG_ML_Q, SEG_ML_K, SEG_ML_V, SEG_ML_O = 0, 1, 2, 3
SEG_GD_Q, SEG_GD_K, SEG_GD_V, SEG_GD_Z = 4, 5, 6, 7
SEG_HG_Q, SEG_HG_F, SEG_HG_I, SEG_HG_G = 8, 9, 10, 11
N_MIX_SEGS = 12
SM_I, SM_F, SM_B, SM_A = 0, N_HEADS, 2 * N_HEADS, 3 * N_HEADS


def _dot(a, b):
    return jnp.dot(a.astype(BF16), b.astype(BF16), preferred_element_type=F32)


def _dot_nt(a, b):
    return lax.dot_general(a.astype(BF16), b.astype(BF16), (((1,), (1,)), ((), ())),
                           preferred_element_type=F32)


def _dot_tn(a, b):
    return _dot(a.T, b)


def _f32_dot(a, b):
    return jnp.dot(a, b, preferred_element_type=F32)


def _split2(a):
    hi = a.astype(BF16)
    lo = (a - hi.astype(F32)).astype(BF16)
    return hi, lo


def _split3(a):
    hi = a.astype(BF16)
    r = a - hi.astype(F32)
    mid = r.astype(BF16)
    lo = (r - mid.astype(F32)).astype(BF16)
    return hi, mid, lo


def _dot_exact_lhs(a01, b):
    a = a01.astype(BF16)
    return sum(_f32_dot(a, part) for part in _split3(b))


def _dot_exact_rhs(a, b01):
    b = b01.astype(BF16)
    return sum(_f32_dot(part, b) for part in _split3(a))


def _sigmoid(x):
    return 1.0 / (1.0 + jnp.exp(-x))


def _silu(x):
    return x * _sigmoid(x)


def _softplus(x):
    return jnp.maximum(x, 0.0) + jnp.log(1.0 + jnp.exp(-jnp.abs(x)))


def _log_sigmoid(x):
    return -_softplus(-x)


def _rms(x, axis=-1):
    return x * lax.rsqrt(jnp.mean(x * x, axis=axis, keepdims=True) + EPS)


def _in_proj_kernel(x_ref, nw_ref, *refs):
    *wbig_refs, wsmall_ref, big_ref, small_ref = refs
    xn = (_rms(x_ref[...]) * nw_ref[...]).astype(BF16)
    small_ref[...] = jnp.dot(xn, wsmall_ref[...], preferred_element_type=F32)
    out0 = 0
    for wbig_ref in wbig_refs:
        for n0 in range(0, wbig_ref.shape[1], PROJ_TN):
            big_ref[:, out0 + n0:out0 + n0 + PROJ_TN] = jnp.dot(
                xn, wbig_ref[:, n0:n0 + PROJ_TN], preferred_element_type=F32)
        out0 += wbig_ref.shape[1]


def _in_proj(x2d, norm_w, w_big, w_small, *, layer, tm):
    m, d = x2d.shape
    n_big = sum(wg.shape[2] for wg in w_big)
    assert m % tm == 0 and all(wg.shape[2] % PROJ_TN == 0 for wg in w_big)
    resident = dict(pipeline_mode=pl.Buffered(1))
    return pl.pallas_call(
        _in_proj_kernel,
        grid=(m // tm,),
        in_specs=[
            pl.BlockSpec((tm, d), lambda i: (i, 0)),
            pl.BlockSpec((None, 1, d), lambda i: (layer, 0, 0), **resident),
            *[pl.BlockSpec((None, d, wg.shape[2]), lambda i: (layer, 0, 0), **resident) for wg in w_big],
            pl.BlockSpec((None, d, SMALL_COLS), lambda i: (layer, 0, 0), **resident),
        ],
        out_specs=[
            pl.BlockSpec((tm, n_big), lambda i: (i, 0)),
            pl.BlockSpec((tm, SMALL_COLS), lambda i: (i, 0)),
        ],
        out_shape=[
            jax.ShapeDtypeStruct((m, n_big), F32),
            jax.ShapeDtypeStruct((m, SMALL_COLS), F32),
        ],
        compiler_params=pltpu.CompilerParams(
            dimension_semantics=("parallel",), vmem_limit_bytes=VMEM_LIMIT_BYTES),
        name="in_proj",
    )(x2d, norm_w, *w_big, w_small)


def _heads(fn, *xs):
    return jnp.stack([fn(*(x[h] for x in xs)) for h in range(xs[0].shape[0])])


def _heads_dot_hi(a, b):
    ah, al = _split2(a)
    bh, bl = _split2(b)
    return _heads(_f32_dot, ah, bh) + _heads(_f32_dot, ah, bl) + _heads(_f32_dot, al, bh)


def _interleave(stage_generators, stages_per_round):
    results = [None] * len(stage_generators)
    live = list(enumerate(stage_generators))
    while live:
        still = []
        for idx, gen in live:
            try:
                for _ in range(stages_per_round[idx]):
                    next(gen)
                still.append((idx, gen))
            except StopIteration as stop:
                results[idx] = stop.value
        live = still
    return results


def _inv_unit_lower(n, level_masks):
    c = n.shape[-1]
    eye = (lax.broadcasted_iota(jnp.int32, (c, c), 0) == lax.broadcasted_iota(jnp.int32, (c, c), 1)).astype(F32)
    n_hi, n_lo = _split2(n)
    zero = jnp.zeros((), BF16)
    d = eye - jnp.where(level_masks[0], n, 0.0)
    for mask in level_masks[1:]:
        c_hi = jnp.where(mask, n_hi, zero)
        c_lo = jnp.where(mask, n_lo, zero)
        d_hi, d_lo = _split2(d)
        t = _heads(_f32_dot, d_hi, c_hi) + _heads(_f32_dot, d_hi, c_lo) + _heads(_f32_dot, d_lo, c_hi)
        yield
        t_hi, t_lo = _split2(t)
        d = d - (_heads(_f32_dot, t_hi, d_hi) + _heads(_f32_dot, t_hi, d_lo) + _heads(_f32_dot, t_lo, d_hi))
        yield
    return d


def _mlstm_chunk(q, k, v, qk, qc, i_col, b_col, i_row, b_row, causal, c_state, n_state, m_state):
    c = q.shape[1]
    a_col = b_col + m_state
    d = jnp.where(causal, b_col - b_row + i_row, NEG)
    mt = jnp.maximum(a_col, jnp.max(d, axis=2, keepdims=True))
    p = qk * jnp.exp(d - mt)
    w_st = jnp.exp(a_col - mt)
    yield
    num = _heads(_dot, p, v) + w_st * qc
    yield
    den = jnp.sum(p, axis=2, keepdims=True) + w_st * jnp.sum(q * n_state, axis=2, keepdims=True)
    h = num / jnp.maximum(jnp.abs(den), jnp.exp(-mt))
    m_new = mt[:, c - 1:c, :]
    b_last = b_col[:, c - 1:c, :]
    ws_col = jnp.exp(b_last - b_col + i_col - m_new)
    wc = jnp.exp(b_last + m_state - m_new)
    kw = k * ws_col
    yield
    c_new = wc * c_state + _heads(_dot_tn, kw, v)
    n_new = wc * n_state + jnp.sum(kw, axis=1, keepdims=True)
    return h, c_new, n_new, m_new


def _gdn_chunk(q, k, v, qk_kk, beta_col, g_col, g_row, incl, strict, level_masks, s_state):
    c = q.shape[1]
    dh = v.shape[2]
    decay = jnp.exp(jnp.where(incl, g_col - g_row, NEG))
    qk = qk_kk[:, :c]
    kk = qk_kk[:, c:]
    n = jnp.where(strict, beta_col * kk * decay, 0.0)
    yield
    a_inv = yield from _inv_unit_lower(n, level_masks)
    eg_col = jnp.exp(g_col)
    rhs = jnp.concatenate([beta_col * v, (beta_col * eg_col) * k], axis=2)
    yield
    sol = _heads_dot_hi(a_inv, rhs)
    u = sol[:, :, :dh]
    w = sol[:, :, dh:]
    yield
    wq_s = _heads(_dot, jnp.concatenate([w, q * eg_col], axis=1), s_state)
    v_new = u - wq_s[:, :c]
    yield
    o = wq_s[:, c:] + _heads(_dot, qk * decay, v_new)
    g_last = g_col[:, c - 1:c, :]
    yield
    s_new = jnp.exp(g_last) * s_state + _heads(_dot_tn, k * jnp.exp(g_last - g_col), v_new)
    return o, s_new


def _row_block_bcast(x, block, row):
    nh, c, l = x.shape
    x3 = x.reshape(nh * c // block, block, l)
    return jnp.broadcast_to(x3[:, row:row + 1, :], x3.shape).reshape(nh, c, l)


def _boundary_rows(x, hb):
    if 2 * hb >= SUBLANES:
        return _row_block_bcast(x, 2 * hb, hb - 1)
    sub = lax.broadcasted_iota(jnp.int32, (1, x.shape[1], 1), 1) % SUBLANES
    out = _row_block_bcast(x, SUBLANES, hb - 1)
    for start in range(2 * hb, SUBLANES, 2 * hb):
        out = jnp.where(sub >= start, _row_block_bcast(x, SUBLANES, start + hb - 1), out)
    return out


def _half_block_sizes(c):
    assert c & (c - 1) == 0
    return [1 << e for e in range(c.bit_length() - 1)]


def _level_masks(c):
    t = lax.broadcasted_iota(jnp.int32, (c, c), 0)
    s = lax.broadcasted_iota(jnp.int32, (c, c), 1)
    masks = []
    for hb in _half_block_sizes(c):
        same = (t // (2 * hb)) == (s // (2 * hb))
        masks.append(same & ((t % (2 * hb)) >= hb) & ((s % (2 * hb)) < hb))
    return masks


def _hgrn_chunk(q, k, v, bc, level_masks, s_state):
    nh, c, dh = q.shape
    diag = lax.broadcasted_iota(jnp.int32, (c, c), 0) == lax.broadcasted_iota(jnp.int32, (c, c), 1)
    o = _heads(_dot, q * jnp.exp(bc), s_state)
    yield

    a = jnp.where(diag, jnp.sum(q * k, axis=2, keepdims=True), 0.0)
    for hb, mask in zip(_half_block_sizes(c), level_masks):
        e = jnp.exp(-jnp.abs(bc - _boundary_rows(bc, hb)))
        a = a + jnp.where(mask, _heads(_dot_nt, q * e, k * e), 0.0)
        yield
    o = o + _heads(_dot, a, v)
    yield

    b_last = bc[:, c - 1:c, :]
    decay_col = _heads(lambda r: jnp.broadcast_to(r, (SUBLANES, dh)).T[:, 0:1], jnp.exp(b_last))
    s_new = decay_col * s_state + _heads(_dot_tn, k * jnp.exp(b_last - bc), v)
    return o, s_new


def _mixers_kernel(*refs, layer, chunk, n_valid, bb, has_state):
    if has_state:
        (big_ref, small_ref, bias_ref, alog_ref, mlnw_ref, convw_ref, gdnw_ref, lbl_ref, hgnw_ref,
         c0_ref, n0_ref, m0_ref, sg0_ref, cv0_ref, sh0_ref,
         ys_ref, c_ref, n_ref, m_ref, sg_ref, cv_ref, sh_ref, convbuf) = refs
    else:
        (big_ref, small_ref, bias_ref, alog_ref, mlnw_ref, convw_ref, gdnw_ref, lbl_ref, hgnw_ref,
         ys_ref, c_ref, n_ref, m_ref, sg_ref, cv_ref, sh_ref, convbuf) = refs
    j = pl.program_id(1)
    nj = pl.num_programs(1)
    c = chunk
    flat = len(big_ref.shape) == 2
    w = MIX_WIDTH
    dh = HEAD_DIM
    nh = N_HEADS
    tail = CONV_W - 1

    @pl.when(j == 0)
    def _():
        if has_state:
            c_ref[...] = c0_ref[...]
            n_ref[...] = n0_ref[...]
            m_ref[...] = m0_ref[...]
            sg_ref[...] = sg0_ref[...]
            sh_ref[...] = sh0_ref[...]
            convbuf[:, SUBLANES - tail:SUBLANES, :] = cv0_ref[...]
        else:
            c_ref[...] = jnp.zeros_like(c_ref)
            n_ref[...] = jnp.zeros_like(n_ref)
            m_ref[...] = jnp.zeros_like(m_ref)
            sg_ref[...] = jnp.zeros_like(sg_ref)
            sh_ref[...] = jnp.zeros_like(sh_ref)
            convbuf[:, SUBLANES - tail:SUBLANES, :] = jnp.zeros((bb, tail, N_BRANCH * w), F32)

    def load(ref, lo, hi):
        if not flat:
            return ref[:, :, lo:hi]
        x = ref[:, lo:hi]
        zeros = jnp.zeros((c - n_valid, hi - lo), x.dtype)
        return jnp.stack([jnp.concatenate([x[b * n_valid:(b + 1) * n_valid], zeros], axis=0) for b in range(bb)])

    def store(col0, y):
        if not flat:
            ys_ref[:, :, col0:col0 + y.shape[2]] = y.astype(ys_ref.dtype)
        else:
            rows_out = jnp.concatenate([y[b, :n_valid] for b in range(bb)], axis=0)
            ys_ref[:, col0:col0 + y.shape[2]] = rows_out.astype(ys_ref.dtype)

    def seg(i):
        return load(big_ref, i * w, (i + 1) * w)

    def per_seq(fn, x):
        return jnp.stack([fn(x[b]) for b in range(bb)])

    def head_major(x):
        return jnp.stack([x[b][:, h * dh:(h + 1) * dh] for b in range(bb) for h in range(nh)])

    def token_major(x):
        return jnp.stack([jnp.concatenate([x[b * nh + h] for h in range(nh)], axis=1) for b in range(bb)])

    def cols(x, first):
        return jnp.stack([x[b][:, first + h:first + h + 1] for b in range(bb) for h in range(nh)])

    def rows(x, first):
        return jnp.stack([x[b][first + h:first + h + 1, :] for b in range(bb) for h in range(nh)])

    lane = lax.broadcasted_iota(jnp.int32, (1, c, SMALL_COLS), 2)
    rowi = lax.broadcasted_iota(jnp.int32, (1, c, SMALL_COLS), 1)
    pre = load(small_ref, 0, SMALL_COLS) + bias_ref[...]
    is_f = (lane >= SM_F) & (lane < SM_B)
    is_b = (lane >= SM_B) & (lane < SM_A)
    is_a = (lane >= SM_A) & (lane < SM_A + nh)
    logg = -jnp.exp(alog_ref[...]) * _softplus(pre)
    gates = jnp.where(is_f, _log_sigmoid(pre), jnp.where(is_b, _sigmoid(pre), jnp.where(is_a, logg, pre)))
    gates = jnp.where(rowi < n_valid, gates, jnp.where(lane < SM_F, NEG, 0.0))
    tt = lax.broadcasted_iota(jnp.int32, (c, c), 0)
    ss = lax.broadcasted_iota(jnp.int32, (c, c), 1)
    incl = tt >= ss
    strict = tt > ss
    ltri = incl.astype(F32)
    utri = (ss >= tt).astype(F32)
    cum_src = jnp.where(is_f | is_a, gates, 0.0)
    cums = per_seq(lambda x: _dot_exact_lhs(ltri, x), cum_src)
    gates_t = per_seq(lambda x: x.T, gates)
    cums_t = per_seq(lambda x: _dot_exact_rhs(x.T, utri), cum_src)

    lbl = lbl_ref[...]
    sm = jnp.exp(lbl - jnp.max(lbl, axis=0, keepdims=True))
    sm = sm / jnp.sum(sm, axis=0, keepdims=True)
    lb = jnp.zeros((1, w), F32)
    for l in range(1, layer + 1):
        lb = lb + sm[l:l + 1, :]
    valid = lax.broadcasted_iota(jnp.int32, (1, c, w), 1) < n_valid
    hf = seg(SEG_HG_F)
    g_f = lb + (1.0 - lb) * _sigmoid(hf)
    k_hg = jnp.where(valid, (1.0 - lb) * _sigmoid(-hf), 0.0)
    f_log = jnp.where(valid, jnp.log(g_f), 0.0)
    bc = per_seq(lambda x: _dot_exact_lhs(ltri, x), f_log)

    convbuf[:, SUBLANES:SUBLANES + c, :] = load(big_ref, SEG_GD_Q * w, (SEG_GD_V + 1) * w)
    conv = convw_ref[CONV_W - 1:CONV_W, :] * convbuf[:, SUBLANES:SUBLANES + c, :]
    for jj in range(CONV_W - 1):
        off = SUBLANES - tail + jj
        conv = conv + convw_ref[jj:jj + 1, :] * convbuf[:, off:off + c, :]
    conv = _silu(conv)
    new_tail = convbuf[:, SUBLANES + n_valid - tail:SUBLANES + n_valid, :]
    convbuf[:, SUBLANES - tail:SUBLANES, :] = new_tail
    cq = head_major(conv[:, :, 0:w])
    ck = head_major(conv[:, :, w:2 * w])
    cv = head_major(conv[:, :, 2 * w:3 * w])
    qn = cq * lax.rsqrt(jnp.sum(cq * cq, axis=2, keepdims=True) + EPS) * (dh ** -0.5)
    kn = ck * lax.rsqrt(jnp.sum(ck * ck, axis=2, keepdims=True) + EPS)
    qk_kk = _heads(_dot_nt, jnp.concatenate([qn, kn], axis=1), kn)

    ml_q = head_major(seg(SEG_ML_Q))
    ml_k = head_major(seg(SEG_ML_K)) * (dh ** -0.5)
    c_old = c_ref[...].reshape(bb * nh, dh, dh)
    ml_qk = _heads(_dot_nt, ml_q, ml_k)
    ml_qc = _heads(_dot, ml_q, c_old)
    m_old = jnp.stack([m_ref[b][:, h:h + 1] for b in range(bb) for h in range(nh)])

    mlstm = _mlstm_chunk(
        ml_q, ml_k, head_major(seg(SEG_ML_V)), ml_qk, ml_qc,
        cols(gates, SM_I), cols(cums, SM_F), rows(gates_t, SM_I), rows(cums_t, SM_F), incl,
        c_old, n_ref[...].reshape(bb * nh, 1, dh), m_old)
    level_masks = _level_masks(c)
    gdn = _gdn_chunk(qn, kn, cv, qk_kk, cols(gates, SM_B), cols(cums, SM_A), rows(cums_t, SM_A),
                     incl, strict, level_masks, sg_ref[...].reshape(bb * nh, dh, dh))
    hgrn = _hgrn_chunk(head_major(_silu(seg(SEG_HG_Q))), head_major(k_hg), head_major(seg(SEG_HG_I)),
                       head_major(bc), level_masks, sh_ref[...].reshape(bb * nh, dh, dh))
    (hh, c_new, n_new, m_new), (o_gd, sg_new), (o_hg, sh_new) = _interleave(
        (mlstm, gdn, hgrn), STAGES_PER_ROUND)

    c_ref[...] = c_new.reshape(c_ref.shape)
    n_ref[...] = n_new.reshape(n_ref.shape)
    m_lane = lax.broadcasted_iota(jnp.int32, (1, SMALL_COLS), 1)
    m_rows = []
    for b in range(bb):
        m_row = jnp.zeros((1, SMALL_COLS), F32)
        for h in range(nh):
            m_row = jnp.where(m_lane == h, m_new[b * nh + h], m_row)
        m_rows.append(m_row)
    m_ref[...] = jnp.stack(m_rows)
    y_ml = token_major(_rms(hh)) * mlnw_ref[...] * _sigmoid(seg(SEG_ML_O))
    store(0, y_ml)

    sg_ref[...] = sg_new.reshape(sg_ref.shape)
    y_gd = token_major(_rms(o_gd) * gdnw_ref[...]) * _silu(seg(SEG_GD_Z))
    store(w, y_gd)

    sh_ref[...] = sh_new.reshape(sh_ref.shape)
    y_hg = _rms(token_major(o_hg)) * hgnw_ref[...] * _silu(seg(SEG_HG_G))
    store(2 * w, y_hg)

    @pl.when(j == nj - 1)
    def _():
        cv_ref[...] = convbuf[:, SUBLANES - tail:SUBLANES, :]


def _mixers(big2, small2, params, states_in, states_prev, *, b, t, layer, depth, chunk, bb):
    rows = min(t, chunk)
    assert t % rows == 0 and chunk % BF16_ROWS == 0 and CONV_W - 1 <= rows and b % bb == 0
    assert big2.shape[0] == b * t
    w, dh, nh = MIX_WIDTH, HEAD_DIM, N_HEADS
    has_state = states_in is not None
    bias_row, alog_row, ml_norm_w, conv_w, gd_norm_w, lb_logits, hg_norm_w = params

    def const(shape):
        return pl.BlockSpec(shape, lambda i, j: (0,) * len(shape))

    def st(shape):
        return pl.BlockSpec((None, bb) + shape, lambda i, j: (layer, i) + (0,) * len(shape))

    flat = rows < chunk

    def view_shape(l):
        return (b * t, l) if flat else (b, t, l)

    def view(x):
        return x.reshape(view_shape(x.shape[-1]))

    def rows_spec(l):
        if flat:
            return pl.BlockSpec((bb * rows, l), lambda i, j: (i, 0))
        return pl.BlockSpec((bb, rows, l), lambda i, j: (i, j, 0))

    state_shapes = [(nh, dh, dh), (nh, 1, dh), (1, SMALL_COLS), (nh, dh, dh), (CONV_W - 1, N_BRANCH * w), (nh, dh, dh)]
    in_specs = [
        rows_spec(N_MIX_SEGS * w), rows_spec(SMALL_COLS),
        const((1, SMALL_COLS)), const((1, SMALL_COLS)), const((1, w)), const((CONV_W, N_BRANCH * w)),
        const((1, dh)), const((depth, w)), const((1, w)),
    ]
    args = [view(big2), view(small2), bias_row, alog_row, ml_norm_w, conv_w, gd_norm_w, lb_logits, hg_norm_w]
    if has_state:
        in_specs += [st(s) for s in state_shapes]
        args += list(states_in)
    aliases = {}
    if states_prev is not None:
        for k_out, arr in enumerate(states_prev):
            aliases[len(args)] = 1 + k_out
            in_specs.append(pl.BlockSpec(memory_space=pl.ANY))
            args.append(arr)
    out_specs = [rows_spec(N_BRANCH * w)] + [st(s) for s in state_shapes]
    out_shape = [jax.ShapeDtypeStruct(view_shape(N_BRANCH * w), BF16)] + [
        jax.ShapeDtypeStruct((depth, b) + s, F32) for s in state_shapes]

    def body(*refs):
        n_in = len(args) - (len(states_prev) if states_prev is not None else 0)
        kept = refs[:n_in] + refs[len(args):]
        _mixers_kernel(*kept, layer=layer, chunk=chunk, n_valid=rows, bb=bb, has_state=has_state)

    outs = pl.pallas_call(
        body,
        grid=(b // bb, t // rows),
        in_specs=in_specs,
        out_specs=out_specs,
        out_shape=out_shape,
        scratch_shapes=[pltpu.VMEM((bb, SUBLANES + chunk, N_BRANCH * w), F32)],
        input_output_aliases=aliases,
        compiler_params=pltpu.CompilerParams(
            dimension_semantics=("parallel", "arbitrary"), vmem_limit_bytes=VMEM_LIMIT_BYTES),
        name="mixers",
    )(*args)
    return outs[0].reshape(b * t, N_BRANCH * w), tuple(outs[1:])


def _merge_ffn_kernel(x_ref, ys_ref, gate_ref, wbr_ref, wout_ref, n2_ref, wup_ref, wdown_ref, fin_ref,
                      out_ref, *, last):
    w = MIX_WIDTH
    d = x_ref.shape[1]
    merged = jnp.zeros(x_ref.shape, F32)
    for n in range(N_BRANCH):
        br = jnp.dot(ys_ref[:, n * w:(n + 1) * w], wbr_ref[n], preferred_element_type=F32)
        merged = merged + _sigmoid(gate_ref[:, n * d:(n + 1) * d]) * br
    x1 = x_ref[...] + jnp.dot(merged.astype(BF16), wout_ref[...], preferred_element_type=F32)
    xn = (_rms(x1) * n2_ref[...]).astype(BF16)
    hid = jnp.square(jnp.maximum(jnp.dot(xn, wup_ref[...], preferred_element_type=F32), 0.0))
    x2 = x1 + jnp.dot(hid.astype(BF16), wdown_ref[...], preferred_element_type=F32)
    if last:
        x2 = _rms(x2) * fin_ref[...]
    out_ref[...] = x2


def _merge_ffn(x2d, ys2d, big2d, w_branch, w_out, norm2_w, w_up, w_down, final_w, *, layer, tm, last):
    m, d = x2d.shape
    w = MIX_WIDTH
    d_ff = w_up.shape[2]
    gate_block = N_BRANCH * d
    assert m % tm == 0 and (N_MIX_SEGS * w) % gate_block == 0
    gate_idx = (N_MIX_SEGS * w) // gate_block

    def per_layer(shape):
        return pl.BlockSpec((None,) + shape, lambda i: (layer,) + (0,) * len(shape), pipeline_mode=pl.Buffered(1))

    return pl.pallas_call(
        functools.partial(_merge_ffn_kernel, last=last),
        grid=(m // tm,),
        in_specs=[
            pl.BlockSpec((tm, d), lambda i: (i, 0)),
            pl.BlockSpec((tm, N_BRANCH * w), lambda i: (i, 0)),
            pl.BlockSpec((tm, gate_block), lambda i: (i, gate_idx)),
            per_layer((N_BRANCH, w, d)), per_layer((d, d)), per_layer((1, d)), per_layer((d, d_ff)),
            per_layer((d_ff, d)),
            pl.BlockSpec((1, d), lambda i: (0, 0), pipeline_mode=pl.Buffered(1)),
        ],
        out_specs=pl.BlockSpec((tm, d), lambda i: (i, 0)),
        out_shape=jax.ShapeDtypeStruct((m, d), F32),
        compiler_params=pltpu.CompilerParams(
            dimension_semantics=("parallel",), vmem_limit_bytes=VMEM_LIMIT_BYTES),
        name="merge_ffn",
    )(x2d, ys2d, big2d, w_branch, w_out, norm2_w, w_up, w_down, final_w)


def _split_w_in(w_in):
    w, nh = MIX_WIDTH, N_HEADS
    a0, a1 = 4 * w, 4 * w + 2 * nh
    b0, b1 = a1 + 4 * w, a1 + 4 * w + 2 * nh
    w_big = tuple(part.astype(BF16) for part in (w_in[:, :, :a0], w_in[:, :, a1:b0], w_in[:, :, b1:]))
    small = jnp.concatenate([w_in[:, :, a0:a1], w_in[:, :, b0:b1]], axis=2)
    w_small = jnp.pad(small, ((0, 0), (0, 0), (0, SMALL_COLS - 4 * nh))).astype(BF16)
    return w_big, w_small


def _pad_lanes(parts, total):
    row = jnp.concatenate([p.reshape(1, -1).astype(F32) for p in parts], axis=1)
    return jnp.pad(row, ((0, 0), (0, total - row.shape[1])))


def _tiles(m):
    return math.gcd(m, 256), math.gcd(m, 256)


def _run_group(x3, states_in, weights, mix_params, seqs_per_step):
    b, t, d = x3.shape
    depth = len(mix_params)
    chunk = CHUNK if t % CHUNK == 0 else BF16_ROWS
    assert t % CHUNK == 0 or t <= BF16_ROWS
    x2d = x3.reshape(b * t, d)
    tm, tm_merge = _tiles(b * t)
    states_prev = None
    for l in range(depth):
        big, small = _in_proj(x2d, weights["norm1_w"], weights["w_big"], weights["w_small"], layer=l, tm=tm)
        ys, states_prev = _mixers(
            big, small, mix_params[l], states_in, states_prev,
            b=b, t=t, layer=l, depth=depth, chunk=chunk, bb=math.gcd(b, seqs_per_step))
        x2d = _merge_ffn(x2d, ys, big, weights["w_branch"], weights["w_out"],
                         weights["norm2_w"], weights["w_up"], weights["w_down"], weights["final_norm_w"],
                         layer=l, tm=tm_merge, last=(l == depth - 1))
    return x2d.reshape(b, t, d), states_prev


def kernel(x_prompt, x_sample, state_mlstm_C, state_mlstm_n, state_mlstm_m, state_gdn_S, state_gdn_conv, state_hgrn_S, norm1_w, w_in, ml_i_bias, ml_f_bias, ml_norm_w, gd_conv_w, gd_A_log, gd_dt_bias, gd_norm_w, hg_lb_logits, hg_norm_w, w_branch, w_out, norm2_w, w_up, w_down, final_norm_w):
    depth, d = norm1_w.shape
    nh = N_HEADS
    w_big, w_small = _split_w_in(w_in)
    weights = dict(
        norm1_w=norm1_w.reshape(depth, 1, d), w_big=w_big, w_small=w_small,
        w_branch=w_branch.astype(BF16), w_out=w_out.astype(BF16), norm2_w=norm2_w.reshape(depth, 1, d),
        w_up=w_up.astype(BF16), w_down=w_down.astype(BF16), final_norm_w=final_norm_w.reshape(1, d))
    mix_params = []
    for l in range(depth):
        zeros_h = jnp.zeros((nh,), F32)
        bias_row = _pad_lanes([ml_i_bias[l], ml_f_bias[l], zeros_h, gd_dt_bias[l]], SMALL_COLS)
        alog_row = _pad_lanes([zeros_h, zeros_h, zeros_h, gd_A_log[l]], SMALL_COLS)
        mix_params.append((bias_row, alog_row, ml_norm_w[l].reshape(1, -1), gd_conv_w[l],
                           gd_norm_w[l].reshape(1, -1), hg_lb_logits, hg_norm_w[l].reshape(1, -1)))

    def unpack(states):
        s_c, s_n, s_m, s_g, s_cv, s_h = states
        return s_c, s_n.reshape(s_n.shape[:3] + s_n.shape[4:]), s_m[:, :, 0, :nh], s_g, s_cv, s_h

    y_prompt, p_states = _run_group(x_prompt, None, weights, mix_params, PROMPT_SEQS_PER_STEP)

    bs = x_sample.shape[0]
    m_in = jnp.pad(state_mlstm_m, ((0, 0), (0, 0), (0, SMALL_COLS - nh))).reshape(depth, bs, 1, SMALL_COLS)
    n_in = state_mlstm_n.reshape(depth, bs, nh, 1, HEAD_DIM)
    s_in = (state_mlstm_C, n_in, m_in, state_gdn_S, state_gdn_conv, state_hgrn_S)
    y_sample, s_states = _run_group(x_sample, s_in, weights, mix_params, SAMPLE_SEQS_PER_STEP)

    return (y_prompt, y_sample) + unpack(p_states) + unpack(s_states)
```

```python
import functools
import math

import jax
import jax.numpy as jnp
from jax import lax
from jax.experimental import pallas as pl
from jax.experimental.pallas import tpu as pltpu

F32 = jnp.float32
BF16 = jnp.bfloat16

HEAD_DIM = 128
N_HEADS = 4
MIX_WIDTH = N_HEADS * HEAD_DIM
N_BRANCH = 3
CONV_W = 4
EPS = 1e-6
CHUNK = 128
SUBLANES = 8
BF16_ROWS = 16
SMALL_COLS = 128
NEG = -1e30
VMEM_LIMIT_BYTES = 56 * 1024 * 1024
PROJ_TN = 1024
PROMPT_SEQS_PER_STEP = 2
SAMPLE_SEQS_PER_STEP = 8
STAGES_PER_ROUND = (1, 4, 2)

SEG_ML_Q, SEG_ML_K, SEG_ML_V, SEG_ML_O = 0, 1, 2, 3
SEG_GD_Q, SEG_GD_K, SEG_GD_V, SEG_GD_Z = 4, 5, 6, 7
SEG_HG_Q, SEG_HG_F, SEG_HG_I, SEG_HG_G = 8, 9, 10, 11
N_MIX_SEGS = 12
SM_I, SM_F, SM_B, SM_A = 0, N_HEADS, 2 * N_HEADS, 3 * N_HEADS


def _dot(a, b):
    return jnp.dot(a.astype(BF16), b.astype(BF16), preferred_element_type=F32)


def _dot_nt(a, b):
    return lax.dot_general(a.astype(BF16), b.astype(BF16), (((1,), (1,)), ((), ())),
                           preferred_element_type=F32)


def _dot_tn(a, b):
    return _dot(a.T, b)


def _f32_dot(a, b):
    return jnp.dot(a, b, preferred_element_type=F32)


def _split3(a):
    hi = a.astype(BF16)
    r = a - hi.astype(F32)
    mid = r.astype(BF16)
    lo = (r - mid.astype(F32)).astype(BF16)
    return hi, mid, lo


def _dot_exact_lhs(a01, b):
    a = a01.astype(BF16)
    return sum(_f32_dot(a, part) for part in _split3(b))


def _dot_exact_rhs(a, b01):
    b = b01.astype(BF16)
    return sum(_f32_dot(part, b) for part in _split3(a))


def _sigmoid(x):
    return 1.0 / (1.0 + jnp.exp(-x))


def _silu(x):
    return x * _sigmoid(x)


def _softplus(x):
    return jnp.maximum(x, 0.0) + jnp.log(1.0 + jnp.exp(-jnp.abs(x)))


def _log_sigmoid(x):
    return -_softplus(-x)


def _rms(x, axis=-1):
    return x * lax.rsqrt(jnp.mean(x * x, axis=axis, keepdims=True) + EPS)


def _in_proj_kernel(x_ref, nw_ref, *refs):
    *wbig_refs, wsmall_ref, big_ref, small_ref = refs
    xn = (_rms(x_ref[...]) * nw_ref[...]).astype(BF16)
    small_ref[...] = jnp.dot(xn, wsmall_ref[...], preferred_element_type=F32)
    out0 = 0
    for wbig_ref in wbig_refs:
        for n0 in range(0, wbig_ref.shape[1], PROJ_TN):
            big_ref[:, out0 + n0:out0 + n0 + PROJ_TN] = jnp.dot(
                xn, wbig_ref[:, n0:n0 + PROJ_TN], preferred_element_type=F32)
        out0 += wbig_ref.shape[1]


def _in_proj(x2d, norm_w, w_big, w_small, *, layer, tm):
    m, d = x2d.shape
    n_big = sum(wg.shape[2] for wg in w_big)
    assert m % tm == 0 and all(wg.shape[2] % PROJ_TN == 0 for wg in w_big)
    resident = dict(pipeline_mode=pl.Buffered(1))
    return pl.pallas_call(
        _in_proj_kernel,
        grid=(m // tm,),
        in_specs=[
            pl.BlockSpec((tm, d), lambda i: (i, 0)),
            pl.BlockSpec((None, 1, d), lambda i: (layer, 0, 0), **resident),
            *[pl.BlockSpec((None, d, wg.shape[2]), lambda i: (layer, 0, 0), **resident) for wg in w_big],
            pl.BlockSpec((None, d, SMALL_COLS), lambda i: (layer, 0, 0), **resident),
        ],
        out_specs=[
            pl.BlockSpec((tm, n_big), lambda i: (i, 0)),
            pl.BlockSpec((tm, SMALL_COLS), lambda i: (i, 0)),
        ],
        out_shape=[
            jax.ShapeDtypeStruct((m, n_big), F32),
            jax.ShapeDtypeStruct((m, SMALL_COLS), F32),
        ],
        compiler_params=pltpu.CompilerParams(
            dimension_semantics=("parallel",), vmem_limit_bytes=VMEM_LIMIT_BYTES),
        name="in_proj",
    )(x2d, norm_w, *w_big, w_small)


def _heads(fn, *xs):
    return jnp.stack([fn(*(x[h] for x in xs)) for h in range(xs[0].shape[0])])


def _interleave(stage_generators, stages_per_round):
    results = [None] * len(stage_generators)
    live = list(enumerate(stage_generators))
    while live:
        still = []
        for idx, gen in live:
            try:
                for _ in range(stages_per_round[idx]):
                    next(gen)
                still.append((idx, gen))
            except StopIteration as stop:
                results[idx] = stop.value
        live = still
    return results


def _inv_unit_lower(n, level_masks):
    c = n.shape[-1]
    eye = (lax.broadcasted_iota(jnp.int32, (c, c), 0) == lax.broadcasted_iota(jnp.int32, (c, c), 1)).astype(F32)
    n_bf = n.astype(BF16)
    zero = jnp.zeros((), BF16)
    d = eye - jnp.where(level_masks[0], n, 0.0)
    for mask in level_masks[1:]:
        c_bf = jnp.where(mask, n_bf, zero)
        d_bf = d.astype(BF16)
        t = _heads(_f32_dot, d_bf, c_bf)
        yield
        d = d - _heads(_f32_dot, t.astype(BF16), d_bf)
        yield
    return d


def _mlstm_chunk(q, k, v, qk, qc, i_col, b_col, i_row, b_row, causal, c_state, n_state, m_state):
    c = q.shape[1]
    a_col = b_col + m_state
    d = jnp.where(causal, b_col - b_row + i_row, NEG)
    mt = jnp.maximum(a_col, jnp.max(d, axis=2, keepdims=True))
    p = qk * jnp.exp(d - mt)
    w_st = jnp.exp(a_col - mt)
    yield
    num = _heads(_dot, p, v) + w_st * qc
    yield
    den = jnp.sum(p, axis=2, keepdims=True) + w_st * jnp.sum(q * n_state, axis=2, keepdims=True)
    h = num / jnp.maximum(jnp.abs(den), jnp.exp(-mt))
    m_new = mt[:, c - 1:c, :]
    b_last = b_col[:, c - 1:c, :]
    ws_col = jnp.exp(b_last - b_col + i_col - m_new)
    wc = jnp.exp(b_last + m_state - m_new)
    kw = k * ws_col
    yield
    c_new = wc * c_state + _heads(_dot_tn, kw, v)
    n_new = wc * n_state + jnp.sum(kw, axis=1, keepdims=True)
    return h, c_new, n_new, m_new


def _gdn_chunk(q, k, v, qk_kk, beta_col, g_col, g_row, incl, strict, level_masks, s_state):
    c = q.shape[1]
    dh = v.shape[2]
    decay = jnp.exp(jnp.where(incl, g_col - g_row, NEG))
    qk = qk_kk[:, :c]
    kk = qk_kk[:, c:]
    n = jnp.where(strict, beta_col * kk * decay, 0.0)
    yield
    a_inv = yield from _inv_unit_lower(n, level_masks)
    eg_col = jnp.exp(g_col)
    rhs = jnp.concatenate([beta_col * v, (beta_col * eg_col) * k], axis=2)
    yield
    sol = _heads(_dot, a_inv, rhs)
    u = sol[:, :, :dh]
    w = sol[:, :, dh:]
    yield
    wq_s = _heads(_dot, jnp.concatenate([w, q * eg_col], axis=1), s_state)
    v_new = u - wq_s[:, :c]
    yield
    o = wq_s[:, c:] + _heads(_dot, qk * decay, v_new)
    g_last = g_col[:, c - 1:c, :]
    yield
    s_new = jnp.exp(g_last) * s_state + _heads(_dot_tn, k * jnp.exp(g_last - g_col), v_new)
    return o, s_new


def _row_block_bcast(x, block, row):
    nh, c, l = x.shape
    x3 = x.reshape(nh * c // block, block, l)
    return jnp.broadcast_to(x3[:, row:row + 1, :], x3.shape).reshape(nh, c, l)


def _boundary_rows(x, hb):
    if 2 * hb >= SUBLANES:
        return _row_block_bcast(x, 2 * hb, hb - 1)
    sub = lax.broadcasted_iota(jnp.int32, (1, x.shape[1], 1), 1) % SUBLANES
    out = _row_block_bcast(x, SUBLANES, hb - 1)
    for start in range(2 * hb, SUBLANES, 2 * hb):
        out = jnp.where(sub >= start, _row_block_bcast(x, SUBLANES, start + hb - 1), out)
    return out


def _half_block_sizes(c):
    assert c & (c - 1) == 0
    return [1 << e for e in range(c.bit_length() - 1)]


def _level_masks(c):
    t = lax.broadcasted_iota(jnp.int32, (c, c), 0)
    s = lax.broadcasted_iota(jnp.int32, (c, c), 1)
    masks = []
    for hb in _half_block_sizes(c):
        same = (t // (2 * hb)) == (s // (2 * hb))
        masks.append(same & ((t % (2 * hb)) >= hb) & ((s % (2 * hb)) < hb))
    return masks


def _hgrn_chunk(q, k, v, bc, level_masks, s_state):
    nh, c, dh = q.shape
    diag = lax.broadcasted_iota(jnp.int32, (c, c), 0) == lax.broadcasted_iota(jnp.int32, (c, c), 1)
    o = _heads(_dot, q * jnp.exp(bc), s_state)
    yield

    a = jnp.where(diag, jnp.sum(q * k, axis=2, keepdims=True), 0.0)
    for hb, mask in zip(_half_block_sizes(c), level_masks):
        e = jnp.exp(-jnp.abs(bc - _boundary_rows(bc, hb)))
        a = a + jnp.where(mask, _heads(_dot_nt, q * e, k * e), 0.0)
        yield
    o = o + _heads(_dot, a, v)
    yield

    b_last = bc[:, c - 1:c, :]
    decay_col = _heads(lambda r: jnp.broadcast_to(r, (SUBLANES, dh)).T[:, 0:1], jnp.exp(b_last))
    s_new = decay_col * s_state + _heads(_dot_tn, k * jnp.exp(b_last - bc), v)
    return o, s_new


def _mixers_kernel(*refs, layer, chunk, n_valid, bb, has_state):
    if has_state:
        (big_ref, small_ref, bias_ref, alog_ref, mlnw_ref, convw_ref, gdnw_ref, lbl_ref, hgnw_ref,
         c0_ref, n0_ref, m0_ref, sg0_ref, cv0_ref, sh0_ref,
         ys_ref, c_ref, n_ref, m_ref, sg_ref, cv_ref, sh_ref, convbuf) = refs
    else:
        (big_ref, small_ref, bias_ref, alog_ref, mlnw_ref, convw_ref, gdnw_ref, lbl_ref, hgnw_ref,
         ys_ref, c_ref, n_ref, m_ref, sg_ref, cv_ref, sh_ref, convbuf) = refs
    j = pl.program_id(1)
    nj = pl.num_programs(1)
    c = chunk
    flat = len(big_ref.shape) == 2
    w = MIX_WIDTH
    dh = HEAD_DIM
    nh = N_HEADS
    tail = CONV_W - 1

    @pl.when(j == 0)
    def _():
        if has_state:
            c_ref[...] = c0_ref[...]
            n_ref[...] = n0_ref[...]
            m_ref[...] = m0_ref[...]
            sg_ref[...] = sg0_ref[...]
            sh_ref[...] = sh0_ref[...]
            convbuf[:, SUBLANES - tail:SUBLANES, :] = cv0_ref[...]
        else:
            c_ref[...] = jnp.zeros_like(c_ref)
            n_ref[...] = jnp.zeros_like(n_ref)
            m_ref[...] = jnp.zeros_like(m_ref)
            sg_ref[...] = jnp.zeros_like(sg_ref)
            sh_ref[...] = jnp.zeros_like(sh_ref)
            convbuf[:, SUBLANES - tail:SUBLANES, :] = jnp.zeros((bb, tail, N_BRANCH * w), F32)

    def load(ref, lo, hi):
        if not flat:
            return ref[:, :, lo:hi]
        x = ref[:, lo:hi]
        zeros = jnp.zeros((c - n_valid, hi - lo), x.dtype)
        return jnp.stack([jnp.concatenate([x[b * n_valid:(b + 1) * n_valid], zeros], axis=0) for b in range(bb)])

    def store(col0, y):
        if not flat:
            ys_ref[:, :, col0:col0 + y.shape[2]] = y.astype(ys_ref.dtype)
        else:
            rows_out = jnp.concatenate([y[b, :n_valid] for b in range(bb)], axis=0)
            ys_ref[:, col0:col0 + y.shape[2]] = rows_out.astype(ys_ref.dtype)

    def seg(i):
        return load(big_ref, i * w, (i + 1) * w)

    def per_seq(fn, x):
        return jnp.stack([fn(x[b]) for b in range(bb)])

    def head_major(x):
        return jnp.stack([x[b][:, h * dh:(h + 1) * dh] for b in range(bb) for h in range(nh)])

    def token_major(x):
        return jnp.stack([jnp.concatenate([x[b * nh + h] for h in range(nh)], axis=1) for b in range(bb)])

    def cols(x, first):
        return jnp.stack([x[b][:, first + h:first + h + 1] for b in range(bb) for h in range(nh)])

    def rows(x, first):
        return jnp.stack([x[b][first + h:first + h + 1, :] for b in range(bb) for h in range(nh)])

    lane = lax.broadcasted_iota(jnp.int32, (1, c, SMALL_COLS), 2)
    rowi = lax.broadcasted_iota(jnp.int32, (1, c, SMALL_COLS), 1)
    pre = load(small_ref, 0, SMALL_COLS) + bias_ref[...]
    is_f = (lane >= SM_F) & (lane < SM_B)
    is_b = (lane >= SM_B) & (lane < SM_A)
    is_a = (lane >= SM_A) & (lane < SM_A + nh)
    logg = -jnp.exp(alog_ref[...]) * _softplus(pre)
    gates = jnp.where(is_f, _log_sigmoid(pre), jnp.where(is_b, _sigmoid(pre), jnp.where(is_a, logg, pre)))
    gates = jnp.where(rowi < n_valid, gates, jnp.where(lane < SM_F, NEG, 0.0))
    tt = lax.broadcasted_iota(jnp.int32, (c, c), 0)
    ss = lax.broadcasted_iota(jnp.int32, (c, c), 1)
    incl = tt >= ss
    strict = tt > ss
    ltri = incl.astype(F32)
    utri = (ss >= tt).astype(F32)
    cum_src = jnp.where(is_f | is_a, gates, 0.0)
    cums = per_seq(lambda x: _dot_exact_lhs(ltri, x), cum_src)
    gates_t = per_seq(lambda x: x.T, gates)
    cums_t = per_seq(lambda x: _dot_exact_rhs(x.T, utri), cum_src)

    lbl = lbl_ref[...]
    sm = jnp.exp(lbl - jnp.max(lbl, axis=0, keepdims=True))
    sm = sm / jnp.sum(sm, axis=0, keepdims=True)
    lb = jnp.zeros((1, w), F32)
    for l in range(1, layer + 1):
        lb = lb + sm[l:l + 1, :]
    valid = lax.broadcasted_iota(jnp.int32, (1, c, w), 1) < n_valid
    hf = seg(SEG_HG_F)
    g_f = lb + (1.0 - lb) * _sigmoid(hf)
    k_hg = jnp.where(valid, (1.0 - lb) * _sigmoid(-hf), 0.0)
    f_log = jnp.where(valid, jnp.log(g_f), 0.0)
    bc = per_seq(lambda x: _dot_exact_lhs(ltri, x), f_log)

    convbuf[:, SUBLANES:SUBLANES + c, :] = load(big_ref, SEG_GD_Q * w, (SEG_GD_V + 1) * w)
    conv = convw_ref[CONV_W - 1:CONV_W, :] * convbuf[:, SUBLANES:SUBLANES + c, :]
    for jj in range(CONV_W - 1):
        off = SUBLANES - tail + jj
        conv = conv + convw_ref[jj:jj + 1, :] * convbuf[:, off:off + c, :]
    conv = _silu(conv)
    new_tail = convbuf[:, SUBLANES + n_valid - tail:SUBLANES + n_valid, :]
    convbuf[:, SUBLANES - tail:SUBLANES, :] = new_tail
    cq = head_major(conv[:, :, 0:w])
    ck = head_major(conv[:, :, w:2 * w])
    cv = head_major(conv[:, :, 2 * w:3 * w])
    qn = cq * lax.rsqrt(jnp.sum(cq * cq, axis=2, keepdims=True) + EPS) * (dh ** -0.5)
    kn = ck * lax.rsqrt(jnp.sum(ck * ck, axis=2, keepdims=True) + EPS)
    qk_kk = _heads(_dot_nt, jnp.concatenate([qn, kn], axis=1), kn)

    ml_q = head_major(seg(SEG_ML_Q))
    ml_k = head_major(seg(SEG_ML_K)) * (dh ** -0.5)
    c_old = c_ref[...].reshape(bb * nh, dh, dh)
    ml_qk = _heads(_dot_nt, ml_q, ml_k)
    ml_qc = _heads(_dot, ml_q, c_old)
    m_old = jnp.stack([m_ref[b][:, h:h + 1] for b in range(bb) for h in range(nh)])

    mlstm = _mlstm_chunk(
        ml_q, ml_k, head_major(seg(SEG_ML_V)), ml_qk, ml_qc,
        cols(gates, SM_I), cols(cums, SM_F), rows(gates_t, SM_I), rows(cums_t, SM_F), incl,
        c_old, n_ref[...].reshape(bb * nh, 1, dh), m_old)
    level_masks = _level_masks(c)
    gdn = _gdn_chunk(qn, kn, cv, qk_kk, cols(gates, SM_B), cols(cums, SM_A), rows(cums_t, SM_A),
                     incl, strict, level_masks, sg_ref[...].reshape(bb * nh, dh, dh))
    hgrn = _hgrn_chunk(head_major(_silu(seg(SEG_HG_Q))), head_major(k_hg), head_major(seg(SEG_HG_I)),
                       head_major(bc), level_masks, sh_ref[...].reshape(bb * nh, dh, dh))
    (hh, c_new, n_new, m_new), (o_gd, sg_new), (o_hg, sh_new) = _interleave(
        (mlstm, gdn, hgrn), STAGES_PER_ROUND)

    c_ref[...] = c_new.reshape(c_ref.shape)
    n_ref[...] = n_new.reshape(n_ref.shape)
    m_lane = lax.broadcasted_iota(jnp.int32, (1, SMALL_COLS), 1)
    m_rows = []
    for b in range(bb):
        m_row = jnp.zeros((1, SMALL_COLS), F32)
        for h in range(nh):
            m_row = jnp.where(m_lane == h, m_new[b * nh + h], m_row)
        m_rows.append(m_row)
    m_ref[...] = jnp.stack(m_rows)
    y_ml = token_major(_rms(hh)) * mlnw_ref[...] * _sigmoid(seg(SEG_ML_O))
    store(0, y_ml)

    sg_ref[...] = sg_new.reshape(sg_ref.shape)
    y_gd = token_major(_rms(o_gd) * gdnw_ref[...]) * _silu(seg(SEG_GD_Z))
    store(w, y_gd)

    sh_ref[...] = sh_new.reshape(sh_ref.shape)
    y_hg = _rms(token_major(o_hg)) * hgnw_ref[...] * _silu(seg(SEG_HG_G))
    store(2 * w, y_hg)

    @pl.when(j == nj - 1)
    def _():
        cv_ref[...] = convbuf[:, SUBLANES - tail:SUBLANES, :]


def _mixers(big2, small2, params, states_in, states_prev, *, b, t, layer, depth, chunk, bb):
    rows = min(t, chunk)
    assert t % rows == 0 and chunk % BF16_ROWS == 0 and CONV_W - 1 <= rows and b % bb == 0
    assert big2.shape[0] == b * t
    w, dh, nh = MIX_WIDTH, HEAD_DIM, N_HEADS
    has_state = states_in is not None
    bias_row, alog_row, ml_norm_w, conv_w, gd_norm_w, lb_logits, hg_norm_w = params

    def const(shape):
        return pl.BlockSpec(shape, lambda i, j: (0,) * len(shape))

    def st(shape):
        return pl.BlockSpec((None, bb) + shape, lambda i, j: (layer, i) + (0,) * len(shape))

    flat = rows < chunk

    def view_shape(l):
        return (b * t, l) if flat else (b, t, l)

    def view(x):
        return x.reshape(view_shape(x.shape[-1]))

    def rows_spec(l):
        if flat:
            return pl.BlockSpec((bb * rows, l), lambda i, j: (i, 0))
        return pl.BlockSpec((bb, rows, l), lambda i, j: (i, j, 0))

    state_shapes = [(nh, dh, dh), (nh, 1, dh), (1, SMALL_COLS), (nh, dh, dh), (CONV_W - 1, N_BRANCH * w), (nh, dh, dh)]
    in_specs = [
        rows_spec(N_MIX_SEGS * w), rows_spec(SMALL_COLS),
        const((1, SMALL_COLS)), const((1, SMALL_COLS)), const((1, w)), const((CONV_W, N_BRANCH * w)),
        const((1, dh)), const((depth, w)), const((1, w)),
    ]
    args = [view(big2), view(small2), bias_row, alog_row, ml_norm_w, conv_w, gd_norm_w, lb_logits, hg_norm_w]
    if has_state:
        in_specs += [st(s) for s in state_shapes]
        args += list(states_in)
    aliases = {}
    if states_prev is not None:
        for k_out, arr in enumerate(states_prev):
            aliases[len(args)] = 1 + k_out
            in_specs.append(pl.BlockSpec(memory_space=pl.ANY))
            args.append(arr)
    out_specs = [rows_spec(N_BRANCH * w)] + [st(s) for s in state_shapes]
    out_shape = [jax.ShapeDtypeStruct(view_shape(N_BRANCH * w), BF16)] + [
        jax.ShapeDtypeStruct((depth, b) + s, F32) for s in state_shapes]

    def body(*refs):
        n_in = len(args) - (len(states_prev) if states_prev is not None else 0)
        kept = refs[:n_in] + refs[len(args):]
        _mixers_kernel(*kept, layer=layer, chunk=chunk, n_valid=rows, bb=bb, has_state=has_state)

    outs = pl.pallas_call(
        body,
        grid=(b // bb, t // rows),
        in_specs=in_specs,
        out_specs=out_specs,
        out_shape=out_shape,
        scratch_shapes=[pltpu.VMEM((bb, SUBLANES + chunk, N_BRANCH * w), F32)],
        input_output_aliases=aliases,
        compiler_params=pltpu.CompilerParams(
            dimension_semantics=("parallel", "arbitrary"), vmem_limit_bytes=VMEM_LIMIT_BYTES),
        name="mixers",
    )(*args)
    return outs[0].reshape(b * t, N_BRANCH * w), tuple(outs[1:])


def _merge_ffn_kernel(x_ref, ys_ref, gate_ref, wbr_ref, wout_ref, n2_ref, wup_ref, wdown_ref, fin_ref,
                      out_ref, *, last):
    w = MIX_WIDTH
    d = x_ref.shape[1]
    merged = jnp.zeros(x_ref.shape, F32)
    for n in range(N_BRANCH):
        br = jnp.dot(ys_ref[:, n * w:(n + 1) * w], wbr_ref[n], preferred_element_type=F32)
        merged = merged + _sigmoid(gate_ref[:, n * d:(n + 1) * d]) * br
    x1 = x_ref[...] + jnp.dot(merged.astype(BF16), wout_ref[...], preferred_element_type=F32)
    xn = (_rms(x1) * n2_ref[...]).astype(BF16)
    hid = jnp.square(jnp.maximum(jnp.dot(xn, wup_ref[...], preferred_element_type=F32), 0.0))
    x2 = x1 + jnp.dot(hid.astype(BF16), wdown_ref[...], preferred_element_type=F32)
    if last:
        x2 = _rms(x2) * fin_ref[...]
    out_ref[...] = x2


def _merge_ffn(x2d, ys2d, big2d, w_branch, w_out, norm2_w, w_up, w_down, final_w, *, layer, tm, last):
    m, d = x2d.shape
    w = MIX_WIDTH
    d_ff = w_up.shape[2]
    gate_block = N_BRANCH * d
    assert m % tm == 0 and (N_MIX_SEGS * w) % gate_block == 0
    gate_idx = (N_MIX_SEGS * w) // gate_block

    def per_layer(shape):
        return pl.BlockSpec((None,) + shape, lambda i: (layer,) + (0,) * len(shape), pipeline_mode=pl.Buffered(1))

    return pl.pallas_call(
        functools.partial(_merge_ffn_kernel, last=last),
        grid=(m // tm,),
        in_specs=[
            pl.BlockSpec((tm, d), lambda i: (i, 0)),
            pl.BlockSpec((tm, N_BRANCH * w), lambda i: (i, 0)),
            pl.BlockSpec((tm, gate_block), lambda i: (i, gate_idx)),
            per_layer((N_BRANCH, w, d)), per_layer((d, d)), per_layer((1, d)), per_layer((d, d_ff)),
            per_layer((d_ff, d)),
            pl.BlockSpec((1, d), lambda i: (0, 0), pipeline_mode=pl.Buffered(1)),
        ],
        out_specs=pl.BlockSpec((tm, d), lambda i: (i, 0)),
        out_shape=jax.ShapeDtypeStruct((m, d), F32),
        compiler_params=pltpu.CompilerParams(
            dimension_semantics=("parallel",), vmem_limit_bytes=VMEM_LIMIT_BYTES),
        name="merge_ffn",
    )(x2d, ys2d, big2d, w_branch, w_out, norm2_w, w_up, w_down, final_w)


def _split_w_in(w_in):
    w, nh = MIX_WIDTH, N_HEADS
    a0, a1 = 4 * w, 4 * w + 2 * nh
    b0, b1 = a1 + 4 * w, a1 + 4 * w + 2 * nh
    w_big = tuple(part.astype(BF16) for part in (w_in[:, :, :a0], w_in[:, :, a1:b0], w_in[:, :, b1:]))
    small = jnp.concatenate([w_in[:, :, a0:a1], w_in[:, :, b0:b1]], axis=2)
    w_small = jnp.pad(small, ((0, 0), (0, 0), (0, SMALL_COLS - 4 * nh))).astype(BF16)
    return w_big, w_small


def _pad_lanes(parts, total):
    row = jnp.concatenate([p.reshape(1, -1).astype(F32) for p in parts], axis=1)
    return jnp.pad(row, ((0, 0), (0, total - row.shape[1])))


def _tiles(m):
    return math.gcd(m, 256), math.gcd(m, 256)


def _run_group(x3, states_in, weights, mix_params, seqs_per_step):
    b, t, d = x3.shape
    depth = len(mix_params)
    chunk = CHUNK if t % CHUNK == 0 else BF16_ROWS
    assert t % CHUNK == 0 or t <= BF16_ROWS
    x2d = x3.reshape(b * t, d)
    tm, tm_merge = _tiles(b * t)
    states_prev = None
    for l in range(depth):
        big, small = _in_proj(x2d, weights["norm1_w"], weights["w_big"], weights["w_small"], layer=l, tm=tm)
        ys, states_prev = _mixers(
            big, small, mix_params[l], states_in, states_prev,
            b=b, t=t, layer=l, depth=depth, chunk=chunk, bb=math.gcd(b, seqs_per_step))
        x2d = _merge_ffn(x2d, ys, big, weights["w_branch"], weights["w_out"],
                         weights["norm2_w"], weights["w_up"], weights["w_down"], weights["final_norm_w"],
                         layer=l, tm=tm_merge, last=(l == depth - 1))
    return x2d.reshape(b, t, d), states_prev


def kernel(x_prompt, x_sample, state_mlstm_C, state_mlstm_n, state_mlstm_m, state_gdn_S, state_gdn_conv, state_hgrn_S, norm1_w, w_in, ml_i_bias, ml_f_bias, ml_norm_w, gd_conv_w, gd_A_log, gd_dt_bias, gd_norm_w, hg_lb_logits, hg_norm_w, w_branch, w_out, norm2_w, w_up, w_down, final_norm_w):
    depth, d = norm1_w.shape
    nh = N_HEADS
    w_big, w_small = _split_w_in(w_in)
    weights = dict(
        norm1_w=norm1_w.reshape(depth, 1, d), w_big=w_big, w_small=w_small,
        w_branch=w_branch.astype(BF16), w_out=w_out.astype(BF16), norm2_w=norm2_w.reshape(depth, 1, d),
        w_up=w_up.astype(BF16), w_down=w_down.astype(BF16), final_norm_w=final_norm_w.reshape(1, d))
    mix_params = []
    for l in range(depth):
        zeros_h = jnp.zeros((nh,), F32)
        bias_row = _pad_lanes([ml_i_bias[l], ml_f_bias[l], zeros_h, gd_dt_bias[l]], SMALL_COLS)
        alog_row = _pad_lanes([zeros_h, zeros_h, zeros_h, gd_A_log[l]], SMALL_COLS)
        mix_params.append((bias_row, alog_row, ml_norm_w[l].reshape(1, -1), gd_conv_w[l],
                           gd_norm_w[l].reshape(1, -1), hg_lb_logits, hg_norm_w[l].reshape(1, -1)))

    def unpack(states):
        s_c, s_n, s_m, s_g, s_cv, s_h = states
        return s_c, s_n.reshape(s_n.shape[:3] + s_n.shape[4:]), s_m[:, :, 0, :nh], s_g, s_cv, s_h

    y_prompt, p_states = _run_group(x_prompt, None, weights, mix_params, PROMPT_SEQS_PER_STEP)

    bs = x_sample.shape[0]
    m_in = jnp.pad(state_mlstm_m, ((0, 0), (0, 0), (0, SMALL_COLS - nh))).reshape(depth, bs, 1, SMALL_COLS)
    n_in = state_mlstm_n.reshape(depth, bs, nh, 1, HEAD_DIM)
    s_in = (state_mlstm_C, n_in, m_in, state_gdn_S, state_gdn_conv, state_hgrn_S)
    y_sample, s_states = _run_group(x_sample, s_in, weights, mix_params, SAMPLE_SEQS_PER_STEP)

    return (y_prompt, y_sample) + unpack(p_states) + unpack(s_states)
```

```python
import functools
import math

import jax
import jax.numpy as jnp
from jax import lax
from jax.experimental import pallas as pl
from jax.experimental.pallas import tpu as pltpu

F32 = jnp.float32
BF16 = jnp.bfloat16

HEAD_DIM = 128
N_HEADS = 4
MIX_WIDTH = N_HEADS * HEAD_DIM
N_BRANCH = 3
CONV_W = 4
EPS = 1e-6
CHUNK = 128
SUBLANES = 8
BF16_ROWS = 16
SMALL_COLS = 128
NEG = -1e30
VMEM_LIMIT_BYTES = 56 * 1024 * 1024
PROJ_TN = 1024
PROMPT_SEQS_PER_STEP = 2
SAMPLE_SEQS_PER_STEP = 8
STAGES_PER_ROUND = (1, 4, 2)

SEG_ML_Q, SEG_ML_K, SEG_ML_V, SEG_ML_O = 0, 1, 2, 3
SEG_GD_Q, SEG_GD_K, SEG_GD_V, SEG_GD_Z = 4, 5, 6, 7
SEG_HG_Q, SEG_HG_F, SEG_HG_I, SEG_HG_G = 8, 9, 10, 11
N_MIX_SEGS = 12
SM_I, SM_F, SM_B, SM_A = 0, N_HEADS, 2 * N_HEADS, 3 * N_HEADS


def _dot(a, b):
    return jnp.dot(a.astype(BF16), b.astype(BF16), preferred_element_type=F32)


def _dot_nt(a, b):
    return lax.dot_general(a.astype(BF16), b.astype(BF16), (((1,), (1,)), ((), ())),
                           preferred_element_type=F32)


def _dot_tn(a, b):
    return _dot(a.T, b)


def _f32_dot(a, b):
    return jnp.dot(a, b, preferred_element_type=F32)


def _split3(a):
    hi = a.astype(BF16)
    r = a - hi.astype(F32)
    mid = r.astype(BF16)
    lo = (r - mid.astype(F32)).astype(BF16)
    return hi, mid, lo


def _dot_exact_lhs(a01, b):
    a = a01.astype(BF16)
    return sum(_f32_dot(a, part) for part in _split3(b))


def _dot_exact_rhs(a, b01):
    b = b01.astype(BF16)
    return sum(_f32_dot(part, b) for part in _split3(a))


def _sigmoid(x):
    return 1.0 / (1.0 + jnp.exp(-x))


def _silu(x):
    return x * _sigmoid(x)


def _softplus(x):
    return jnp.maximum(x, 0.0) + jnp.log(1.0 + jnp.exp(-jnp.abs(x)))


def _log_sigmoid(x):
    return -_softplus(-x)


def _rms(x, axis=-1):
    return x * lax.rsqrt(jnp.mean(x * x, axis=axis, keepdims=True) + EPS)


def _in_proj_kernel(x_ref, nw_ref, *refs):
    *wbig_refs, wsmall_ref, big_ref, small_ref = refs
    xn = (_rms(x_ref[...]) * nw_ref[...]).astype(BF16)
    small_ref[...] = jnp.dot(xn, wsmall_ref[...], preferred_element_type=F32)
    out0 = 0
    for wbig_ref in wbig_refs:
        for n0 in range(0, wbig_ref.shape[1], PROJ_TN):
            big_ref[:, out0 + n0:out0 + n0 + PROJ_TN] = jnp.dot(
                xn, wbig_ref[:, n0:n0 + PROJ_TN], preferred_element_type=F32)
        out0 += wbig_ref.shape[1]


def _in_proj(x2d, norm_w, w_big, w_small, *, layer, tm):
    m, d = x2d.shape
    n_big = sum(wg.shape[2] for wg in w_big)
    assert m % tm == 0 and all(wg.shape[2] % PROJ_TN == 0 for wg in w_big)
    resident = dict(pipeline_mode=pl.Buffered(1))
    return pl.pallas_call(
        _in_proj_kernel,
        grid=(m // tm,),
        in_specs=[
            pl.BlockSpec((tm, d), lambda i: (i, 0)),
            pl.BlockSpec((None, 1, d), lambda i: (layer, 0, 0), **resident),
            *[pl.BlockSpec((None, d, wg.shape[2]), lambda i: (layer, 0, 0), **resident) for wg in w_big],
            pl.BlockSpec((None, d, SMALL_COLS), lambda i: (layer, 0, 0), **resident),
        ],
        out_specs=[
            pl.BlockSpec((tm, n_big), lambda i: (i, 0)),
            pl.BlockSpec((tm, SMALL_COLS), lambda i: (i, 0)),
        ],
        out_shape=[
            jax.ShapeDtypeStruct((m, n_big), F32),
            jax.ShapeDtypeStruct((m, SMALL_COLS), F32),
        ],
        compiler_params=pltpu.CompilerParams(
            dimension_semantics=("parallel",), vmem_limit_bytes=VMEM_LIMIT_BYTES),
        name="in_proj",
    )(x2d, norm_w, *w_big, w_small)


def _heads(fn, *xs):
    return jnp.stack([fn(*(x[h] for x in xs)) for h in range(xs[0].shape[0])])


def _interleave(stage_generators, stages_per_round):
    results = [None] * len(stage_generators)
    live = list(enumerate(stage_generators))
    while live:
        still = []
        for idx, gen in live:
            try:
                for _ in range(stages_per_round[idx]):
                    next(gen)
                still.append((idx, gen))
            except StopIteration as stop:
                results[idx] = stop.value
        live = still
    return results


def _inv_unit_lower(n, level_masks):
    c = n.shape[-1]
    eye = (lax.broadcasted_iota(jnp.int32, (c, c), 0) == lax.broadcasted_iota(jnp.int32, (c, c), 1)).astype(F32)
    n_bf = n.astype(BF16)
    zero = jnp.zeros((), BF16)
    d = eye - jnp.where(level_masks[0], n, 0.0)
    for mask in level_masks[1:]:
        c_bf = jnp.where(mask, n_bf, zero)
        d_bf = d.astype(BF16)
        t = _heads(_f32_dot, d_bf, c_bf)
        yield
        d = d - _heads(_f32_dot, t.astype(BF16), d_bf)
        yield
    return d


def _mlstm_chunk(q, k, v, qk, qc, i_col, b_col, i_row, b_row, causal, c_state, n_state, m_state):
    c = q.shape[1]
    a_col = b_col + m_state
    d = jnp.where(causal, b_col - b_row + i_row, NEG)
    mt = jnp.maximum(a_col, jnp.max(d, axis=2, keepdims=True))
    p = qk * jnp.exp(d - mt)
    w_st = jnp.exp(a_col - mt)
    yield
    num = _heads(_dot, p, v) + w_st * qc
    yield
    den = jnp.sum(p, axis=2, keepdims=True) + w_st * jnp.sum(q * n_state, axis=2, keepdims=True)
    h = num / jnp.maximum(jnp.abs(den), jnp.exp(-mt))
    m_new = mt[:, c - 1:c, :]
    b_last = b_col[:, c - 1:c, :]
    ws_col = jnp.exp(b_last - b_col + i_col - m_new)
    wc = jnp.exp(b_last + m_state - m_new)
    kw = k * ws_col
    yield
    c_new = wc * c_state + _heads(_dot_tn, kw, v)
    n_new = wc * n_state + jnp.sum(kw, axis=1, keepdims=True)
    return h, c_new, n_new, m_new


def _gdn_chunk(q, k, v, qk_kk, beta_col, g_col, g_row, incl, strict, level_masks, s_state):
    c = q.shape[1]
    dh = v.shape[2]
    decay = jnp.exp(jnp.where(incl, g_col - g_row, NEG))
    qk = qk_kk[:, :c]
    kk = qk_kk[:, c:]
    n = jnp.where(strict, beta_col * kk * decay, 0.0)
    yield
    a_inv = yield from _inv_unit_lower(n, level_masks)
    eg_col = jnp.exp(g_col)
    rhs = jnp.concatenate([beta_col * v, (beta_col * eg_col) * k], axis=2)
    yield
    sol = _heads(_dot, a_inv, rhs)
    u = sol[:, :, :dh]
    w = sol[:, :, dh:]
    yield
    wq_s = _heads(_dot, jnp.concatenate([w, q * eg_col], axis=1), s_state)
    v_new = u - wq_s[:, :c]
    yield
    o = wq_s[:, c:] + _heads(_dot, qk * decay, v_new)
    g_last = g_col[:, c - 1:c, :]
    yield
    s_new = jnp.exp(g_last) * s_state + _heads(_dot_tn, k * jnp.exp(g_last - g_col), v_new)
    return o, s_new


def _row_block_bcast(x, block, row):
    nh, c, l = x.shape
    x3 = x.reshape(nh * c // block, block, l)
    return jnp.broadcast_to(x3[:, row:row + 1, :], x3.shape).reshape(nh, c, l)


def _boundary_rows(x, hb):
    if 2 * hb >= SUBLANES:
        return _row_block_bcast(x, 2 * hb, hb - 1)
    sub = lax.broadcasted_iota(jnp.int32, (1, x.shape[1], 1), 1) % SUBLANES
    out = _row_block_bcast(x, SUBLANES, hb - 1)
    for start in range(2 * hb, SUBLANES, 2 * hb):
        out = jnp.where(sub >= start, _row_block_bcast(x, SUBLANES, start + hb - 1), out)
    return out


def _half_block_sizes(c):
    assert c & (c - 1) == 0
    return [1 << e for e in range(c.bit_length() - 1)]


def _level_masks(c):
    t = lax.broadcasted_iota(jnp.int32, (c, c), 0)
    s = lax.broadcasted_iota(jnp.int32, (c, c), 1)
    masks = []
    for hb in _half_block_sizes(c):
        same = (t // (2 * hb)) == (s // (2 * hb))
        masks.append(same & ((t % (2 * hb)) >= hb) & ((s % (2 * hb)) < hb))
    return masks


def _hgrn_chunk(q, k, v, bc, level_masks, s_state):
    nh, c, dh = q.shape
    diag = lax.broadcasted_iota(jnp.int32, (c, c), 0) == lax.broadcasted_iota(jnp.int32, (c, c), 1)
    o = _heads(_dot, q * jnp.exp(bc), s_state)
    yield

    a = jnp.where(diag, jnp.sum(q * k, axis=2, keepdims=True), 0.0)
    for hb, mask in zip(_half_block_sizes(c), level_masks):
        e = jnp.exp(-jnp.abs(bc - _boundary_rows(bc, hb)))
        a = a + jnp.where(mask, _heads(_dot_nt, q * e, k * e), 0.0)
        yield
    o = o + _heads(_dot, a, v)
    yield

    b_last = bc[:, c - 1:c, :]
    decay_col = _heads(lambda r: jnp.broadcast_to(r, (SUBLANES, dh)).T[:, 0:1], jnp.exp(b_last))
    s_new = decay_col * s_state + _heads(_dot_tn, k * jnp.exp(b_last - bc), v)
    return o, s_new


def _mixers_kernel(*refs, layer, chunk, n_valid, bb, has_state):
    if has_state:
        (big_ref, small_ref, bias_ref, alog_ref, mlnw_ref, convw_ref, gdnw_ref, lbl_ref, hgnw_ref,
         c0_ref, n0_ref, m0_ref, sg0_ref, cv0_ref, sh0_ref,
         ys_ref, c_ref, n_ref, m_ref, sg_ref, cv_ref, sh_ref, convbuf) = refs
    else:
        (big_ref, small_ref, bias_ref, alog_ref, mlnw_ref, convw_ref, gdnw_ref, lbl_ref, hgnw_ref,
         ys_ref, c_ref, n_ref, m_ref, sg_ref, cv_ref, sh_ref, convbuf) = refs
    j = pl.program_id(1)
    nj = pl.num_programs(1)
    c = chunk
    flat = len(big_ref.shape) == 2
    w = MIX_WIDTH
    dh = HEAD_DIM
    nh = N_HEADS
    tail = CONV_W - 1

    @pl.when(j == 0)
    def _():
        if has_state:
            c_ref[...] = c0_ref[...]
            n_ref[...] = n0_ref[...]
            m_ref[...] = m0_ref[...]
            sg_ref[...] = sg0_ref[...]
            sh_ref[...] = sh0_ref[...]
            convbuf[:, SUBLANES - tail:SUBLANES, :] = cv0_ref[...]
        else:
            c_ref[...] = jnp.zeros_like(c_ref)
            n_ref[...] = jnp.zeros_like(n_ref)
            m_ref[...] = jnp.zeros_like(m_ref)
            sg_ref[...] = jnp.zeros_like(sg_ref)
            sh_ref[...] = jnp.zeros_like(sh_ref)
            convbuf[:, SUBLANES - tail:SUBLANES, :] = jnp.zeros((bb, tail, N_BRANCH * w), F32)

    def load(ref, lo, hi):
        if not flat:
            return ref[:, :, lo:hi]
        x = ref[:, lo:hi]
        zeros = jnp.zeros((c - n_valid, hi - lo), x.dtype)
        return jnp.stack([jnp.concatenate([x[b * n_valid:(b + 1) * n_valid], zeros], axis=0) for b in range(bb)])

    def store(col0, y):
        if not flat:
            ys_ref[:, :, col0:col0 + y.shape[2]] = y.astype(ys_ref.dtype)
        else:
            rows_out = jnp.concatenate([y[b, :n_valid] for b in range(bb)], axis=0)
            ys_ref[:, col0:col0 + y.shape[2]] = rows_out.astype(ys_ref.dtype)

    def seg(i):
        return load(big_ref, i * w, (i + 1) * w)

    def per_seq(fn, x):
        return jnp.stack([fn(x[b]) for b in range(bb)])

    def head_major(x):
        return jnp.stack([x[b][:, h * dh:(h + 1) * dh] for b in range(bb) for h in range(nh)])

    def token_major(x):
        return jnp.stack([jnp.concatenate([x[b * nh + h] for h in range(nh)], axis=1) for b in range(bb)])

    def cols(x, first):
        return jnp.stack([x[b][:, first + h:first + h + 1] for b in range(bb) for h in range(nh)])

    def rows(x, first):
        return jnp.stack([x[b][first + h:first + h + 1, :] for b in range(bb) for h in range(nh)])

    lane = lax.broadcasted_iota(jnp.int32, (1, c, SMALL_COLS), 2)
    rowi = lax.broadcasted_iota(jnp.int32, (1, c, SMALL_COLS), 1)
    pre = load(small_ref, 0, SMALL_COLS) + bias_ref[...]
    is_f = (lane >= SM_F) & (lane < SM_B)
    is_b = (lane >= SM_B) & (lane < SM_A)
    is_a = (lane >= SM_A) & (lane < SM_A + nh)
    logg = -jnp.exp(alog_ref[...]) * _softplus(pre)
    gates = jnp.where(is_f, _log_sigmoid(pre), jnp.where(is_b, _sigmoid(pre), jnp.where(is_a, logg, pre)))
    gates = jnp.where(rowi < n_valid, gates, jnp.where(lane < SM_F, NEG, 0.0))
    tt = lax.broadcasted_iota(jnp.int32, (c, c), 0)
    ss = lax.broadcasted_iota(jnp.int32, (c, c), 1)
    incl = tt >= ss
    strict = tt > ss
    ltri = incl.astype(F32)
    utri = (ss >= tt).astype(F32)
    cum_src = jnp.where(is_f | is_a, gates, 0.0)
    cums = per_seq(lambda x: _dot_exact_lhs(ltri, x), cum_src)
    gates_t = per_seq(lambda x: x.T, gates)
    cums_t = per_seq(lambda x: _dot_exact_rhs(x.T, utri), cum_src)

    lbl = lbl_ref[...]
    sm = jnp.exp(lbl - jnp.max(lbl, axis=0, keepdims=True))
    sm = sm / jnp.sum(sm, axis=0, keepdims=True)
    lb = jnp.zeros((1, w), F32)
    for l in range(1, layer + 1):
        lb = lb + sm[l:l + 1, :]
    valid = lax.broadcasted_iota(jnp.int32, (1, c, w), 1) < n_valid
    hf = seg(SEG_HG_F)
    g_f = lb + (1.0 - lb) * _sigmoid(hf)
    k_hg = jnp.where(valid, (1.0 - lb) * _sigmoid(-hf), 0.0)
    f_log = jnp.where(valid, jnp.log(g_f), 0.0)
    bc = per_seq(lambda x: _dot_exact_lhs(ltri, x), f_log)

    convbuf[:, SUBLANES:SUBLANES + c, :] = load(big_ref, SEG_GD_Q * w, (SEG_GD_V + 1) * w)
    conv = convw_ref[CONV_W - 1:CONV_W, :] * convbuf[:, SUBLANES:SUBLANES + c, :]
    for jj in range(CONV_W - 1):
        off = SUBLANES - tail + jj
        conv = conv + convw_ref[jj:jj + 1, :] * convbuf[:, off:off + c, :]
    conv = _silu(conv)
    new_tail = convbuf[:, SUBLANES + n_valid - tail:SUBLANES + n_valid, :]
    convbuf[:, SUBLANES - tail:SUBLANES, :] = new_tail
    cq = head_major(conv[:, :, 0:w])
    ck = head_major(conv[:, :, w:2 * w])
    cv = head_major(conv[:, :, 2 * w:3 * w])
    qn = cq * lax.rsqrt(jnp.sum(cq * cq, axis=2, keepdims=True) + EPS) * (dh ** -0.5)
    kn = ck * lax.rsqrt(jnp.sum(ck * ck, axis=2, keepdims=True) + EPS)
    qk_kk = _heads(_dot_nt, jnp.concatenate([qn, kn], axis=1), kn)

    ml_q = head_major(seg(SEG_ML_Q))
    ml_k = head_major(seg(SEG_ML_K)) * (dh ** -0.5)
    c_old = c_ref[...].reshape(bb * nh, dh, dh)
    ml_qk = _heads(_dot_nt, ml_q, ml_k)
    ml_qc = _heads(_dot, ml_q, c_old)
    m_old = jnp.stack([m_ref[b][:, h:h + 1] for b in range(bb) for h in range(nh)])

    mlstm = _mlstm_chunk(
        ml_q, ml_k, head_major(seg(SEG_ML_V)), ml_qk, ml_qc,
        cols(gates, SM_I), cols(cums, SM_F), rows(gates_t, SM_I), rows(cums_t, SM_F), incl,
        c_old, n_ref[...].reshape(bb * nh, 1, dh), m_old)
    level_masks = _level_masks(c)
    gdn = _gdn_chunk(qn, kn, cv, qk_kk, cols(gates, SM_B), cols(cums, SM_A), rows(cums_t, SM_A),
                     incl, strict, level_masks, sg_ref[...].reshape(bb * nh, dh, dh))
    hgrn = _hgrn_chunk(head_major(_silu(seg(SEG_HG_Q))), head_major(k_hg), head_major(seg(SEG_HG_I)),
                       head_major(bc), level_masks, sh_ref[...].reshape(bb * nh, dh, dh))
    (hh, c_new, n_new, m_new), (o_gd, sg_new), (o_hg, sh_new) = _interleave(
        (mlstm, gdn, hgrn), STAGES_PER_ROUND)

    c_ref[...] = c_new.reshape(c_ref.shape)
    n_ref[...] = n_new.reshape(n_ref.shape)
    m_lane = lax.broadcasted_iota(jnp.int32, (1, SMALL_COLS), 1)
    m_rows = []
    for b in range(bb):
        m_row = jnp.zeros((1, SMALL_COLS), F32)
        for h in range(nh):
            m_row = jnp.where(m_lane == h, m_new[b * nh + h], m_row)
        m_rows.append(m_row)
    m_ref[...] = jnp.stack(m_rows)
    y_ml = token_major(_rms(hh)) * mlnw_ref[...] * _sigmoid(seg(SEG_ML_O))
    store(0, y_ml)

    sg_ref[...] = sg_new.reshape(sg_ref.shape)
    y_gd = token_major(_rms(o_gd) * gdnw_ref[...]) * _silu(seg(SEG_GD_Z))
    store(w, y_gd)

    sh_ref[...] = sh_new.reshape(sh_ref.shape)
    y_hg = _rms(token_major(o_hg)) * hgnw_ref[...] * _silu(seg(SEG_HG_G))
    store(2 * w, y_hg)

    @pl.when(j == nj - 1)
    def _():
        cv_ref[...] = convbuf[:, SUBLANES - tail:SUBLANES, :]


def _mixers(big2, small2, params, states_in, states_prev, *, b, t, layer, depth, chunk, bb):
    rows = min(t, chunk)
    assert t % rows == 0 and chunk % BF16_ROWS == 0 and CONV_W - 1 <= rows and b % bb == 0
    assert big2.shape[0] == b * t
    w, dh, nh = MIX_WIDTH, HEAD_DIM, N_HEADS
    has_state = states_in is not None
    bias_row, alog_row, ml_norm_w, conv_w, gd_norm_w, lb_logits, hg_norm_w = params

    def const(shape):
        return pl.BlockSpec(shape, lambda i, j: (0,) * len(shape))

    def st(shape):
        return pl.BlockSpec((None, bb) + shape, lambda i, j: (layer, i) + (0,) * len(shape))

    flat = rows < chunk

    def view_shape(l):
        return (b * t, l) if flat else (b, t, l)

    def view(x):
        return x.reshape(view_shape(x.shape[-1]))

    def rows_spec(l):
        if flat:
            return pl.BlockSpec((bb * rows, l), lambda i, j: (i, 0))
        return pl.BlockSpec((bb, rows, l), lambda i, j: (i, j, 0))

    state_shapes = [(nh, dh, dh), (nh, 1, dh), (1, SMALL_COLS), (nh, dh, dh), (CONV_W - 1, N_BRANCH * w), (nh, dh, dh)]
    in_specs = [
        rows_spec(N_MIX_SEGS * w), rows_spec(SMALL_COLS),
        const((1, SMALL_COLS)), const((1, SMALL_COLS)), const((1, w)), const((CONV_W, N_BRANCH * w)),
        const((1, dh)), const((depth, w)), const((1, w)),
    ]
    args = [view(big2), view(small2), bias_row, alog_row, ml_norm_w, conv_w, gd_norm_w, lb_logits, hg_norm_w]
    if has_state:
        in_specs += [st(s) for s in state_shapes]
        args += list(states_in)
    aliases = {}
    if states_prev is not None:
        for k_out, arr in enumerate(states_prev):
            aliases[len(args)] = 1 + k_out
            in_specs.append(pl.BlockSpec(memory_space=pl.ANY))
            args.append(arr)
    out_specs = [rows_spec(N_BRANCH * w)] + [st(s) for s in state_shapes]
    out_shape = [jax.ShapeDtypeStruct(view_shape(N_BRANCH * w), BF16)] + [
        jax.ShapeDtypeStruct((depth, b) + s, F32) for s in state_shapes]

    def body(*refs):
        n_in = len(args) - (len(states_prev) if states_prev is not None else 0)
        kept = refs[:n_in] + refs[len(args):]
        _mixers_kernel(*kept, layer=layer, chunk=chunk, n_valid=rows, bb=bb, has_state=has_state)

    outs = pl.pallas_call(
        body,
        grid=(b // bb, t // rows),
        in_specs=in_specs,
        out_specs=out_specs,
        out_shape=out_shape,
        scratch_shapes=[pltpu.VMEM((bb, SUBLANES + chunk, N_BRANCH * w), F32)],
        input_output_aliases=aliases,
        compiler_params=pltpu.CompilerParams(
            dimension_semantics=("parallel", "arbitrary"), vmem_limit_bytes=VMEM_LIMIT_BYTES),
        name="mixers",
    )(*args)
    return outs[0].reshape(b * t, N_BRANCH * w), tuple(outs[1:])


def _merge_ffn_kernel(x_ref, ys_ref, gate_ref, wbr_ref, wout_ref, n2_ref, wup_ref, wdown_ref, fin_ref,
                      out_ref, *, last):
    w = MIX_WIDTH
    d = x_ref.shape[1]
    merged = jnp.zeros(x_ref.shape, F32)
    for n in range(N_BRANCH):
        br = jnp.dot(ys_ref[:, n * w:(n + 1) * w], wbr_ref[n], preferred_element_type=F32)
        merged = merged + _sigmoid(gate_ref[:, n * d:(n + 1) * d]) * br
    x1 = x_ref[...] + jnp.dot(merged.astype(BF16), wout_ref[...], preferred_element_type=F32)
    xn = (_rms(x1) * n2_ref[...]).astype(BF16)
    hid = jnp.square(jnp.maximum(jnp.dot(xn, wup_ref[...], preferred_element_type=F32), 0.0))
    x2 = x1 + jnp.dot(hid.astype(BF16), wdown_ref[...], preferred_element_type=F32)
    if last:
        x2 = _rms(x2) * fin_ref[...]
    out_ref[...] = x2


def _merge_ffn(x2d, ys2d, big2d, w_branch, w_out, norm2_w, w_up, w_down, final_w, *, layer, tm, last):
    m, d = x2d.shape
    w = MIX_WIDTH
    d_ff = w_up.shape[2]
    gate_block = N_BRANCH * d
    assert m % tm == 0 and (N_MIX_SEGS * w) % gate_block == 0
    gate_idx = (N_MIX_SEGS * w) // gate_block

    def per_layer(shape):
        return pl.BlockSpec((None,) + shape, lambda i: (layer,) + (0,) * len(shape), pipeline_mode=pl.Buffered(1))

    return pl.pallas_call(
        functools.partial(_merge_ffn_kernel, last=last),
        grid=(m // tm,),
        in_specs=[
            pl.BlockSpec((tm, d), lambda i: (i, 0)),
            pl.BlockSpec((tm, N_BRANCH * w), lambda i: (i, 0)),
            pl.BlockSpec((tm, gate_block), lambda i: (i, gate_idx)),
            per_layer((N_BRANCH, w, d)), per_layer((d, d)), per_layer((1, d)), per_layer((d, d_ff)),
            per_layer((d_ff, d)),
            pl.BlockSpec((1, d), lambda i: (0, 0), pipeline_mode=pl.Buffered(1)),
        ],
        out_specs=pl.BlockSpec((tm, d), lambda i: (i, 0)),
        out_shape=jax.ShapeDtypeStruct((m, d), F32),
        compiler_params=pltpu.CompilerParams(
            dimension_semantics=("parallel",), vmem_limit_bytes=VMEM_LIMIT_BYTES),
        name="merge_ffn",
    )(x2d, ys2d, big2d, w_branch, w_out, norm2_w, w_up, w_down, final_w)


def _split_w_in(w_in):
    w, nh = MIX_WIDTH, N_HEADS
    depth, d, n_in = w_in.shape
    a0, a1 = 4 * w, 4 * w + 2 * nh
    b0, b1 = a1 + 4 * w, a1 + 4 * w + 2 * nh
    bounds = ((0, a0), (a1, b0), (b1, n_in))
    rows = math.gcd(d, 256)

    def body(w_ref, *out_refs):
        *big_refs, small_ref = out_refs
        for (lo, hi), ref in zip(bounds, big_refs):
            ref[...] = w_ref[:, lo:hi].astype(BF16)
        small = jnp.concatenate(
            [w_ref[:, a0:a1], w_ref[:, b0:b1], jnp.zeros((rows, SMALL_COLS - 4 * nh), F32)], axis=1)
        small_ref[...] = small.astype(BF16)

    widths = [hi - lo for lo, hi in bounds] + [SMALL_COLS]
    outs = pl.pallas_call(
        body,
        grid=(depth, d // rows),
        in_specs=[pl.BlockSpec((None, rows, n_in), lambda l, i: (l, i, 0))],
        out_specs=[pl.BlockSpec((None, rows, n), lambda l, i: (l, i, 0)) for n in widths],
        out_shape=[jax.ShapeDtypeStruct((depth, d, n), BF16) for n in widths],
        compiler_params=pltpu.CompilerParams(
            dimension_semantics=("parallel", "parallel"), vmem_limit_bytes=VMEM_LIMIT_BYTES),
        name="split_w_in",
    )(w_in)
    return tuple(outs[:-1]), outs[-1]


def _pad_lanes(parts, total):
    row = jnp.concatenate([p.reshape(1, -1).astype(F32) for p in parts], axis=1)
    return jnp.pad(row, ((0, 0), (0, total - row.shape[1])))


def _tiles(m):
    return math.gcd(m, 256), math.gcd(m, 256)


def _run_group(x3, states_in, weights, mix_params, seqs_per_step):
    b, t, d = x3.shape
    depth = len(mix_params)
    chunk = CHUNK if t % CHUNK == 0 else BF16_ROWS
    assert t % CHUNK == 0 or t <= BF16_ROWS
    x2d = x3.reshape(b * t, d)
    tm, tm_merge = _tiles(b * t)
    states_prev = None
    for l in range(depth):
        big, small = _in_proj(x2d, weights["norm1_w"], weights["w_big"], weights["w_small"], layer=l, tm=tm)
        ys, states_prev = _mixers(
            big, small, mix_params[l], states_in, states_prev,
            b=b, t=t, layer=l, depth=depth, chunk=chunk, bb=math.gcd(b, seqs_per_step))
        x2d = _merge_ffn(x2d, ys, big, weights["w_branch"], weights["w_out"],
                         weights["norm2_w"], weights["w_up"], weights["w_down"], weights["final_norm_w"],
                         layer=l, tm=tm_merge, last=(l == depth - 1))
    return x2d.reshape(b, t, d), states_prev


def kernel(x_prompt, x_sample, state_mlstm_C, state_mlstm_n, state_mlstm_m, state_gdn_S, state_gdn_conv, state_hgrn_S, norm1_w, w_in, ml_i_bias, ml_f_bias, ml_norm_w, gd_conv_w, gd_A_log, gd_dt_bias, gd_norm_w, hg_lb_logits, hg_norm_w, w_branch, w_out, norm2_w, w_up, w_down, final_norm_w):
    depth, d = norm1_w.shape
    nh = N_HEADS
    w_big, w_small = _split_w_in(w_in)
    weights = dict(
        norm1_w=norm1_w.reshape(depth, 1, d), w_big=w_big, w_small=w_small,
        w_branch=w_branch.astype(BF16), w_out=w_out.astype(BF16), norm2_w=norm2_w.reshape(depth, 1, d),
        w_up=w_up.astype(BF16), w_down=w_down.astype(BF16), final_norm_w=final_norm_w.reshape(1, d))
    mix_params = []
    for l in range(depth):
        zeros_h = jnp.zeros((nh,), F32)
        bias_row = _pad_lanes([ml_i_bias[l], ml_f_bias[l], zeros_h, gd_dt_bias[l]], SMALL_COLS)
        alog_row = _pad_lanes([zeros_h, zeros_h, zeros_h, gd_A_log[l]], SMALL_COLS)
        mix_params.append((bias_row, alog_row, ml_norm_w[l].reshape(1, -1), gd_conv_w[l],
                           gd_norm_w[l].reshape(1, -1), hg_lb_logits, hg_norm_w[l].reshape(1, -1)))

    def unpack(states):
        s_c, s_n, s_m, s_g, s_cv, s_h = states
        return s_c, s_n.reshape(s_n.shape[:3] + s_n.shape[4:]), s_m[:, :, 0, :nh], s_g, s_cv, s_h

    y_prompt, p_states = _run_group(x_prompt, None, weights, mix_params, PROMPT_SEQS_PER_STEP)

    bs = x_sample.shape[0]
    m_in = jnp.pad(state_mlstm_m, ((0, 0), (0, 0), (0, SMALL_COLS - nh))).reshape(depth, bs, 1, SMALL_COLS)
    n_in = state_mlstm_n.reshape(depth, bs, nh, 1, HEAD_DIM)
    s_in = (state_mlstm_C, n_in, m_in, state_gdn_S, state_gdn_conv, state_hgrn_S)
    y_sample, s_states = _run_group(x_sample, s_in, weights, mix_params, SAMPLE_SEQS_PER_STEP)

    return (y_prompt, y_sample) + unpack(p_states) + unpack(s_states)
```

```python
import functools
import math

import jax
import jax.numpy as jnp
from jax import lax
from jax.experimental import pallas as pl
from jax.experimental.pallas import tpu as pltpu

F32 = jnp.float32
BF16 = jnp.bfloat16

HEAD_DIM = 128
N_HEADS = 4
MIX_WIDTH = N_HEADS * HEAD_DIM
N_BRANCH = 3
CONV_W = 4
EPS = 1e-6
CHUNK = 128
SUBLANES = 8
BF16_ROWS = 16
SMALL_COLS = 128
NEG = -1e30
VMEM_LIMIT_BYTES = 56 * 1024 * 1024
PROJ_TN = 1024
PROMPT_SEQS_PER_STEP = 2
SAMPLE_SEQS_PER_STEP = 8
STAGES_PER_ROUND = (1, 4, 2)

SEG_ML_Q, SEG_ML_K, SEG_ML_V, SEG_ML_O = 0, 1, 2, 3
SEG_GD_Q, SEG_GD_K, SEG_GD_V, SEG_GD_Z = 4, 5, 6, 7
SEG_HG_Q, SEG_HG_F, SEG_HG_I, SEG_HG_G = 8, 9, 10, 11
N_MIX_SEGS = 12
SM_I, SM_F, SM_B, SM_A = 0, N_HEADS, 2 * N_HEADS, 3 * N_HEADS


def _dot(a, b):
    return jnp.dot(a.astype(BF16), b.astype(BF16), preferred_element_type=F32)


def _dot_nt(a, b):
    return lax.dot_general(a.astype(BF16), b.astype(BF16), (((1,), (1,)), ((), ())),
                           preferred_element_type=F32)


def _dot_tn(a, b):
    return _dot(a.T, b)


def _f32_dot(a, b):
    return jnp.dot(a, b, preferred_element_type=F32)


def _split3(a):
    hi = a.astype(BF16)
    r = a - hi.astype(F32)
    mid = r.astype(BF16)
    lo = (r - mid.astype(F32)).astype(BF16)
    return hi, mid, lo


def _dot_exact_lhs(a01, b):
    a = a01.astype(BF16)
    return sum(_f32_dot(a, part) for part in _split3(b))


def _dot_exact_rhs(a, b01):
    b = b01.astype(BF16)
    return sum(_f32_dot(part, b) for part in _split3(a))


def _sigmoid(x):
    return 1.0 / (1.0 + jnp.exp(-x))


def _silu(x):
    return x * _sigmoid(x)


def _softplus(x):
    return jnp.maximum(x, 0.0) + jnp.log(1.0 + jnp.exp(-jnp.abs(x)))


def _log_sigmoid(x):
    return -_softplus(-x)


def _rms(x, axis=-1):
    return x * lax.rsqrt(jnp.mean(x * x, axis=axis, keepdims=True) + EPS)


def _in_proj_kernel(x_ref, nw_ref, *refs):
    *wbig_refs, wsmall_ref, big_ref, small_ref = refs
    xn = (_rms(x_ref[...]) * nw_ref[...]).astype(BF16)
    small_ref[...] = jnp.dot(xn, wsmall_ref[...], preferred_element_type=F32)
    out0 = 0
    for wbig_ref in wbig_refs:
        for n0 in range(0, wbig_ref.shape[1], PROJ_TN):
            big_ref[:, out0 + n0:out0 + n0 + PROJ_TN] = jnp.dot(
                xn, wbig_ref[:, n0:n0 + PROJ_TN], preferred_element_type=F32)
        out0 += wbig_ref.shape[1]


def _in_proj(x2d, norm_w, w_big, w_small, *, layer, tm):
    m, d = x2d.shape
    n_big = sum(wg.shape[2] for wg in w_big)
    assert m % tm == 0 and all(wg.shape[2] % PROJ_TN == 0 for wg in w_big)
    resident = dict(pipeline_mode=pl.Buffered(1))
    return pl.pallas_call(
        _in_proj_kernel,
        grid=(m // tm,),
        in_specs=[
            pl.BlockSpec((tm, d), lambda i: (i, 0)),
            pl.BlockSpec((None, 1, d), lambda i: (layer, 0, 0), **resident),
            *[pl.BlockSpec((None, d, wg.shape[2]), lambda i: (layer, 0, 0), **resident) for wg in w_big],
            pl.BlockSpec((None, d, SMALL_COLS), lambda i: (layer, 0, 0), **resident),
        ],
        out_specs=[
            pl.BlockSpec((tm, n_big), lambda i: (i, 0)),
            pl.BlockSpec((tm, SMALL_COLS), lambda i: (i, 0)),
        ],
        out_shape=[
            jax.ShapeDtypeStruct((m, n_big), F32),
            jax.ShapeDtypeStruct((m, SMALL_COLS), F32),
        ],
        compiler_params=pltpu.CompilerParams(
            dimension_semantics=("parallel",), vmem_limit_bytes=VMEM_LIMIT_BYTES),
        name="in_proj",
    )(x2d, norm_w, *w_big, w_small)


def _heads(fn, *xs):
    return jnp.stack([fn(*(x[h] for x in xs)) for h in range(xs[0].shape[0])])


def _interleave(stage_generators, stages_per_round):
    results = [None] * len(stage_generators)
    live = list(enumerate(stage_generators))
    while live:
        still = []
        for idx, gen in live:
            try:
                for _ in range(stages_per_round[idx]):
                    next(gen)
                still.append((idx, gen))
            except StopIteration as stop:
                results[idx] = stop.value
        live = still
    return results


def _inv_unit_lower(n, level_masks):
    c = n.shape[-1]
    eye = (lax.broadcasted_iota(jnp.int32, (c, c), 0) == lax.broadcasted_iota(jnp.int32, (c, c), 1)).astype(F32)
    n_bf = n.astype(BF16)
    zero = jnp.zeros((), BF16)
    d = eye - jnp.where(level_masks[0], n, 0.0)
    for mask in level_masks[1:]:
        c_bf = jnp.where(mask, n_bf, zero)
        d_bf = d.astype(BF16)
        t = _heads(_f32_dot, d_bf, c_bf)
        yield
        d = d - _heads(_f32_dot, t.astype(BF16), d_bf)
        yield
    return d


def _mlstm_chunk(q, k, v, qk, qc, i_col, b_col, i_row, b_row, causal, c_state, n_state, m_state):
    c = q.shape[1]
    a_col = b_col + m_state
    d = jnp.where(causal, b_col - b_row + i_row, NEG)
    mt = jnp.maximum(a_col, jnp.max(d, axis=2, keepdims=True))
    p = qk * jnp.exp(d - mt)
    w_st = jnp.exp(a_col - mt)
    yield
    num = _heads(_dot, p, v) + w_st * qc
    yield
    den = jnp.sum(p, axis=2, keepdims=True) + w_st * jnp.sum(q * n_state, axis=2, keepdims=True)
    h = num / jnp.maximum(jnp.abs(den), jnp.exp(-mt))
    m_new = mt[:, c - 1:c, :]
    b_last = b_col[:, c - 1:c, :]
    ws_col = jnp.exp(b_last - b_col + i_col - m_new)
    wc = jnp.exp(b_last + m_state - m_new)
    kw = k * ws_col
    yield
    c_new = wc * c_state + _heads(_dot_tn, kw, v)
    n_new = wc * n_state + jnp.sum(kw, axis=1, keepdims=True)
    return h, c_new, n_new, m_new


def _gdn_chunk(q, k, v, qk_kk, beta_col, g_col, g_row, incl, strict, level_masks, s_state):
    c = q.shape[1]
    dh = v.shape[2]
    decay = jnp.exp(jnp.where(incl, g_col - g_row, NEG))
    qk = qk_kk[:, :c]
    kk = qk_kk[:, c:]
    n = jnp.where(strict, beta_col * kk * decay, 0.0)
    yield
    a_inv = yield from _inv_unit_lower(n, level_masks)
    eg_col = jnp.exp(g_col)
    rhs = jnp.concatenate([beta_col * v, (beta_col * eg_col) * k], axis=2)
    yield
    sol = _heads(_dot, a_inv, rhs)
    u = sol[:, :, :dh]
    w = sol[:, :, dh:]
    yield
    wq_s = _heads(_dot, jnp.concatenate([w, q * eg_col], axis=1), s_state)
    v_new = u - wq_s[:, :c]
    yield
    o = wq_s[:, c:] + _heads(_dot, qk * decay, v_new)
    g_last = g_col[:, c - 1:c, :]
    yield
    s_new = jnp.exp(g_last) * s_state + _heads(_dot_tn, k * jnp.exp(g_last - g_col), v_new)
    return o, s_new


def _row_block_bcast(x, block, row):
    nh, c, l = x.shape
    x3 = x.reshape(nh * c // block, block, l)
    return jnp.broadcast_to(x3[:, row:row + 1, :], x3.shape).reshape(nh, c, l)


def _boundary_rows(x, hb):
    if 2 * hb >= SUBLANES:
        return _row_block_bcast(x, 2 * hb, hb - 1)
    sub = lax.broadcasted_iota(jnp.int32, (1, x.shape[1], 1), 1) % SUBLANES
    out = _row_block_bcast(x, SUBLANES, hb - 1)
    for start in range(2 * hb, SUBLANES, 2 * hb):
        out = jnp.where(sub >= start, _row_block_bcast(x, SUBLANES, start + hb - 1), out)
    return out


def _half_block_sizes(c):
    assert c & (c - 1) == 0
    return [1 << e for e in range(c.bit_length() - 1)]


def _level_masks(c):
    t = lax.broadcasted_iota(jnp.int32, (c, c), 0)
    s = lax.broadcasted_iota(jnp.int32, (c, c), 1)
    masks = []
    for hb in _half_block_sizes(c):
        same = (t // (2 * hb)) == (s // (2 * hb))
        masks.append(same & ((t % (2 * hb)) >= hb) & ((s % (2 * hb)) < hb))
    return masks


def _hgrn_chunk(q, k, v, bc, level_masks, s_state):
    nh, c, dh = q.shape
    diag = lax.broadcasted_iota(jnp.int32, (c, c), 0) == lax.broadcasted_iota(jnp.int32, (c, c), 1)
    o = _heads(_dot, q * jnp.exp(bc), s_state)
    yield

    a = jnp.where(diag, jnp.sum(q * k, axis=2, keepdims=True), 0.0)
    for hb, mask in zip(_half_block_sizes(c), level_masks):
        e = jnp.exp(-jnp.abs(bc - _boundary_rows(bc, hb)))
        a = a + jnp.where(mask, _heads(_dot_nt, q * e, k * e), 0.0)
        yield
    o = o + _heads(_dot, a, v)
    yield

    b_last = bc[:, c - 1:c, :]
    decay_col = _heads(lambda r: jnp.broadcast_to(r, (SUBLANES, dh)).T[:, 0:1], jnp.exp(b_last))
    s_new = decay_col * s_state + _heads(_dot_tn, k * jnp.exp(b_last - bc), v)
    return o, s_new


def _mixers_kernel(*refs, layer, chunk, n_valid, bb, has_state, fused):
    gates_ref = None
    if fused:
        x_ref, nw_ref, wa_ref, wb_ref, wc_ref, wsmall_ref, *refs = refs
        *refs, gates_ref, convbuf_last = refs
        refs = [None, None, *refs, convbuf_last]
    if has_state:
        (big_ref, small_ref, bias_ref, alog_ref, mlnw_ref, convw_ref, gdnw_ref, lbl_ref, hgnw_ref,
         c0_ref, n0_ref, m0_ref, sg0_ref, cv0_ref, sh0_ref,
         ys_ref, c_ref, n_ref, m_ref, sg_ref, cv_ref, sh_ref, convbuf) = refs
    else:
        (big_ref, small_ref, bias_ref, alog_ref, mlnw_ref, convw_ref, gdnw_ref, lbl_ref, hgnw_ref,
         ys_ref, c_ref, n_ref, m_ref, sg_ref, cv_ref, sh_ref, convbuf) = refs
    j = pl.program_id(1)
    nj = pl.num_programs(1)
    c = chunk
    flat = not fused and len(big_ref.shape) == 2
    w = MIX_WIDTH
    dh = HEAD_DIM
    nh = N_HEADS
    tail = CONV_W - 1

    @pl.when(j == 0)
    def _():
        if has_state:
            c_ref[...] = c0_ref[...]
            n_ref[...] = n0_ref[...]
            m_ref[...] = m0_ref[...]
            sg_ref[...] = sg0_ref[...]
            sh_ref[...] = sh0_ref[...]
            convbuf[:, SUBLANES - tail:SUBLANES, :] = cv0_ref[...]
        else:
            c_ref[...] = jnp.zeros_like(c_ref)
            n_ref[...] = jnp.zeros_like(n_ref)
            m_ref[...] = jnp.zeros_like(m_ref)
            sg_ref[...] = jnp.zeros_like(sg_ref)
            sh_ref[...] = jnp.zeros_like(sh_ref)
            convbuf[:, SUBLANES - tail:SUBLANES, :] = jnp.zeros((bb, tail, N_BRANCH * w), F32)

    if fused:
        xn = (_rms(x_ref[...].reshape(bb * c, x_ref.shape[2])) * nw_ref[...]).astype(BF16)

        def project(w_ref, lo, hi):
            return jnp.dot(xn, w_ref[:, lo:hi], preferred_element_type=F32).reshape(bb, c, hi - lo)

        small_proj = project(wsmall_ref, 0, SMALL_COLS)
        big_proj = []
        for w_ref in (wa_ref, wb_ref, wc_ref):
            for n0 in range(0, 4 * w, PROJ_TN):
                cols = project(w_ref, n0, n0 + PROJ_TN)
                big_proj += [cols[:, :, k0:k0 + w] for k0 in range(0, PROJ_TN, w)]

        def gate_projection():
            n_gate = wc_ref.shape[1] - 4 * w
            for n0 in range(0, n_gate, PROJ_TN):
                gates_ref[:, :, n0:n0 + PROJ_TN] = project(wc_ref, 4 * w + n0, 4 * w + n0 + PROJ_TN)
                yield

    def load_rows(ref, lo, hi):
        if not flat:
            return ref[:, :, lo:hi]
        x = ref[:, lo:hi]
        zeros = jnp.zeros((c - n_valid, hi - lo), x.dtype)
        return jnp.stack([jnp.concatenate([x[b * n_valid:(b + 1) * n_valid], zeros], axis=0) for b in range(bb)])

    def load_wide(lo, hi):
        if fused:
            return jnp.concatenate(big_proj[lo // w:hi // w], axis=2) if hi - lo > w else big_proj[lo // w]
        return load_rows(big_ref, lo, hi)

    def store(col0, y):
        if not flat:
            ys_ref[:, :, col0:col0 + y.shape[2]] = y.astype(ys_ref.dtype)
        else:
            rows_out = jnp.concatenate([y[b, :n_valid] for b in range(bb)], axis=0)
            ys_ref[:, col0:col0 + y.shape[2]] = rows_out.astype(ys_ref.dtype)

    def seg(i):
        return load_wide(i * w, (i + 1) * w)

    def per_seq(fn, x):
        return jnp.stack([fn(x[b]) for b in range(bb)])

    def head_major(x):
        return jnp.stack([x[b][:, h * dh:(h + 1) * dh] for b in range(bb) for h in range(nh)])

    def token_major(x):
        return jnp.stack([jnp.concatenate([x[b * nh + h] for h in range(nh)], axis=1) for b in range(bb)])

    def cols(x, first):
        return jnp.stack([x[b][:, first + h:first + h + 1] for b in range(bb) for h in range(nh)])

    def rows(x, first):
        return jnp.stack([x[b][first + h:first + h + 1, :] for b in range(bb) for h in range(nh)])

    lane = lax.broadcasted_iota(jnp.int32, (1, c, SMALL_COLS), 2)
    rowi = lax.broadcasted_iota(jnp.int32, (1, c, SMALL_COLS), 1)
    pre = (small_proj if fused else load_rows(small_ref, 0, SMALL_COLS)) + bias_ref[...]
    is_f = (lane >= SM_F) & (lane < SM_B)
    is_b = (lane >= SM_B) & (lane < SM_A)
    is_a = (lane >= SM_A) & (lane < SM_A + nh)
    logg = -jnp.exp(alog_ref[...]) * _softplus(pre)
    gates = jnp.where(is_f, _log_sigmoid(pre), jnp.where(is_b, _sigmoid(pre), jnp.where(is_a, logg, pre)))
    gates = jnp.where(rowi < n_valid, gates, jnp.where(lane < SM_F, NEG, 0.0))
    tt = lax.broadcasted_iota(jnp.int32, (c, c), 0)
    ss = lax.broadcasted_iota(jnp.int32, (c, c), 1)
    incl = tt >= ss
    strict = tt > ss
    ltri = incl.astype(F32)
    utri = (ss >= tt).astype(F32)
    cum_src = jnp.where(is_f | is_a, gates, 0.0)
    cums = per_seq(lambda x: _dot_exact_lhs(ltri, x), cum_src)
    gates_t = per_seq(lambda x: x.T, gates)
    cums_t = per_seq(lambda x: _dot_exact_rhs(x.T, utri), cum_src)

    lbl = lbl_ref[...]
    sm = jnp.exp(lbl - jnp.max(lbl, axis=0, keepdims=True))
    sm = sm / jnp.sum(sm, axis=0, keepdims=True)
    lb = jnp.zeros((1, w), F32)
    for l in range(1, layer + 1):
        lb = lb + sm[l:l + 1, :]
    valid = lax.broadcasted_iota(jnp.int32, (1, c, w), 1) < n_valid
    hf = seg(SEG_HG_F)
    g_f = lb + (1.0 - lb) * _sigmoid(hf)
    k_hg = jnp.where(valid, (1.0 - lb) * _sigmoid(-hf), 0.0)
    f_log = jnp.where(valid, jnp.log(g_f), 0.0)
    bc = per_seq(lambda x: _dot_exact_lhs(ltri, x), f_log)

    convbuf[:, SUBLANES:SUBLANES + c, :] = load_wide(SEG_GD_Q * w, (SEG_GD_V + 1) * w)
    conv = convw_ref[CONV_W - 1:CONV_W, :] * convbuf[:, SUBLANES:SUBLANES + c, :]
    for jj in range(CONV_W - 1):
        off = SUBLANES - tail + jj
        conv = conv + convw_ref[jj:jj + 1, :] * convbuf[:, off:off + c, :]
    conv = _silu(conv)
    new_tail = convbuf[:, SUBLANES + n_valid - tail:SUBLANES + n_valid, :]
    convbuf[:, SUBLANES - tail:SUBLANES, :] = new_tail
    cq = head_major(conv[:, :, 0:w])
    ck = head_major(conv[:, :, w:2 * w])
    cv = head_major(conv[:, :, 2 * w:3 * w])
    qn = cq * lax.rsqrt(jnp.sum(cq * cq, axis=2, keepdims=True) + EPS) * (dh ** -0.5)
    kn = ck * lax.rsqrt(jnp.sum(ck * ck, axis=2, keepdims=True) + EPS)
    qk_kk = _heads(_dot_nt, jnp.concatenate([qn, kn], axis=1), kn)

    ml_q = head_major(seg(SEG_ML_Q))
    ml_k = head_major(seg(SEG_ML_K)) * (dh ** -0.5)
    c_old = c_ref[...].reshape(bb * nh, dh, dh)
    ml_qk = _heads(_dot_nt, ml_q, ml_k)
    ml_qc = _heads(_dot, ml_q, c_old)
    m_old = jnp.stack([m_ref[b][:, h:h + 1] for b in range(bb) for h in range(nh)])

    mlstm = _mlstm_chunk(
        ml_q, ml_k, head_major(seg(SEG_ML_V)), ml_qk, ml_qc,
        cols(gates, SM_I), cols(cums, SM_F), rows(gates_t, SM_I), rows(cums_t, SM_F), incl,
        c_old, n_ref[...].reshape(bb * nh, 1, dh), m_old)
    level_masks = _level_masks(c)
    gdn = _gdn_chunk(qn, kn, cv, qk_kk, cols(gates, SM_B), cols(cums, SM_A), rows(cums_t, SM_A),
                     incl, strict, level_masks, sg_ref[...].reshape(bb * nh, dh, dh))
    hgrn = _hgrn_chunk(head_major(_silu(seg(SEG_HG_Q))), head_major(k_hg), head_major(seg(SEG_HG_I)),
                       head_major(bc), level_masks, sh_ref[...].reshape(bb * nh, dh, dh))
    streams, per_round = (mlstm, gdn, hgrn), STAGES_PER_ROUND
    if fused:
        streams, per_round = streams + (gate_projection(),), per_round + (1,)
    (hh, c_new, n_new, m_new), (o_gd, sg_new), (o_hg, sh_new) = _interleave(streams, per_round)[:3]

    c_ref[...] = c_new.reshape(c_ref.shape)
    n_ref[...] = n_new.reshape(n_ref.shape)
    m_lane = lax.broadcasted_iota(jnp.int32, (1, SMALL_COLS), 1)
    m_rows = []
    for b in range(bb):
        m_row = jnp.zeros((1, SMALL_COLS), F32)
        for h in range(nh):
            m_row = jnp.where(m_lane == h, m_new[b * nh + h], m_row)
        m_rows.append(m_row)
    m_ref[...] = jnp.stack(m_rows)
    y_ml = token_major(_rms(hh)) * mlnw_ref[...] * _sigmoid(seg(SEG_ML_O))
    store(0, y_ml)

    sg_ref[...] = sg_new.reshape(sg_ref.shape)
    y_gd = token_major(_rms(o_gd) * gdnw_ref[...]) * _silu(seg(SEG_GD_Z))
    store(w, y_gd)

    sh_ref[...] = sh_new.reshape(sh_ref.shape)
    y_hg = _rms(token_major(o_hg)) * hgnw_ref[...] * _silu(seg(SEG_HG_G))
    store(2 * w, y_hg)

    @pl.when(j == nj - 1)
    def _():
        cv_ref[...] = convbuf[:, SUBLANES - tail:SUBLANES, :]


def _mixers(big2, small2, params, states_in, states_prev, proj_inputs, *, b, t, layer, depth, chunk, bb):
    rows = min(t, chunk)
    assert t % rows == 0 and chunk % BF16_ROWS == 0 and CONV_W - 1 <= rows and b % bb == 0
    fused = proj_inputs is not None
    assert fused or big2.shape[0] == b * t
    w, dh, nh = MIX_WIDTH, HEAD_DIM, N_HEADS
    has_state = states_in is not None
    bias_row, alog_row, ml_norm_w, conv_w, gd_norm_w, lb_logits, hg_norm_w = params

    def const(shape):
        return pl.BlockSpec(shape, lambda i, j: (0,) * len(shape))

    def st(shape):
        return pl.BlockSpec((None, bb) + shape, lambda i, j: (layer, i) + (0,) * len(shape))

    flat = rows < chunk

    def view_shape(l):
        return (b * t, l) if flat else (b, t, l)

    def view(x):
        return x.reshape(view_shape(x.shape[-1]))

    def rows_spec(l):
        if flat:
            return pl.BlockSpec((bb * rows, l), lambda i, j: (i, 0))
        return pl.BlockSpec((bb, rows, l), lambda i, j: (i, j, 0))

    state_shapes = [(nh, dh, dh), (nh, 1, dh), (1, SMALL_COLS), (nh, dh, dh), (CONV_W - 1, N_BRANCH * w), (nh, dh, dh)]
    if fused:
        assert not flat
        x2d, norm_w, w_big, w_small = proj_inputs
        d = x2d.shape[1]
        assert [wg.shape[2] for wg in w_big[:2]] == [4 * w, 4 * w] and w_big[2].shape[2] > 4 * w
        resident = dict(pipeline_mode=pl.Buffered(1))
        in_specs = [rows_spec(d), pl.BlockSpec((None, 1, d), lambda i, j: (layer, 0, 0), **resident)]
        in_specs += [pl.BlockSpec((None, d, wg.shape[2]), lambda i, j: (layer, 0, 0), **resident)
                     for wg in (*w_big, w_small)]
        args = [view(x2d), norm_w, *w_big, w_small]
        n_gate = w_big[2].shape[2] - 4 * w
    else:
        in_specs = [rows_spec(N_MIX_SEGS * w), rows_spec(SMALL_COLS)]
        args = [view(big2), view(small2)]
    in_specs += [
        const((1, SMALL_COLS)), const((1, SMALL_COLS)), const((1, w)), const((CONV_W, N_BRANCH * w)),
        const((1, dh)), const((depth, w)), const((1, w)),
    ]
    args += [bias_row, alog_row, ml_norm_w, conv_w, gd_norm_w, lb_logits, hg_norm_w]
    if has_state:
        in_specs += [st(s) for s in state_shapes]
        args += list(states_in)
    aliases = {}
    if states_prev is not None:
        for k_out, arr in enumerate(states_prev):
            aliases[len(args)] = 1 + k_out
            in_specs.append(pl.BlockSpec(memory_space=pl.ANY))
            args.append(arr)
    out_specs = [rows_spec(N_BRANCH * w)] + [st(s) for s in state_shapes]
    out_shape = [jax.ShapeDtypeStruct(view_shape(N_BRANCH * w), BF16)] + [
        jax.ShapeDtypeStruct((depth, b) + s, F32) for s in state_shapes]
    if fused:
        out_specs.append(rows_spec(n_gate))
        out_shape.append(jax.ShapeDtypeStruct(view_shape(n_gate), F32))

    def body(*refs):
        n_in = len(args) - (len(states_prev) if states_prev is not None else 0)
        kept = refs[:n_in] + refs[len(args):]
        _mixers_kernel(*kept, layer=layer, chunk=chunk, n_valid=rows, bb=bb, has_state=has_state, fused=fused)

    outs = pl.pallas_call(
        body,
        grid=(b // bb, t // rows),
        in_specs=in_specs,
        out_specs=out_specs,
        out_shape=out_shape,
        scratch_shapes=[pltpu.VMEM((bb, SUBLANES + chunk, N_BRANCH * w), F32)],
        input_output_aliases=aliases,
        compiler_params=pltpu.CompilerParams(
            dimension_semantics=("parallel", "arbitrary"), vmem_limit_bytes=VMEM_LIMIT_BYTES),
        name="mixers",
    )(*args)
    gates = outs[-1].reshape(b * t, -1) if fused else None
    return outs[0].reshape(b * t, N_BRANCH * w), tuple(outs[1:1 + len(state_shapes)]), gates


def _merge_ffn_kernel(x_ref, ys_ref, gate_ref, wbr_ref, wout_ref, n2_ref, wup_ref, wdown_ref, fin_ref,
                      out_ref, *, last):
    w = MIX_WIDTH
    d = x_ref.shape[1]
    merged = jnp.zeros(x_ref.shape, F32)
    for n in range(N_BRANCH):
        br = jnp.dot(ys_ref[:, n * w:(n + 1) * w], wbr_ref[n], preferred_element_type=F32)
        merged = merged + _sigmoid(gate_ref[:, n * d:(n + 1) * d]) * br
    x1 = x_ref[...] + jnp.dot(merged.astype(BF16), wout_ref[...], preferred_element_type=F32)
    xn = (_rms(x1) * n2_ref[...]).astype(BF16)
    hid = jnp.square(jnp.maximum(jnp.dot(xn, wup_ref[...], preferred_element_type=F32), 0.0))
    x2 = x1 + jnp.dot(hid.astype(BF16), wdown_ref[...], preferred_element_type=F32)
    if last:
        x2 = _rms(x2) * fin_ref[...]
    out_ref[...] = x2


def _merge_ffn(x2d, ys2d, gate_src, w_branch, w_out, norm2_w, w_up, w_down, final_w, *, layer, tm, last):
    m, d = x2d.shape
    w = MIX_WIDTH
    d_ff = w_up.shape[2]
    gate_block = N_BRANCH * d
    assert m % tm == 0 and gate_src.shape[1] % gate_block == 0
    gate_idx = gate_src.shape[1] // gate_block - 1

    def per_layer(shape):
        return pl.BlockSpec((None,) + shape, lambda i: (layer,) + (0,) * len(shape), pipeline_mode=pl.Buffered(1))

    return pl.pallas_call(
        functools.partial(_merge_ffn_kernel, last=last),
        grid=(m // tm,),
        in_specs=[
            pl.BlockSpec((tm, d), lambda i: (i, 0)),
            pl.BlockSpec((tm, N_BRANCH * w), lambda i: (i, 0)),
            pl.BlockSpec((tm, gate_block), lambda i: (i, gate_idx)),
            per_layer((N_BRANCH, w, d)), per_layer((d, d)), per_layer((1, d)), per_layer((d, d_ff)),
            per_layer((d_ff, d)),
            pl.BlockSpec((1, d), lambda i: (0, 0), pipeline_mode=pl.Buffered(1)),
        ],
        out_specs=pl.BlockSpec((tm, d), lambda i: (i, 0)),
        out_shape=jax.ShapeDtypeStruct((m, d), F32),
        compiler_params=pltpu.CompilerParams(
            dimension_semantics=("parallel",), vmem_limit_bytes=VMEM_LIMIT_BYTES),
        name="merge_ffn",
    )(x2d, ys2d, gate_src, w_branch, w_out, norm2_w, w_up, w_down, final_w)


def _split_w_in(w_in):
    w, nh = MIX_WIDTH, N_HEADS
    a0, a1 = 4 * w, 4 * w + 2 * nh
    b0, b1 = a1 + 4 * w, a1 + 4 * w + 2 * nh
    w_big = tuple(part.astype(BF16) for part in (w_in[:, :, :a0], w_in[:, :, a1:b0], w_in[:, :, b1:]))
    small = jnp.concatenate([w_in[:, :, a0:a1], w_in[:, :, b0:b1]], axis=2)
    w_small = jnp.pad(small, ((0, 0), (0, 0), (0, SMALL_COLS - 4 * nh))).astype(BF16)
    return w_big, w_small


def _pad_lanes(parts, total):
    row = jnp.concatenate([p.reshape(1, -1).astype(F32) for p in parts], axis=1)
    return jnp.pad(row, ((0, 0), (0, total - row.shape[1])))


def _tiles(m):
    return math.gcd(m, 256), math.gcd(m, 256)


def _run_group(x3, states_in, weights, mix_params, seqs_per_step):
    b, t, d = x3.shape
    depth = len(mix_params)
    chunk = CHUNK if t % CHUNK == 0 else BF16_ROWS
    assert t % CHUNK == 0 or t <= BF16_ROWS
    x2d = x3.reshape(b * t, d)
    tm, tm_merge = _tiles(b * t)
    states_prev = None
    fuse_projection = t % CHUNK == 0
    for l in range(depth):
        big = small = proj_inputs = None
        if fuse_projection:
            proj_inputs = (x2d, weights["norm1_w"], weights["w_big"], weights["w_small"])
        else:
            big, small = _in_proj(x2d, weights["norm1_w"], weights["w_big"], weights["w_small"], layer=l, tm=tm)
        ys, states_prev, gates = _mixers(
            big, small, mix_params[l], states_in, states_prev, proj_inputs,
            b=b, t=t, layer=l, depth=depth, chunk=chunk, bb=math.gcd(b, seqs_per_step))
        x2d = _merge_ffn(x2d, ys, gates if fuse_projection else big, weights["w_branch"], weights["w_out"],
                         weights["norm2_w"], weights["w_up"], weights["w_down"], weights["final_norm_w"],
                         layer=l, tm=tm_merge, last=(l == depth - 1))
    return x2d.reshape(b, t, d), states_prev


def kernel(x_prompt, x_sample, state_mlstm_C, state_mlstm_n, state_mlstm_m, state_gdn_S, state_gdn_conv, state_hgrn_S, norm1_w, w_in, ml_i_bias, ml_f_bias, ml_norm_w, gd_conv_w, gd_A_log, gd_dt_bias, gd_norm_w, hg_lb_logits, hg_norm_w, w_branch, w_out, norm2_w, w_up, w_down, final_norm_w):
    depth, d = norm1_w.shape
    nh = N_HEADS
    w_big, w_small = _split_w_in(w_in)
    weights = dict(
        norm1_w=norm1_w.reshape(depth, 1, d), w_big=w_big, w_small=w_small,
        w_branch=w_branch.astype(BF16), w_out=w_out.astype(BF16), norm2_w=norm2_w.reshape(depth, 1, d),
        w_up=w_up.astype(BF16), w_down=w_down.astype(BF16), final_norm_w=final_norm_w.reshape(1, d))
    mix_params = []
    for l in range(depth):
        zeros_h = jnp.zeros((nh,), F32)
        bias_row = _pad_lanes([ml_i_bias[l], ml_f_bias[l], zeros_h, gd_dt_bias[l]], SMALL_COLS)
        alog_row = _pad_lanes([zeros_h, zeros_h, zeros_h, gd_A_log[l]], SMALL_COLS)
        mix_params.append((bias_row, alog_row, ml_norm_w[l].reshape(1, -1), gd_conv_w[l],
                           gd_norm_w[l].reshape(1, -1), hg_lb_logits, hg_norm_w[l].reshape(1, -1)))

    def unpack(states):
        s_c, s_n, s_m, s_g, s_cv, s_h = states
        return s_c, s_n.reshape(s_n.shape[:3] + s_n.shape[4:]), s_m[:, :, 0, :nh], s_g, s_cv, s_h

    y_prompt, p_states = _run_group(x_prompt, None, weights, mix_params, PROMPT_SEQS_PER_STEP)

    bs = x_sample.shape[0]
    m_in = jnp.pad(state_mlstm_m, ((0, 0), (0, 0), (0, SMALL_COLS - nh))).reshape(depth, bs, 1, SMALL_COLS)
    n_in = state_mlstm_n.reshape(depth, bs, nh, 1, HEAD_DIM)
    s_in = (state_mlstm_C, n_in, m_in, state_gdn_S, state_gdn_conv, state_hgrn_S)
    y_sample, s_states = _run_group(x_sample, s_in, weights, mix_params, SAMPLE_SEQS_PER_STEP)

    return (y_prompt, y_sample) + unpack(p_states) + unpack(s_states)
```

```python
import functools
import math

import jax
import jax.numpy as jnp
from jax import lax
from jax.experimental import pallas as pl
from jax.experimental.pallas import tpu as pltpu

F32 = jnp.float32
BF16 = jnp.bfloat16

HEAD_DIM = 128
N_HEADS = 4
MIX_WIDTH = N_HEADS * HEAD_DIM
N_BRANCH = 3
CONV_W = 4
EPS = 1e-6
CHUNK = 128
SUBLANES = 8
BF16_ROWS = 16
SMALL_COLS = 128
NEG = -1e30
VMEM_LIMIT_BYTES = 56 * 1024 * 1024
PROJ_TN = 1024
PROMPT_SEQS_PER_STEP = 2
SAMPLE_SEQS_PER_STEP = 8
STAGES_PER_ROUND = (1, 4, 2)
GATE_STAGES_PER_ROUND = 3

SEG_ML_Q, SEG_ML_K, SEG_ML_V, SEG_ML_O = 0, 1, 2, 3
SEG_GD_Q, SEG_GD_K, SEG_GD_V, SEG_GD_Z = 4, 5, 6, 7
SEG_HG_Q, SEG_HG_F, SEG_HG_I, SEG_HG_G = 8, 9, 10, 11
N_MIX_SEGS = 12
SM_I, SM_F, SM_B, SM_A = 0, N_HEADS, 2 * N_HEADS, 3 * N_HEADS


def _dot(a, b):
    return jnp.dot(a.astype(BF16), b.astype(BF16), preferred_element_type=F32)


def _dot_nt(a, b):
    return lax.dot_general(a.astype(BF16), b.astype(BF16), (((1,), (1,)), ((), ())),
                           preferred_element_type=F32)


def _dot_tn(a, b):
    return _dot(a.T, b)


def _f32_dot(a, b):
    return jnp.dot(a, b, preferred_element_type=F32)


def _split3(a):
    hi = a.astype(BF16)
    r = a - hi.astype(F32)
    mid = r.astype(BF16)
    lo = (r - mid.astype(F32)).astype(BF16)
    return hi, mid, lo


def _dot_exact_lhs(a01, b):
    a = a01.astype(BF16)
    return sum(_f32_dot(a, part) for part in _split3(b))


def _dot_exact_rhs(a, b01):
    b = b01.astype(BF16)
    return sum(_f32_dot(part, b) for part in _split3(a))


def _sigmoid(x):
    return 1.0 / (1.0 + jnp.exp(-x))


def _silu(x):
    return x * _sigmoid(x)


def _softplus(x):
    return jnp.maximum(x, 0.0) + jnp.log(1.0 + jnp.exp(-jnp.abs(x)))


def _log_sigmoid(x):
    return -_softplus(-x)


def _rms(x, axis=-1):
    return x * lax.rsqrt(jnp.mean(x * x, axis=axis, keepdims=True) + EPS)


def _in_proj_kernel(x_ref, nw_ref, *refs):
    *wbig_refs, wsmall_ref, big_ref, small_ref = refs
    xn = (_rms(x_ref[...]) * nw_ref[...]).astype(BF16)
    small_ref[...] = jnp.dot(xn, wsmall_ref[...], preferred_element_type=F32)
    out0 = 0
    for wbig_ref in wbig_refs:
        for n0 in range(0, wbig_ref.shape[1], PROJ_TN):
            big_ref[:, out0 + n0:out0 + n0 + PROJ_TN] = jnp.dot(
                xn, wbig_ref[:, n0:n0 + PROJ_TN], preferred_element_type=F32)
        out0 += wbig_ref.shape[1]


def _in_proj(x2d, norm_w, w_big, w_small, *, layer, tm):
    m, d = x2d.shape
    n_big = sum(wg.shape[2] for wg in w_big)
    assert m % tm == 0 and all(wg.shape[2] % PROJ_TN == 0 for wg in w_big)
    resident = dict(pipeline_mode=pl.Buffered(1))
    return pl.pallas_call(
        _in_proj_kernel,
        grid=(m // tm,),
        in_specs=[
            pl.BlockSpec((tm, d), lambda i: (i, 0)),
            pl.BlockSpec((None, 1, d), lambda i: (layer, 0, 0), **resident),
            *[pl.BlockSpec((None, d, wg.shape[2]), lambda i: (layer, 0, 0), **resident) for wg in w_big],
            pl.BlockSpec((None, d, SMALL_COLS), lambda i: (layer, 0, 0), **resident),
        ],
        out_specs=[
            pl.BlockSpec((tm, n_big), lambda i: (i, 0)),
            pl.BlockSpec((tm, SMALL_COLS), lambda i: (i, 0)),
        ],
        out_shape=[
            jax.ShapeDtypeStruct((m, n_big), F32),
            jax.ShapeDtypeStruct((m, SMALL_COLS), F32),
        ],
        compiler_params=pltpu.CompilerParams(
            dimension_semantics=("parallel",), vmem_limit_bytes=VMEM_LIMIT_BYTES),
        name="in_proj",
    )(x2d, norm_w, *w_big, w_small)


def _heads(fn, *xs):
    return jnp.stack([fn(*(x[h] for x in xs)) for h in range(xs[0].shape[0])])


def _interleave(stage_generators, stages_per_round):
    results = [None] * len(stage_generators)
    live = list(enumerate(stage_generators))
    while live:
        still = []
        for idx, gen in live:
            try:
                for _ in range(stages_per_round[idx]):
                    next(gen)
                still.append((idx, gen))
            except StopIteration as stop:
                results[idx] = stop.value
        live = still
    return results


def _inv_unit_lower(n, level_masks):
    c = n.shape[-1]
    eye = (lax.broadcasted_iota(jnp.int32, (c, c), 0) == lax.broadcasted_iota(jnp.int32, (c, c), 1)).astype(F32)
    n_bf = n.astype(BF16)
    zero = jnp.zeros((), BF16)
    d = eye - jnp.where(level_masks[0], n, 0.0)
    for mask in level_masks[1:]:
        c_bf = jnp.where(mask, n_bf, zero)
        d_bf = d.astype(BF16)
        t = _heads(_f32_dot, d_bf, c_bf)
        yield
        d = d - _heads(_f32_dot, t.astype(BF16), d_bf)
        yield
    return d


def _mlstm_chunk(q, k, v, qk, qc, i_col, b_col, i_row, b_row, causal, c_state, n_state, m_state):
    c = q.shape[1]
    a_col = b_col + m_state
    d = jnp.where(causal, b_col - b_row + i_row, NEG)
    mt = jnp.maximum(a_col, jnp.max(d, axis=2, keepdims=True))
    p = qk * jnp.exp(d - mt)
    w_st = jnp.exp(a_col - mt)
    yield
    num = _heads(_dot, p, v) + w_st * qc
    yield
    den = jnp.sum(p, axis=2, keepdims=True) + w_st * jnp.sum(q * n_state, axis=2, keepdims=True)
    h = num / jnp.maximum(jnp.abs(den), jnp.exp(-mt))
    m_new = mt[:, c - 1:c, :]
    b_last = b_col[:, c - 1:c, :]
    ws_col = jnp.exp(b_last - b_col + i_col - m_new)
    wc = jnp.exp(b_last + m_state - m_new)
    kw = k * ws_col
    yield
    c_new = wc * c_state + _heads(_dot_tn, kw, v)
    n_new = wc * n_state + jnp.sum(kw, axis=1, keepdims=True)
    return h, c_new, n_new, m_new


def _gdn_chunk(q, k, v, qk_kk, beta_col, g_col, g_row, incl, strict, level_masks, s_state):
    c = q.shape[1]
    dh = v.shape[2]
    decay = jnp.exp(jnp.where(incl, g_col - g_row, NEG))
    qk = qk_kk[:, :c]
    kk = qk_kk[:, c:]
    n = jnp.where(strict, beta_col * kk * decay, 0.0)
    yield
    a_inv = yield from _inv_unit_lower(n, level_masks)
    eg_col = jnp.exp(g_col)
    rhs = jnp.concatenate([beta_col * v, (beta_col * eg_col) * k], axis=2)
    yield
    sol = _heads(_dot, a_inv, rhs)
    u = sol[:, :, :dh]
    w = sol[:, :, dh:]
    yield
    wq_s = _heads(_dot, jnp.concatenate([w, q * eg_col], axis=1), s_state)
    v_new = u - wq_s[:, :c]
    yield
    o = wq_s[:, c:] + _heads(_dot, qk * decay, v_new)
    g_last = g_col[:, c - 1:c, :]
    yield
    s_new = jnp.exp(g_last) * s_state + _heads(_dot_tn, k * jnp.exp(g_last - g_col), v_new)
    return o, s_new


def _row_block_bcast(x, block, row):
    nh, c, l = x.shape
    x3 = x.reshape(nh * c // block, block, l)
    return jnp.broadcast_to(x3[:, row:row + 1, :], x3.shape).reshape(nh, c, l)


def _boundary_rows(x, hb):
    if 2 * hb >= SUBLANES:
        return _row_block_bcast(x, 2 * hb, hb - 1)
    sub = lax.broadcasted_iota(jnp.int32, (1, x.shape[1], 1), 1) % SUBLANES
    out = _row_block_bcast(x, SUBLANES, hb - 1)
    for start in range(2 * hb, SUBLANES, 2 * hb):
        out = jnp.where(sub >= start, _row_block_bcast(x, SUBLANES, start + hb - 1), out)
    return out


def _half_block_sizes(c):
    assert c & (c - 1) == 0
    return [1 << e for e in range(c.bit_length() - 1)]


def _level_masks(c):
    t = lax.broadcasted_iota(jnp.int32, (c, c), 0)
    s = lax.broadcasted_iota(jnp.int32, (c, c), 1)
    masks = []
    for hb in _half_block_sizes(c):
        same = (t // (2 * hb)) == (s // (2 * hb))
        masks.append(same & ((t % (2 * hb)) >= hb) & ((s % (2 * hb)) < hb))
    return masks


def _hgrn_chunk(q, k, v, bc, level_masks, s_state):
    nh, c, dh = q.shape
    diag = lax.broadcasted_iota(jnp.int32, (c, c), 0) == lax.broadcasted_iota(jnp.int32, (c, c), 1)
    o = _heads(_dot, q * jnp.exp(bc), s_state)
    yield

    a = jnp.where(diag, jnp.sum(q * k, axis=2, keepdims=True), 0.0)
    for hb, mask in zip(_half_block_sizes(c), level_masks):
        e = jnp.exp(-jnp.abs(bc - _boundary_rows(bc, hb)))
        a = a + jnp.where(mask, _heads(_dot_nt, q * e, k * e), 0.0)
        yield
    o = o + _heads(_dot, a, v)
    yield

    b_last = bc[:, c - 1:c, :]
    decay_col = _heads(lambda r: jnp.broadcast_to(r, (SUBLANES, dh)).T[:, 0:1], jnp.exp(b_last))
    s_new = decay_col * s_state + _heads(_dot_tn, k * jnp.exp(b_last - bc), v)
    return o, s_new


def _mixers_kernel(*refs, layer, chunk, n_valid, bb, has_state, fused):
    gates_ref = None
    if fused:
        x_ref, nw_ref, wa_ref, wb_ref, wc_ref, wsmall_ref, *refs = refs
        *refs, gates_ref, convbuf_last = refs
        refs = [None, None, *refs, convbuf_last]
    if has_state:
        (big_ref, small_ref, bias_ref, alog_ref, mlnw_ref, convw_ref, gdnw_ref, lbl_ref, hgnw_ref,
         c0_ref, n0_ref, m0_ref, sg0_ref, cv0_ref, sh0_ref,
         ys_ref, c_ref, n_ref, m_ref, sg_ref, cv_ref, sh_ref, convbuf) = refs
    else:
        (big_ref, small_ref, bias_ref, alog_ref, mlnw_ref, convw_ref, gdnw_ref, lbl_ref, hgnw_ref,
         ys_ref, c_ref, n_ref, m_ref, sg_ref, cv_ref, sh_ref, convbuf) = refs
    j = pl.program_id(1)
    nj = pl.num_programs(1)
    c = chunk
    flat = not fused and len(big_ref.shape) == 2
    w = MIX_WIDTH
    dh = HEAD_DIM
    nh = N_HEADS
    tail = CONV_W - 1

    @pl.when(j == 0)
    def _():
        if has_state:
            c_ref[...] = c0_ref[...]
            n_ref[...] = n0_ref[...]
            m_ref[...] = m0_ref[...]
            sg_ref[...] = sg0_ref[...]
            sh_ref[...] = sh0_ref[...]
            convbuf[:, SUBLANES - tail:SUBLANES, :] = cv0_ref[...]
        else:
            c_ref[...] = jnp.zeros_like(c_ref)
            n_ref[...] = jnp.zeros_like(n_ref)
            m_ref[...] = jnp.zeros_like(m_ref)
            sg_ref[...] = jnp.zeros_like(sg_ref)
            sh_ref[...] = jnp.zeros_like(sh_ref)
            convbuf[:, SUBLANES - tail:SUBLANES, :] = jnp.zeros((bb, tail, N_BRANCH * w), F32)

    if fused:
        xn = (_rms(x_ref[...].reshape(bb * c, x_ref.shape[2])) * nw_ref[...]).astype(BF16)

        def project(w_ref, lo, hi):
            return jnp.dot(xn, w_ref[:, lo:hi], preferred_element_type=F32).reshape(bb, c, hi - lo)

        small_proj = project(wsmall_ref, 0, SMALL_COLS)
        big_proj = []
        for w_ref in (wa_ref, wb_ref, wc_ref):
            for n0 in range(0, 4 * w, PROJ_TN):
                cols = project(w_ref, n0, n0 + PROJ_TN)
                big_proj += [cols[:, :, k0:k0 + w] for k0 in range(0, PROJ_TN, w)]

        def gate_projection():
            n_gate = wc_ref.shape[1] - 4 * w
            for n0 in range(0, n_gate, PROJ_TN):
                gates_ref[:, :, n0:n0 + PROJ_TN] = project(wc_ref, 4 * w + n0, 4 * w + n0 + PROJ_TN)
                yield

    def load_rows(ref, lo, hi):
        if not flat:
            return ref[:, :, lo:hi]
        x = ref[:, lo:hi]
        zeros = jnp.zeros((c - n_valid, hi - lo), x.dtype)
        return jnp.stack([jnp.concatenate([x[b * n_valid:(b + 1) * n_valid], zeros], axis=0) for b in range(bb)])

    def load_wide(lo, hi):
        if fused:
            return jnp.concatenate(big_proj[lo // w:hi // w], axis=2) if hi - lo > w else big_proj[lo // w]
        return load_rows(big_ref, lo, hi)

    def store(col0, y):
        if not flat:
            ys_ref[:, :, col0:col0 + y.shape[2]] = y.astype(ys_ref.dtype)
        else:
            rows_out = jnp.concatenate([y[b, :n_valid] for b in range(bb)], axis=0)
            ys_ref[:, col0:col0 + y.shape[2]] = rows_out.astype(ys_ref.dtype)

    def seg(i):
        return load_wide(i * w, (i + 1) * w)

    def per_seq(fn, x):
        return jnp.stack([fn(x[b]) for b in range(bb)])

    def head_major(x):
        return jnp.stack([x[b][:, h * dh:(h + 1) * dh] for b in range(bb) for h in range(nh)])

    def token_major(x):
        return jnp.stack([jnp.concatenate([x[b * nh + h] for h in range(nh)], axis=1) for b in range(bb)])

    def cols(x, first):
        return jnp.stack([x[b][:, first + h:first + h + 1] for b in range(bb) for h in range(nh)])

    def rows(x, first):
        return jnp.stack([x[b][first + h:first + h + 1, :] for b in range(bb) for h in range(nh)])

    lane = lax.broadcasted_iota(jnp.int32, (1, c, SMALL_COLS), 2)
    rowi = lax.broadcasted_iota(jnp.int32, (1, c, SMALL_COLS), 1)
    pre = (small_proj if fused else load_rows(small_ref, 0, SMALL_COLS)) + bias_ref[...]
    is_f = (lane >= SM_F) & (lane < SM_B)
    is_b = (lane >= SM_B) & (lane < SM_A)
    is_a = (lane >= SM_A) & (lane < SM_A + nh)
    logg = -jnp.exp(alog_ref[...]) * _softplus(pre)
    gates = jnp.where(is_f, _log_sigmoid(pre), jnp.where(is_b, _sigmoid(pre), jnp.where(is_a, logg, pre)))
    gates = jnp.where(rowi < n_valid, gates, jnp.where(lane < SM_F, NEG, 0.0))
    tt = lax.broadcasted_iota(jnp.int32, (c, c), 0)
    ss = lax.broadcasted_iota(jnp.int32, (c, c), 1)
    incl = tt >= ss
    strict = tt > ss
    ltri = incl.astype(F32)
    utri = (ss >= tt).astype(F32)
    cum_src = jnp.where(is_f | is_a, gates, 0.0)
    cums = per_seq(lambda x: _dot_exact_lhs(ltri, x), cum_src)
    gates_t = per_seq(lambda x: x.T, gates)
    cums_t = per_seq(lambda x: _dot_exact_rhs(x.T, utri), cum_src)

    lbl = lbl_ref[...]
    sm = jnp.exp(lbl - jnp.max(lbl, axis=0, keepdims=True))
    sm = sm / jnp.sum(sm, axis=0, keepdims=True)
    lb = jnp.zeros((1, w), F32)
    for l in range(1, layer + 1):
        lb = lb + sm[l:l + 1, :]
    valid = lax.broadcasted_iota(jnp.int32, (1, c, w), 1) < n_valid
    hf = seg(SEG_HG_F)
    g_f = lb + (1.0 - lb) * _sigmoid(hf)
    k_hg = jnp.where(valid, (1.0 - lb) * _sigmoid(-hf), 0.0)
    f_log = jnp.where(valid, jnp.log(g_f), 0.0)
    bc = per_seq(lambda x: _dot_exact_lhs(ltri, x), f_log)

    convbuf[:, SUBLANES:SUBLANES + c, :] = load_wide(SEG_GD_Q * w, (SEG_GD_V + 1) * w)
    conv = convw_ref[CONV_W - 1:CONV_W, :] * convbuf[:, SUBLANES:SUBLANES + c, :]
    for jj in range(CONV_W - 1):
        off = SUBLANES - tail + jj
        conv = conv + convw_ref[jj:jj + 1, :] * convbuf[:, off:off + c, :]
    conv = _silu(conv)
    new_tail = convbuf[:, SUBLANES + n_valid - tail:SUBLANES + n_valid, :]
    convbuf[:, SUBLANES - tail:SUBLANES, :] = new_tail
    cq = head_major(conv[:, :, 0:w])
    ck = head_major(conv[:, :, w:2 * w])
    cv = head_major(conv[:, :, 2 * w:3 * w])
    qn = cq * lax.rsqrt(jnp.sum(cq * cq, axis=2, keepdims=True) + EPS) * (dh ** -0.5)
    kn = ck * lax.rsqrt(jnp.sum(ck * ck, axis=2, keepdims=True) + EPS)
    qk_kk = _heads(_dot_nt, jnp.concatenate([qn, kn], axis=1), kn)

    ml_q = head_major(seg(SEG_ML_Q))
    ml_k = head_major(seg(SEG_ML_K)) * (dh ** -0.5)
    c_old = c_ref[...].reshape(bb * nh, dh, dh)
    ml_qk = _heads(_dot_nt, ml_q, ml_k)
    ml_qc = _heads(_dot, ml_q, c_old)
    m_old = jnp.stack([m_ref[b][:, h:h + 1] for b in range(bb) for h in range(nh)])

    mlstm = _mlstm_chunk(
        ml_q, ml_k, head_major(seg(SEG_ML_V)), ml_qk, ml_qc,
        cols(gates, SM_I), cols(cums, SM_F), rows(gates_t, SM_I), rows(cums_t, SM_F), incl,
        c_old, n_ref[...].reshape(bb * nh, 1, dh), m_old)
    level_masks = _level_masks(c)
    gdn = _gdn_chunk(qn, kn, cv, qk_kk, cols(gates, SM_B), cols(cums, SM_A), rows(cums_t, SM_A),
                     incl, strict, level_masks, sg_ref[...].reshape(bb * nh, dh, dh))
    hgrn = _hgrn_chunk(head_major(_silu(seg(SEG_HG_Q))), head_major(k_hg), head_major(seg(SEG_HG_I)),
                       head_major(bc), level_masks, sh_ref[...].reshape(bb * nh, dh, dh))
    streams, per_round = (mlstm, gdn, hgrn), STAGES_PER_ROUND
    if fused:
        streams, per_round = streams + (gate_projection(),), per_round + (GATE_STAGES_PER_ROUND,)
    (hh, c_new, n_new, m_new), (o_gd, sg_new), (o_hg, sh_new) = _interleave(streams, per_round)[:3]

    c_ref[...] = c_new.reshape(c_ref.shape)
    n_ref[...] = n_new.reshape(n_ref.shape)
    m_lane = lax.broadcasted_iota(jnp.int32, (1, SMALL_COLS), 1)
    m_rows = []
    for b in range(bb):
        m_row = jnp.zeros((1, SMALL_COLS), F32)
        for h in range(nh):
            m_row = jnp.where(m_lane == h, m_new[b * nh + h], m_row)
        m_rows.append(m_row)
    m_ref[...] = jnp.stack(m_rows)
    y_ml = token_major(_rms(hh)) * mlnw_ref[...] * _sigmoid(seg(SEG_ML_O))
    store(0, y_ml)

    sg_ref[...] = sg_new.reshape(sg_ref.shape)
    y_gd = token_major(_rms(o_gd) * gdnw_ref[...]) * _silu(seg(SEG_GD_Z))
    store(w, y_gd)

    sh_ref[...] = sh_new.reshape(sh_ref.shape)
    y_hg = _rms(token_major(o_hg)) * hgnw_ref[...] * _silu(seg(SEG_HG_G))
    store(2 * w, y_hg)

    @pl.when(j == nj - 1)
    def _():
        cv_ref[...] = convbuf[:, SUBLANES - tail:SUBLANES, :]


def _mixers(big2, small2, params, states_in, states_prev, proj_inputs, *, b, t, layer, depth, chunk, bb):
    rows = min(t, chunk)
    assert t % rows == 0 and chunk % BF16_ROWS == 0 and CONV_W - 1 <= rows and b % bb == 0
    fused = proj_inputs is not None
    assert fused or big2.shape[0] == b * t
    w, dh, nh = MIX_WIDTH, HEAD_DIM, N_HEADS
    has_state = states_in is not None
    bias_row, alog_row, ml_norm_w, conv_w, gd_norm_w, lb_logits, hg_norm_w = params

    def const(shape):
        return pl.BlockSpec(shape, lambda i, j: (0,) * len(shape))

    def st(shape):
        return pl.BlockSpec((None, bb) + shape, lambda i, j: (layer, i) + (0,) * len(shape))

    flat = rows < chunk

    def view_shape(l):
        return (b * t, l) if flat else (b, t, l)

    def view(x):
        return x.reshape(view_shape(x.shape[-1]))

    def rows_spec(l):
        if flat:
            return pl.BlockSpec((bb * rows, l), lambda i, j: (i, 0))
        return pl.BlockSpec((bb, rows, l), lambda i, j: (i, j, 0))

    state_shapes = [(nh, dh, dh), (nh, 1, dh), (1, SMALL_COLS), (nh, dh, dh), (CONV_W - 1, N_BRANCH * w), (nh, dh, dh)]
    if fused:
        assert not flat
        x2d, norm_w, w_big, w_small = proj_inputs
        d = x2d.shape[1]
        assert [wg.shape[2] for wg in w_big[:2]] == [4 * w, 4 * w] and w_big[2].shape[2] > 4 * w
        resident = dict(pipeline_mode=pl.Buffered(1))
        in_specs = [rows_spec(d), pl.BlockSpec((None, 1, d), lambda i, j: (layer, 0, 0), **resident)]
        in_specs += [pl.BlockSpec((None, d, wg.shape[2]), lambda i, j: (layer, 0, 0), **resident)
                     for wg in (*w_big, w_small)]
        args = [view(x2d), norm_w, *w_big, w_small]
        n_gate = w_big[2].shape[2] - 4 * w
    else:
        in_specs = [rows_spec(N_MIX_SEGS * w), rows_spec(SMALL_COLS)]
        args = [view(big2), view(small2)]
    in_specs += [
        const((1, SMALL_COLS)), const((1, SMALL_COLS)), const((1, w)), const((CONV_W, N_BRANCH * w)),
        const((1, dh)), const((depth, w)), const((1, w)),
    ]
    args += [bias_row, alog_row, ml_norm_w, conv_w, gd_norm_w, lb_logits, hg_norm_w]
    if has_state:
        in_specs += [st(s) for s in state_shapes]
        args += list(states_in)
    aliases = {}
    if states_prev is not None:
        for k_out, arr in enumerate(states_prev):
            aliases[len(args)] = 1 + k_out
            in_specs.append(pl.BlockSpec(memory_space=pl.ANY))
            args.append(arr)
    out_specs = [rows_spec(N_BRANCH * w)] + [st(s) for s in state_shapes]
    out_shape = [jax.ShapeDtypeStruct(view_shape(N_BRANCH * w), BF16)] + [
        jax.ShapeDtypeStruct((depth, b) + s, F32) for s in state_shapes]
    if fused:
        out_specs.append(rows_spec(n_gate))
        out_shape.append(jax.ShapeDtypeStruct(view_shape(n_gate), F32))

    def body(*refs):
        n_in = len(args) - (len(states_prev) if states_prev is not None else 0)
        kept = refs[:n_in] + refs[len(args):]
        _mixers_kernel(*kept, layer=layer, chunk=chunk, n_valid=rows, bb=bb, has_state=has_state, fused=fused)

    outs = pl.pallas_call(
        body,
        grid=(b // bb, t // rows),
        in_specs=in_specs,
        out_specs=out_specs,
        out_shape=out_shape,
        scratch_shapes=[pltpu.VMEM((bb, SUBLANES + chunk, N_BRANCH * w), F32)],
        input_output_aliases=aliases,
        compiler_params=pltpu.CompilerParams(
            dimension_semantics=("parallel", "arbitrary"), vmem_limit_bytes=VMEM_LIMIT_BYTES),
        name="mixers",
    )(*args)
    gates = outs[-1].reshape(b * t, -1) if fused else None
    return outs[0].reshape(b * t, N_BRANCH * w), tuple(outs[1:1 + len(state_shapes)]), gates


def _merge_ffn_kernel(x_ref, ys_ref, gate_ref, wbr_ref, wout_ref, n2_ref, wup_ref, wdown_ref, fin_ref,
                      out_ref, *, last):
    w = MIX_WIDTH
    d = x_ref.shape[1]
    merged = jnp.zeros(x_ref.shape, F32)
    for n in range(N_BRANCH):
        br = jnp.dot(ys_ref[:, n * w:(n + 1) * w], wbr_ref[n], preferred_element_type=F32)
        merged = merged + _sigmoid(gate_ref[:, n * d:(n + 1) * d]) * br
    x1 = x_ref[...] + jnp.dot(merged.astype(BF16), wout_ref[...], preferred_element_type=F32)
    xn = (_rms(x1) * n2_ref[...]).astype(BF16)
    hid = jnp.square(jnp.maximum(jnp.dot(xn, wup_ref[...], preferred_element_type=F32), 0.0))
    x2 = x1 + jnp.dot(hid.astype(BF16), wdown_ref[...], preferred_element_type=F32)
    if last:
        x2 = _rms(x2) * fin_ref[...]
    out_ref[...] = x2


def _merge_ffn(x2d, ys2d, gate_src, w_branch, w_out, norm2_w, w_up, w_down, final_w, *, layer, tm, last):
    m, d = x2d.shape
    w = MIX_WIDTH
    d_ff = w_up.shape[2]
    gate_block = N_BRANCH * d
    assert m % tm == 0 and gate_src.shape[1] % gate_block == 0
    gate_idx = gate_src.shape[1] // gate_block - 1

    def per_layer(shape):
        return pl.BlockSpec((None,) + shape, lambda i: (layer,) + (0,) * len(shape), pipeline_mode=pl.Buffered(1))

    return pl.pallas_call(
        functools.partial(_merge_ffn_kernel, last=last),
        grid=(m // tm,),
        in_specs=[
            pl.BlockSpec((tm, d), lambda i: (i, 0)),
            pl.BlockSpec((tm, N_BRANCH * w), lambda i: (i, 0)),
            pl.BlockSpec((tm, gate_block), lambda i: (i, gate_idx)),
            per_layer((N_BRANCH, w, d)), per_layer((d, d)), per_layer((1, d)), per_layer((d, d_ff)),
            per_layer((d_ff, d)),
            pl.BlockSpec((1, d), lambda i: (0, 0), pipeline_mode=pl.Buffered(1)),
        ],
        out_specs=pl.BlockSpec((tm, d), lambda i: (i, 0)),
        out_shape=jax.ShapeDtypeStruct((m, d), F32),
        compiler_params=pltpu.CompilerParams(
            dimension_semantics=("parallel",), vmem_limit_bytes=VMEM_LIMIT_BYTES),
        name="merge_ffn",
    )(x2d, ys2d, gate_src, w_branch, w_out, norm2_w, w_up, w_down, final_w)


def _split_w_in(w_in):
    w, nh = MIX_WIDTH, N_HEADS
    a0, a1 = 4 * w, 4 * w + 2 * nh
    b0, b1 = a1 + 4 * w, a1 + 4 * w + 2 * nh
    w_big = tuple(part.astype(BF16) for part in (w_in[:, :, :a0], w_in[:, :, a1:b0], w_in[:, :, b1:]))
    small = jnp.concatenate([w_in[:, :, a0:a1], w_in[:, :, b0:b1]], axis=2)
    w_small = jnp.pad(small, ((0, 0), (0, 0), (0, SMALL_COLS - 4 * nh))).astype(BF16)
    return w_big, w_small


def _pad_lanes(parts, total):
    row = jnp.concatenate([p.reshape(1, -1).astype(F32) for p in parts], axis=1)
    return jnp.pad(row, ((0, 0), (0, total - row.shape[1])))


def _tiles(m):
    return math.gcd(m, 256), math.gcd(m, 256)


def _run_group(x3, states_in, weights, mix_params, seqs_per_step):
    b, t, d = x3.shape
    depth = len(mix_params)
    chunk = CHUNK if t % CHUNK == 0 else BF16_ROWS
    assert t % CHUNK == 0 or t <= BF16_ROWS
    x2d = x3.reshape(b * t, d)
    tm, tm_merge = _tiles(b * t)
    states_prev = None
    fuse_projection = t % CHUNK == 0
    for l in range(depth):
        big = small = proj_inputs = None
        if fuse_projection:
            proj_inputs = (x2d, weights["norm1_w"], weights["w_big"], weights["w_small"])
        else:
            big, small = _in_proj(x2d, weights["norm1_w"], weights["w_big"], weights["w_small"], layer=l, tm=tm)
        ys, states_prev, gates = _mixers(
            big, small, mix_params[l], states_in, states_prev, proj_inputs,
            b=b, t=t, layer=l, depth=depth, chunk=chunk, bb=math.gcd(b, seqs_per_step))
        x2d = _merge_ffn(x2d, ys, gates if fuse_projection else big, weights["w_branch"], weights["w_out"],
                         weights["norm2_w"], weights["w_up"], weights["w_down"], weights["final_norm_w"],
                         layer=l, tm=tm_merge, last=(l == depth - 1))
    return x2d.reshape(b, t, d), states_prev


def kernel(x_prompt, x_sample, state_mlstm_C, state_mlstm_n, state_mlstm_m, state_gdn_S, state_gdn_conv, state_hgrn_S, norm1_w, w_in, ml_i_bias, ml_f_bias, ml_norm_w, gd_conv_w, gd_A_log, gd_dt_bias, gd_norm_w, hg_lb_logits, hg_norm_w, w_branch, w_out, norm2_w, w_up, w_down, final_norm_w):
    depth, d = norm1_w.shape
    nh = N_HEADS
    w_big, w_small = _split_w_in(w_in)
    weights = dict(
        norm1_w=norm1_w.reshape(depth, 1, d), w_big=w_big, w_small=w_small,
        w_branch=w_branch.astype(BF16), w_out=w_out.astype(BF16), norm2_w=norm2_w.reshape(depth, 1, d),
        w_up=w_up.astype(BF16), w_down=w_down.astype(BF16), final_norm_w=final_norm_w.reshape(1, d))
    mix_params = []
    for l in range(depth):
        zeros_h = jnp.zeros((nh,), F32)
        bias_row = _pad_lanes([ml_i_bias[l], ml_f_bias[l], zeros_h, gd_dt_bias[l]], SMALL_COLS)
        alog_row = _pad_lanes([zeros_h, zeros_h, zeros_h, gd_A_log[l]], SMALL_COLS)
        mix_params.append((bias_row, alog_row, ml_norm_w[l].reshape(1, -1), gd_conv_w[l],
                           gd_norm_w[l].reshape(1, -1), hg_lb_logits, hg_norm_w[l].reshape(1, -1)))

    def unpack(states):
        s_c, s_n, s_m, s_g, s_cv, s_h = states
        return s_c, s_n.reshape(s_n.shape[:3] + s_n.shape[4:]), s_m[:, :, 0, :nh], s_g, s_cv, s_h

    y_prompt, p_states = _run_group(x_prompt, None, weights, mix_params, PROMPT_SEQS_PER_STEP)

    bs = x_sample.shape[0]
    m_in = jnp.pad(state_mlstm_m, ((0, 0), (0, 0), (0, SMALL_COLS - nh))).reshape(depth, bs, 1, SMALL_COLS)
    n_in = state_mlstm_n.reshape(depth, bs, nh, 1, HEAD_DIM)
    s_in = (state_mlstm_C, n_in, m_in, state_gdn_S, state_gdn_conv, state_hgrn_S)
    y_sample, s_states = _run_group(x_sample, s_in, weights, mix_params, SAMPLE_SEQS_PER_STEP)

    return (y_prompt, y_sample) + unpack(p_states) + unpack(s_states)
```

```python
import functools
import math

import jax
import jax.numpy as jnp
from jax import lax
from jax.experimental import pallas as pl
from jax.experimental.pallas import tpu as pltpu

F32 = jnp.float32
BF16 = jnp.bfloat16

HEAD_DIM = 128
N_HEADS = 4
MIX_WIDTH = N_HEADS * HEAD_DIM
N_BRANCH = 3
CONV_W = 4
EPS = 1e-6
CHUNK = 128
SUBLANES = 8
BF16_ROWS = 16
SMALL_COLS = 128
NEG = -1e30
VMEM_LIMIT_BYTES = 56 * 1024 * 1024
PROJ_TN = 1024
PROMPT_SEQS_PER_STEP = 2
SAMPLE_SEQS_PER_STEP = 8
STAGES_PER_ROUND = (1, 5, 2)
GATE_STAGES_PER_ROUND = 3

SEG_ML_Q, SEG_ML_K, SEG_ML_V, SEG_ML_O = 0, 1, 2, 3
SEG_GD_Q, SEG_GD_K, SEG_GD_V, SEG_GD_Z = 4, 5, 6, 7
SEG_HG_Q, SEG_HG_F, SEG_HG_I, SEG_HG_G = 8, 9, 10, 11
N_MIX_SEGS = 12
SM_I, SM_F, SM_B, SM_A = 0, N_HEADS, 2 * N_HEADS, 3 * N_HEADS


def _dot(a, b):
    return jnp.dot(a.astype(BF16), b.astype(BF16), preferred_element_type=F32)


def _dot_nt(a, b):
    return lax.dot_general(a.astype(BF16), b.astype(BF16), (((1,), (1,)), ((), ())),
                           preferred_element_type=F32)


def _dot_tn(a, b):
    return _dot(a.T, b)


def _f32_dot(a, b):
    return jnp.dot(a, b, preferred_element_type=F32)


def _split3(a):
    hi = a.astype(BF16)
    r = a - hi.astype(F32)
    mid = r.astype(BF16)
    lo = (r - mid.astype(F32)).astype(BF16)
    return hi, mid, lo


def _dot_exact_lhs(a01, b):
    a = a01.astype(BF16)
    return sum(_f32_dot(a, part) for part in _split3(b))


def _dot_exact_rhs(a, b01):
    b = b01.astype(BF16)
    return sum(_f32_dot(part, b) for part in _split3(a))


def _sigmoid(x):
    return 1.0 / (1.0 + jnp.exp(-x))


def _silu(x):
    return x * _sigmoid(x)


def _softplus(x):
    return jnp.maximum(x, 0.0) + jnp.log(1.0 + jnp.exp(-jnp.abs(x)))


def _log_sigmoid(x):
    return -_softplus(-x)


def _rms(x, axis=-1):
    return x * lax.rsqrt(jnp.mean(x * x, axis=axis, keepdims=True) + EPS)


def _in_proj_kernel(x_ref, nw_ref, *refs):
    *wbig_refs, wsmall_ref, big_ref, small_ref = refs
    xn = (_rms(x_ref[...]) * nw_ref[...]).astype(BF16)
    small_ref[...] = jnp.dot(xn, wsmall_ref[...], preferred_element_type=F32)
    out0 = 0
    for wbig_ref in wbig_refs:
        for n0 in range(0, wbig_ref.shape[1], PROJ_TN):
            big_ref[:, out0 + n0:out0 + n0 + PROJ_TN] = jnp.dot(
                xn, wbig_ref[:, n0:n0 + PROJ_TN], preferred_element_type=F32)
        out0 += wbig_ref.shape[1]


def _in_proj(x2d, norm_w, w_big, w_small, *, layer, tm):
    m, d = x2d.shape
    n_big = sum(wg.shape[2] for wg in w_big)
    assert m % tm == 0 and all(wg.shape[2] % PROJ_TN == 0 for wg in w_big)
    resident = dict(pipeline_mode=pl.Buffered(1))
    return pl.pallas_call(
        _in_proj_kernel,
        grid=(m // tm,),
        in_specs=[
            pl.BlockSpec((tm, d), lambda i: (i, 0)),
            pl.BlockSpec((None, 1, d), lambda i: (layer, 0, 0), **resident),
            *[pl.BlockSpec((None, d, wg.shape[2]), lambda i: (layer, 0, 0), **resident) for wg in w_big],
            pl.BlockSpec((None, d, SMALL_COLS), lambda i: (layer, 0, 0), **resident),
        ],
        out_specs=[
            pl.BlockSpec((tm, n_big), lambda i: (i, 0)),
            pl.BlockSpec((tm, SMALL_COLS), lambda i: (i, 0)),
        ],
        out_shape=[
            jax.ShapeDtypeStruct((m, n_big), F32),
            jax.ShapeDtypeStruct((m, SMALL_COLS), F32),
        ],
        compiler_params=pltpu.CompilerParams(
            dimension_semantics=("parallel",), vmem_limit_bytes=VMEM_LIMIT_BYTES),
        name="in_proj",
    )(x2d, norm_w, *w_big, w_small)


def _heads(fn, *xs):
    return jnp.stack([fn(*(x[h] for x in xs)) for h in range(xs[0].shape[0])])


def _interleave(stage_generators, stages_per_round):
    results = [None] * len(stage_generators)
    live = list(enumerate(stage_generators))
    while live:
        still = []
        for idx, gen in live:
            try:
                for _ in range(stages_per_round[idx]):
                    next(gen)
                still.append((idx, gen))
            except StopIteration as stop:
                results[idx] = stop.value
        live = still
    return results


def _inv_unit_lower(n, level_masks):
    c = n.shape[-1]
    eye = (lax.broadcasted_iota(jnp.int32, (c, c), 0) == lax.broadcasted_iota(jnp.int32, (c, c), 1)).astype(F32)
    n_bf = n.astype(BF16)
    zero = jnp.zeros((), BF16)
    d = eye - jnp.where(level_masks[0], n, 0.0)
    for mask in level_masks[1:]:
        c_bf = jnp.where(mask, n_bf, zero)
        d_bf = d.astype(BF16)
        t = _heads(_f32_dot, d_bf, c_bf)
        yield
        d = d - _heads(_f32_dot, t.astype(BF16), d_bf)
        yield
    return d


def _mlstm_chunk(q, k, v, qk, qc, i_col, b_col, i_row, b_row, causal, c_state, n_state, m_state):
    c = q.shape[1]
    a_col = b_col + m_state
    d = jnp.where(causal, b_col - b_row + i_row, NEG)
    mt = jnp.maximum(a_col, jnp.max(d, axis=2, keepdims=True))
    p = qk * jnp.exp(d - mt)
    w_st = jnp.exp(a_col - mt)
    yield
    num = _heads(_dot, p, v) + w_st * qc
    yield
    den = jnp.sum(p, axis=2, keepdims=True) + w_st * jnp.sum(q * n_state, axis=2, keepdims=True)
    h = num / jnp.maximum(jnp.abs(den), jnp.exp(-mt))
    m_new = mt[:, c - 1:c, :]
    b_last = b_col[:, c - 1:c, :]
    ws_col = jnp.exp(b_last - b_col + i_col - m_new)
    wc = jnp.exp(b_last + m_state - m_new)
    kw = k * ws_col
    yield
    c_new = wc * c_state + _heads(_dot_tn, kw, v)
    n_new = wc * n_state + jnp.sum(kw, axis=1, keepdims=True)
    return h, c_new, n_new, m_new


def _gdn_chunk(q, k, v, qk_kk, beta_col, g_col, g_row, incl, strict, level_masks, s_state):
    c = q.shape[1]
    dh = v.shape[2]
    decay = jnp.exp(jnp.where(incl, g_col - g_row, NEG))
    qk = qk_kk[:, :c]
    kk = qk_kk[:, c:]
    n = jnp.where(strict, beta_col * kk * decay, 0.0)
    yield
    a_inv = yield from _inv_unit_lower(n, level_masks)
    eg_col = jnp.exp(g_col)
    rhs = jnp.concatenate([beta_col * v, (beta_col * eg_col) * k], axis=2)
    yield
    sol = _heads(_dot, a_inv, rhs)
    u = sol[:, :, :dh]
    w = sol[:, :, dh:]
    yield
    wq_s = _heads(_dot, jnp.concatenate([w, q * eg_col], axis=1), s_state)
    v_new = u - wq_s[:, :c]
    yield
    o = wq_s[:, c:] + _heads(_dot, qk * decay, v_new)
    g_last = g_col[:, c - 1:c, :]
    yield
    s_new = jnp.exp(g_last) * s_state + _heads(_dot_tn, k * jnp.exp(g_last - g_col), v_new)
    return o, s_new


def _row_block_bcast(x, block, row):
    nh, c, l = x.shape
    x3 = x.reshape(nh * c // block, block, l)
    return jnp.broadcast_to(x3[:, row:row + 1, :], x3.shape).reshape(nh, c, l)


def _boundary_rows(x, hb):
    if 2 * hb >= SUBLANES:
        return _row_block_bcast(x, 2 * hb, hb - 1)
    sub = lax.broadcasted_iota(jnp.int32, (1, x.shape[1], 1), 1) % SUBLANES
    out = _row_block_bcast(x, SUBLANES, hb - 1)
    for start in range(2 * hb, SUBLANES, 2 * hb):
        out = jnp.where(sub >= start, _row_block_bcast(x, SUBLANES, start + hb - 1), out)
    return out


def _half_block_sizes(c):
    assert c & (c - 1) == 0
    return [1 << e for e in range(c.bit_length() - 1)]


def _level_masks(c):
    t = lax.broadcasted_iota(jnp.int32, (c, c), 0)
    s = lax.broadcasted_iota(jnp.int32, (c, c), 1)
    masks = []
    for hb in _half_block_sizes(c):
        same = (t // (2 * hb)) == (s // (2 * hb))
        masks.append(same & ((t % (2 * hb)) >= hb) & ((s % (2 * hb)) < hb))
    return masks


def _hgrn_chunk(q, k, v, bc, level_masks, s_state):
    nh, c, dh = q.shape
    diag = lax.broadcasted_iota(jnp.int32, (c, c), 0) == lax.broadcasted_iota(jnp.int32, (c, c), 1)
    o = _heads(_dot, q * jnp.exp(bc), s_state)
    yield

    a = jnp.where(diag, jnp.sum(q * k, axis=2, keepdims=True), 0.0)
    for hb, mask in zip(_half_block_sizes(c), level_masks):
        e = jnp.exp(-jnp.abs(bc - _boundary_rows(bc, hb)))
        a = a + jnp.where(mask, _heads(_dot_nt, q * e, k * e), 0.0)
        yield
    o = o + _heads(_dot, a, v)
    yield

    b_last = bc[:, c - 1:c, :]
    decay_col = _heads(lambda r: jnp.broadcast_to(r, (SUBLANES, dh)).T[:, 0:1], jnp.exp(b_last))
    s_new = decay_col * s_state + _heads(_dot_tn, k * jnp.exp(b_last - bc), v)
    return o, s_new


def _mixers_kernel(*refs, layer, chunk, n_valid, bb, has_state, fused):
    gates_ref = None
    if fused:
        x_ref, nw_ref, wa_ref, wb_ref, wc_ref, wsmall_ref, *refs = refs
        *refs, gates_ref, convbuf_last = refs
        refs = [None, None, *refs, convbuf_last]
    if has_state:
        (big_ref, small_ref, bias_ref, alog_ref, mlnw_ref, convw_ref, gdnw_ref, lbl_ref, hgnw_ref,
         c0_ref, n0_ref, m0_ref, sg0_ref, cv0_ref, sh0_ref,
         ys_ref, c_ref, n_ref, m_ref, sg_ref, cv_ref, sh_ref, convbuf) = refs
    else:
        (big_ref, small_ref, bias_ref, alog_ref, mlnw_ref, convw_ref, gdnw_ref, lbl_ref, hgnw_ref,
         ys_ref, c_ref, n_ref, m_ref, sg_ref, cv_ref, sh_ref, convbuf) = refs
    j = pl.program_id(1)
    nj = pl.num_programs(1)
    c = chunk
    flat = not fused and len(big_ref.shape) == 2
    w = MIX_WIDTH
    dh = HEAD_DIM
    nh = N_HEADS
    tail = CONV_W - 1

    @pl.when(j == 0)
    def _():
        if has_state:
            c_ref[...] = c0_ref[...]
            n_ref[...] = n0_ref[...]
            m_ref[...] = m0_ref[...]
            sg_ref[...] = sg0_ref[...]
            sh_ref[...] = sh0_ref[...]
            convbuf[:, SUBLANES - tail:SUBLANES, :] = cv0_ref[...]
        else:
            c_ref[...] = jnp.zeros_like(c_ref)
            n_ref[...] = jnp.zeros_like(n_ref)
            m_ref[...] = jnp.zeros_like(m_ref)
            sg_ref[...] = jnp.zeros_like(sg_ref)
            sh_ref[...] = jnp.zeros_like(sh_ref)
            convbuf[:, SUBLANES - tail:SUBLANES, :] = jnp.zeros((bb, tail, N_BRANCH * w), F32)

    if fused:
        xn = (_rms(x_ref[...].reshape(bb * c, x_ref.shape[2])) * nw_ref[...]).astype(BF16)

        def project(w_ref, lo, hi):
            return jnp.dot(xn, w_ref[:, lo:hi], preferred_element_type=F32).reshape(bb, c, hi - lo)

        small_proj = project(wsmall_ref, 0, SMALL_COLS)
        big_proj = [None] * N_MIX_SEGS
        for group in (2, 1, 0):
            w_ref = (wa_ref, wb_ref, wc_ref)[group]
            for n0 in range(0, 4 * w, PROJ_TN):
                cols = project(w_ref, n0, n0 + PROJ_TN)
                for k0 in range(0, PROJ_TN, w):
                    big_proj[(group * 4 * w + n0 + k0) // w] = cols[:, :, k0:k0 + w]

        def gate_projection():
            n_gate = wc_ref.shape[1] - 4 * w
            for n0 in range(0, n_gate, PROJ_TN):
                gates_ref[:, :, n0:n0 + PROJ_TN] = project(wc_ref, 4 * w + n0, 4 * w + n0 + PROJ_TN)
                yield

    def load_rows(ref, lo, hi):
        if not flat:
            return ref[:, :, lo:hi]
        x = ref[:, lo:hi]
        zeros = jnp.zeros((c - n_valid, hi - lo), x.dtype)
        return jnp.stack([jnp.concatenate([x[b * n_valid:(b + 1) * n_valid], zeros], axis=0) for b in range(bb)])

    def load_wide(lo, hi):
        if fused:
            return jnp.concatenate(big_proj[lo // w:hi // w], axis=2) if hi - lo > w else big_proj[lo // w]
        return load_rows(big_ref, lo, hi)

    def store(col0, y):
        if not flat:
            ys_ref[:, :, col0:col0 + y.shape[2]] = y.astype(ys_ref.dtype)
        else:
            rows_out = jnp.concatenate([y[b, :n_valid] for b in range(bb)], axis=0)
            ys_ref[:, col0:col0 + y.shape[2]] = rows_out.astype(ys_ref.dtype)

    def seg(i):
        return load_wide(i * w, (i + 1) * w)

    def per_seq(fn, x):
        return jnp.stack([fn(x[b]) for b in range(bb)])

    def head_major(x):
        return jnp.stack([x[b][:, h * dh:(h + 1) * dh] for b in range(bb) for h in range(nh)])

    def token_major(x):
        return jnp.stack([jnp.concatenate([x[b * nh + h] for h in range(nh)], axis=1) for b in range(bb)])

    def cols(x, first):
        return jnp.stack([x[b][:, first + h:first + h + 1] for b in range(bb) for h in range(nh)])

    def rows(x, first):
        return jnp.stack([x[b][first + h:first + h + 1, :] for b in range(bb) for h in range(nh)])

    lane = lax.broadcasted_iota(jnp.int32, (1, c, SMALL_COLS), 2)
    rowi = lax.broadcasted_iota(jnp.int32, (1, c, SMALL_COLS), 1)
    pre = (small_proj if fused else load_rows(small_ref, 0, SMALL_COLS)) + bias_ref[...]
    is_f = (lane >= SM_F) & (lane < SM_B)
    is_b = (lane >= SM_B) & (lane < SM_A)
    is_a = (lane >= SM_A) & (lane < SM_A + nh)
    logg = -jnp.exp(alog_ref[...]) * _softplus(pre)
    gates = jnp.where(is_f, _log_sigmoid(pre), jnp.where(is_b, _sigmoid(pre), jnp.where(is_a, logg, pre)))
    gates = jnp.where(rowi < n_valid, gates, jnp.where(lane < SM_F, NEG, 0.0))
    tt = lax.broadcasted_iota(jnp.int32, (c, c), 0)
    ss = lax.broadcasted_iota(jnp.int32, (c, c), 1)
    incl = tt >= ss
    strict = tt > ss
    ltri = incl.astype(F32)
    utri = (ss >= tt).astype(F32)
    cum_src = jnp.where(is_f | is_a, gates, 0.0)
    cums = per_seq(lambda x: _dot_exact_lhs(ltri, x), cum_src)
    gates_t = per_seq(lambda x: x.T, gates)
    cums_t = per_seq(lambda x: _dot_exact_rhs(x.T, utri), cum_src)

    lbl = lbl_ref[...]
    sm = jnp.exp(lbl - jnp.max(lbl, axis=0, keepdims=True))
    sm = sm / jnp.sum(sm, axis=0, keepdims=True)
    lb = jnp.zeros((1, w), F32)
    for l in range(1, layer + 1):
        lb = lb + sm[l:l + 1, :]
    valid = lax.broadcasted_iota(jnp.int32, (1, c, w), 1) < n_valid
    hf = seg(SEG_HG_F)
    g_f = lb + (1.0 - lb) * _sigmoid(hf)
    k_hg = jnp.where(valid, (1.0 - lb) * _sigmoid(-hf), 0.0)
    f_log = jnp.where(valid, jnp.log(g_f), 0.0)
    bc = per_seq(lambda x: _dot_exact_lhs(ltri, x), f_log)

    convbuf[:, SUBLANES:SUBLANES + c, :] = load_wide(SEG_GD_Q * w, (SEG_GD_V + 1) * w)
    conv = convw_ref[CONV_W - 1:CONV_W, :] * convbuf[:, SUBLANES:SUBLANES + c, :]
    for jj in range(CONV_W - 1):
        off = SUBLANES - tail + jj
        conv = conv + convw_ref[jj:jj + 1, :] * convbuf[:, off:off + c, :]
    conv = _silu(conv)
    new_tail = convbuf[:, SUBLANES + n_valid - tail:SUBLANES + n_valid, :]
    convbuf[:, SUBLANES - tail:SUBLANES, :] = new_tail
    cq = head_major(conv[:, :, 0:w])
    ck = head_major(conv[:, :, w:2 * w])
    cv = head_major(conv[:, :, 2 * w:3 * w])
    qn = cq * lax.rsqrt(jnp.sum(cq * cq, axis=2, keepdims=True) + EPS) * (dh ** -0.5)
    kn = ck * lax.rsqrt(jnp.sum(ck * ck, axis=2, keepdims=True) + EPS)
    qk_kk = _heads(_dot_nt, jnp.concatenate([qn, kn], axis=1), kn)

    ml_q = head_major(seg(SEG_ML_Q))
    ml_k = head_major(seg(SEG_ML_K)) * (dh ** -0.5)
    c_old = c_ref[...].reshape(bb * nh, dh, dh)
    ml_qk = _heads(_dot_nt, ml_q, ml_k)
    ml_qc = _heads(_dot, ml_q, c_old)
    m_old = jnp.stack([m_ref[b][:, h:h + 1] for b in range(bb) for h in range(nh)])

    mlstm = _mlstm_chunk(
        ml_q, ml_k, head_major(seg(SEG_ML_V)), ml_qk, ml_qc,
        cols(gates, SM_I), cols(cums, SM_F), rows(gates_t, SM_I), rows(cums_t, SM_F), incl,
        c_old, n_ref[...].reshape(bb * nh, 1, dh), m_old)
    level_masks = _level_masks(c)
    gdn = _gdn_chunk(qn, kn, cv, qk_kk, cols(gates, SM_B), cols(cums, SM_A), rows(cums_t, SM_A),
                     incl, strict, level_masks, sg_ref[...].reshape(bb * nh, dh, dh))
    hgrn = _hgrn_chunk(head_major(_silu(seg(SEG_HG_Q))), head_major(k_hg), head_major(seg(SEG_HG_I)),
                       head_major(bc), level_masks, sh_ref[...].reshape(bb * nh, dh, dh))
    streams, per_round = (mlstm, gdn, hgrn), STAGES_PER_ROUND
    if fused:
        streams, per_round = streams + (gate_projection(),), per_round + (GATE_STAGES_PER_ROUND,)
    (hh, c_new, n_new, m_new), (o_gd, sg_new), (o_hg, sh_new) = _interleave(streams, per_round)[:3]

    c_ref[...] = c_new.reshape(c_ref.shape)
    n_ref[...] = n_new.reshape(n_ref.shape)
    m_lane = lax.broadcasted_iota(jnp.int32, (1, SMALL_COLS), 1)
    m_rows = []
    for b in range(bb):
        m_row = jnp.zeros((1, SMALL_COLS), F32)
        for h in range(nh):
            m_row = jnp.where(m_lane == h, m_new[b * nh + h], m_row)
        m_rows.append(m_row)
    m_ref[...] = jnp.stack(m_rows)
    y_ml = token_major(_rms(hh)) * mlnw_ref[...] * _sigmoid(seg(SEG_ML_O))
    store(0, y_ml)

    sg_ref[...] = sg_new.reshape(sg_ref.shape)
    y_gd = token_major(_rms(o_gd) * gdnw_ref[...]) * _silu(seg(SEG_GD_Z))
    store(w, y_gd)

    sh_ref[...] = sh_new.reshape(sh_ref.shape)
    y_hg = _rms(token_major(o_hg)) * hgnw_ref[...] * _silu(seg(SEG_HG_G))
    store(2 * w, y_hg)

    @pl.when(j == nj - 1)
    def _():
        cv_ref[...] = convbuf[:, SUBLANES - tail:SUBLANES, :]


def _mixers(big2, small2, params, states_in, states_prev, proj_inputs, *, b, t, layer, depth, chunk, bb):
    rows = min(t, chunk)
    assert t % rows == 0 and chunk % BF16_ROWS == 0 and CONV_W - 1 <= rows and b % bb == 0
    fused = proj_inputs is not None
    assert fused or big2.shape[0] == b * t
    w, dh, nh = MIX_WIDTH, HEAD_DIM, N_HEADS
    has_state = states_in is not None
    bias_row, alog_row, ml_norm_w, conv_w, gd_norm_w, lb_logits, hg_norm_w = params

    def const(shape):
        return pl.BlockSpec(shape, lambda i, j: (0,) * len(shape))

    def st(shape):
        return pl.BlockSpec((None, bb) + shape, lambda i, j: (layer, i) + (0,) * len(shape))

    flat = rows < chunk

    def view_shape(l):
        return (b * t, l) if flat else (b, t, l)

    def view(x):
        return x.reshape(view_shape(x.shape[-1]))

    def rows_spec(l):
        if flat:
            return pl.BlockSpec((bb * rows, l), lambda i, j: (i, 0))
        return pl.BlockSpec((bb, rows, l), lambda i, j: (i, j, 0))

    state_shapes = [(nh, dh, dh), (nh, 1, dh), (1, SMALL_COLS), (nh, dh, dh), (CONV_W - 1, N_BRANCH * w), (nh, dh, dh)]
    if fused:
        assert not flat
        x2d, norm_w, w_big, w_small = proj_inputs
        d = x2d.shape[1]
        assert [wg.shape[2] for wg in w_big[:2]] == [4 * w, 4 * w] and w_big[2].shape[2] > 4 * w
        resident = dict(pipeline_mode=pl.Buffered(1))
        in_specs = [rows_spec(d), pl.BlockSpec((None, 1, d), lambda i, j: (layer, 0, 0), **resident)]
        in_specs += [pl.BlockSpec((None, d, wg.shape[2]), lambda i, j: (layer, 0, 0), **resident)
                     for wg in (*w_big, w_small)]
        args = [view(x2d), norm_w, *w_big, w_small]
        n_gate = w_big[2].shape[2] - 4 * w
    else:
        in_specs = [rows_spec(N_MIX_SEGS * w), rows_spec(SMALL_COLS)]
        args = [view(big2), view(small2)]
    in_specs += [
        const((1, SMALL_COLS)), const((1, SMALL_COLS)), const((1, w)), const((CONV_W, N_BRANCH * w)),
        const((1, dh)), const((depth, w)), const((1, w)),
    ]
    args += [bias_row, alog_row, ml_norm_w, conv_w, gd_norm_w, lb_logits, hg_norm_w]
    if has_state:
        in_specs += [st(s) for s in state_shapes]
        args += list(states_in)
    aliases = {}
    if states_prev is not None:
        for k_out, arr in enumerate(states_prev):
            aliases[len(args)] = 1 + k_out
            in_specs.append(pl.BlockSpec(memory_space=pl.ANY))
            args.append(arr)
    out_specs = [rows_spec(N_BRANCH * w)] + [st(s) for s in state_shapes]
    out_shape = [jax.ShapeDtypeStruct(view_shape(N_BRANCH * w), BF16)] + [
        jax.ShapeDtypeStruct((depth, b) + s, F32) for s in state_shapes]
    if fused:
        out_specs.append(rows_spec(n_gate))
        out_shape.append(jax.ShapeDtypeStruct(view_shape(n_gate), F32))

    def body(*refs):
        n_in = len(args) - (len(states_prev) if states_prev is not None else 0)
        kept = refs[:n_in] + refs[len(args):]
        _mixers_kernel(*kept, layer=layer, chunk=chunk, n_valid=rows, bb=bb, has_state=has_state, fused=fused)

    outs = pl.pallas_call(
        body,
        grid=(b // bb, t // rows),
        in_specs=in_specs,
        out_specs=out_specs,
        out_shape=out_shape,
        scratch_shapes=[pltpu.VMEM((bb, SUBLANES + chunk, N_BRANCH * w), F32)],
        input_output_aliases=aliases,
        compiler_params=pltpu.CompilerParams(
            dimension_semantics=("parallel", "arbitrary"), vmem_limit_bytes=VMEM_LIMIT_BYTES),
        name="mixers",
    )(*args)
    gates = outs[-1].reshape(b * t, -1) if fused else None
    return outs[0].reshape(b * t, N_BRANCH * w), tuple(outs[1:1 + len(state_shapes)]), gates


def _merge_ffn_kernel(x_ref, ys_ref, gate_ref, wbr_ref, wout_ref, n2_ref, wup_ref, wdown_ref, fin_ref,
                      out_ref, *, last):
    w = MIX_WIDTH
    d = x_ref.shape[1]
    merged = jnp.zeros(x_ref.shape, F32)
    for n in range(N_BRANCH):
        br = jnp.dot(ys_ref[:, n * w:(n + 1) * w], wbr_ref[n], preferred_element_type=F32)
        merged = merged + _sigmoid(gate_ref[:, n * d:(n + 1) * d]) * br
    x1 = x_ref[...] + jnp.dot(merged.astype(BF16), wout_ref[...], preferred_element_type=F32)
    xn = (_rms(x1) * n2_ref[...]).astype(BF16)
    hid = jnp.square(jnp.maximum(jnp.dot(xn, wup_ref[...], preferred_element_type=F32), 0.0))
    x2 = x1 + jnp.dot(hid.astype(BF16), wdown_ref[...], preferred_element_type=F32)
    if last:
        x2 = _rms(x2) * fin_ref[...]
    out_ref[...] = x2


def _merge_ffn(x2d, ys2d, gate_src, w_branch, w_out, norm2_w, w_up, w_down, final_w, *, layer, tm, last):
    m, d = x2d.shape
    w = MIX_WIDTH
    d_ff = w_up.shape[2]
    gate_block = N_BRANCH * d
    assert m % tm == 0 and gate_src.shape[1] % gate_block == 0
    gate_idx = gate_src.shape[1] // gate_block - 1

    def per_layer(shape):
        return pl.BlockSpec((None,) + shape, lambda i: (layer,) + (0,) * len(shape), pipeline_mode=pl.Buffered(1))

    return pl.pallas_call(
        functools.partial(_merge_ffn_kernel, last=last),
        grid=(m // tm,),
        in_specs=[
            pl.BlockSpec((tm, d), lambda i: (i, 0)),
            pl.BlockSpec((tm, N_BRANCH * w), lambda i: (i, 0)),
            pl.BlockSpec((tm, gate_block), lambda i: (i, gate_idx)),
            per_layer((N_BRANCH, w, d)), per_layer((d, d)), per_layer((1, d)), per_layer((d, d_ff)),
            per_layer((d_ff, d)),
            pl.BlockSpec((1, d), lambda i: (0, 0), pipeline_mode=pl.Buffered(1)),
        ],
        out_specs=pl.BlockSpec((tm, d), lambda i: (i, 0)),
        out_shape=jax.ShapeDtypeStruct((m, d), F32),
        compiler_params=pltpu.CompilerParams(
            dimension_semantics=("parallel",), vmem_limit_bytes=VMEM_LIMIT_BYTES),
        name="merge_ffn",
    )(x2d, ys2d, gate_src, w_branch, w_out, norm2_w, w_up, w_down, final_w)


def _split_w_in(w_in):
    w, nh = MIX_WIDTH, N_HEADS
    a0, a1 = 4 * w, 4 * w + 2 * nh
    b0, b1 = a1 + 4 * w, a1 + 4 * w + 2 * nh
    w_big = tuple(part.astype(BF16) for part in (w_in[:, :, :a0], w_in[:, :, a1:b0], w_in[:, :, b1:]))
    small = jnp.concatenate([w_in[:, :, a0:a1], w_in[:, :, b0:b1]], axis=2)
    w_small = jnp.pad(small, ((0, 0), (0, 0), (0, SMALL_COLS - 4 * nh))).astype(BF16)
    return w_big, w_small


def _pad_lanes(parts, total):
    row = jnp.concatenate([p.reshape(1, -1).astype(F32) for p in parts], axis=1)
    return jnp.pad(row, ((0, 0), (0, total - row.shape[1])))


def _tiles(m):
    return math.gcd(m, 256), math.gcd(m, 256)


def _run_group(x3, states_in, weights, mix_params, seqs_per_step):
    b, t, d = x3.shape
    depth = len(mix_params)
    chunk = CHUNK if t % CHUNK == 0 else BF16_ROWS
    assert t % CHUNK == 0 or t <= BF16_ROWS
    x2d = x3.reshape(b * t, d)
    tm, tm_merge = _tiles(b * t)
    states_prev = None
    fuse_projection = t % CHUNK == 0
    for l in range(depth):
        big = small = proj_inputs = None
        if fuse_projection:
            proj_inputs = (x2d, weights["norm1_w"], weights["w_big"], weights["w_small"])
        else:
            big, small = _in_proj(x2d, weights["norm1_w"], weights["w_big"], weights["w_small"], layer=l, tm=tm)
        ys, states_prev, gates = _mixers(
            big, small, mix_params[l], states_in, states_prev, proj_inputs,
            b=b, t=t, layer=l, depth=depth, chunk=chunk, bb=math.gcd(b, seqs_per_step))
        x2d = _merge_ffn(x2d, ys, gates if fuse_projection else big, weights["w_branch"], weights["w_out"],
                         weights["norm2_w"], weights["w_up"], weights["w_down"], weights["final_norm_w"],
                         layer=l, tm=tm_merge, last=(l == depth - 1))
    return x2d.reshape(b, t, d), states_prev


def kernel(x_prompt, x_sample, state_mlstm_C, state_mlstm_n, state_mlstm_m, state_gdn_S, state_gdn_conv, state_hgrn_S, norm1_w, w_in, ml_i_bias, ml_f_bias, ml_norm_w, gd_conv_w, gd_A_log, gd_dt_bias, gd_norm_w, hg_lb_logits, hg_norm_w, w_branch, w_out, norm2_w, w_up, w_down, final_norm_w):
    depth, d = norm1_w.shape
    nh = N_HEADS
    w_big, w_small = _split_w_in(w_in)
    weights = dict(
        norm1_w=norm1_w.reshape(depth, 1, d), w_big=w_big, w_small=w_small,
        w_branch=w_branch.astype(BF16), w_out=w_out.astype(BF16), norm2_w=norm2_w.reshape(depth, 1, d),
        w_up=w_up.astype(BF16), w_down=w_down.astype(BF16), final_norm_w=final_norm_w.reshape(1, d))
    mix_params = []
    for l in range(depth):
        zeros_h = jnp.zeros((nh,), F32)
        bias_row = _pad_lanes([ml_i_bias[l], ml_f_bias[l], zeros_h, gd_dt_bias[l]], SMALL_COLS)
        alog_row = _pad_lanes([zeros_h, zeros_h, zeros_h, gd_A_log[l]], SMALL_COLS)
        mix_params.append((bias_row, alog_row, ml_norm_w[l].reshape(1, -1), gd_conv_w[l],
                           gd_norm_w[l].reshape(1, -1), hg_lb_logits, hg_norm_w[l].reshape(1, -1)))

    def unpack(states):
        s_c, s_n, s_m, s_g, s_cv, s_h = states
        return s_c, s_n.reshape(s_n.shape[:3] + s_n.shape[4:]), s_m[:, :, 0, :nh], s_g, s_cv, s_h

    y_prompt, p_states = _run_group(x_prompt, None, weights, mix_params, PROMPT_SEQS_PER_STEP)

    bs = x_sample.shape[0]
    m_in = jnp.pad(state_mlstm_m, ((0, 0), (0, 0), (0, SMALL_COLS - nh))).reshape(depth, bs, 1, SMALL_COLS)
    n_in = state_mlstm_n.reshape(depth, bs, nh, 1, HEAD_DIM)
    s_in = (state_mlstm_C, n_in, m_in, state_gdn_S, state_gdn_conv, state_hgrn_S)
    y_sample, s_states = _run_group(x_sample, s_in, weights, mix_params, SAMPLE_SEQS_PER_STEP)

    return (y_prompt, y_sample) + unpack(p_states) + unpack(s_states)
```

```python
import functools
import math

import jax
import jax.numpy as jnp
from jax import lax
from jax.experimental import pallas as pl
from jax.experimental.pallas import tpu as pltpu

F32 = jnp.float32
BF16 = jnp.bfloat16

HEAD_DIM = 128
N_HEADS = 4
MIX_WIDTH = N_HEADS * HEAD_DIM
N_BRANCH = 3
CONV_W = 4
EPS = 1e-6
CHUNK = 128
SUBLANES = 8
BF16_ROWS = 16
SMALL_COLS = 128
NEG = -1e30
VMEM_LIMIT_BYTES = 56 * 1024 * 1024
PROJ_TN = 1024
PROMPT_SEQS_PER_STEP = 2
SAMPLE_SEQS_PER_STEP = 8
STAGES_PER_ROUND = (1, 4, 2)
GATE_STAGES_PER_ROUND = 3

SEG_ML_Q, SEG_ML_K, SEG_ML_V, SEG_ML_O = 0, 1, 2, 3
SEG_GD_Q, SEG_GD_K, SEG_GD_V, SEG_GD_Z = 4, 5, 6, 7
SEG_HG_Q, SEG_HG_F, SEG_HG_I, SEG_HG_G = 8, 9, 10, 11
N_MIX_SEGS = 12
SM_I, SM_F, SM_B, SM_A = 0, N_HEADS, 2 * N_HEADS, 3 * N_HEADS


def _dot(a, b):
    return jnp.dot(a.astype(BF16), b.astype(BF16), preferred_element_type=F32)


def _dot_nt(a, b):
    return lax.dot_general(a.astype(BF16), b.astype(BF16), (((1,), (1,)), ((), ())),
                           preferred_element_type=F32)


def _dot_tn(a, b):
    return _dot(a.T, b)


def _f32_dot(a, b):
    return jnp.dot(a, b, preferred_element_type=F32)


def _split3(a):
    hi = a.astype(BF16)
    r = a - hi.astype(F32)
    mid = r.astype(BF16)
    lo = (r - mid.astype(F32)).astype(BF16)
    return hi, mid, lo


def _dot_exact_lhs(a01, b):
    a = a01.astype(BF16)
    return sum(_f32_dot(a, part) for part in _split3(b))


def _dot_exact_rhs(a, b01):
    b = b01.astype(BF16)
    return sum(_f32_dot(part, b) for part in _split3(a))


def _sigmoid(x):
    return 1.0 / (1.0 + jnp.exp(-x))


def _silu(x):
    return x * _sigmoid(x)


def _softplus(x):
    return jnp.maximum(x, 0.0) + jnp.log(1.0 + jnp.exp(-jnp.abs(x)))


def _log_sigmoid(x):
    return -_softplus(-x)


def _rms(x, axis=-1):
    return x * lax.rsqrt(jnp.mean(x * x, axis=axis, keepdims=True) + EPS)


def _in_proj_kernel(x_ref, nw_ref, *refs):
    *wbig_refs, wsmall_ref, big_ref, small_ref = refs
    xn = (_rms(x_ref[...]) * nw_ref[...]).astype(BF16)
    small_ref[...] = jnp.dot(xn, wsmall_ref[...], preferred_element_type=F32)
    out0 = 0
    for wbig_ref in wbig_refs:
        for n0 in range(0, wbig_ref.shape[1], PROJ_TN):
            big_ref[:, out0 + n0:out0 + n0 + PROJ_TN] = jnp.dot(
                xn, wbig_ref[:, n0:n0 + PROJ_TN], preferred_element_type=F32)
        out0 += wbig_ref.shape[1]


def _in_proj(x2d, norm_w, w_big, w_small, *, layer, tm):
    m, d = x2d.shape
    n_big = sum(wg.shape[2] for wg in w_big)
    assert m % tm == 0 and all(wg.shape[2] % PROJ_TN == 0 for wg in w_big)
    resident = dict(pipeline_mode=pl.Buffered(1))
    return pl.pallas_call(
        _in_proj_kernel,
        grid=(m // tm,),
        in_specs=[
            pl.BlockSpec((tm, d), lambda i: (i, 0)),
            pl.BlockSpec((None, 1, d), lambda i: (layer, 0, 0), **resident),
            *[pl.BlockSpec((None, d, wg.shape[2]), lambda i: (layer, 0, 0), **resident) for wg in w_big],
            pl.BlockSpec((None, d, SMALL_COLS), lambda i: (layer, 0, 0), **resident),
        ],
        out_specs=[
            pl.BlockSpec((tm, n_big), lambda i: (i, 0)),
            pl.BlockSpec((tm, SMALL_COLS), lambda i: (i, 0)),
        ],
        out_shape=[
            jax.ShapeDtypeStruct((m, n_big), F32),
            jax.ShapeDtypeStruct((m, SMALL_COLS), F32),
        ],
        compiler_params=pltpu.CompilerParams(
            dimension_semantics=("parallel",), vmem_limit_bytes=VMEM_LIMIT_BYTES),
        name="in_proj",
    )(x2d, norm_w, *w_big, w_small)


def _heads(fn, *xs):
    return jnp.stack([fn(*(x[h] for x in xs)) for h in range(xs[0].shape[0])])


def _interleave(stage_generators, stages_per_round):
    results = [None] * len(stage_generators)
    live = list(enumerate(stage_generators))
    while live:
        still = []
        for idx, gen in live:
            try:
                for _ in range(stages_per_round[idx]):
                    next(gen)
                still.append((idx, gen))
            except StopIteration as stop:
                results[idx] = stop.value
        live = still
    return results


def _inv_unit_lower(n, level_masks):
    c = n.shape[-1]
    eye = (lax.broadcasted_iota(jnp.int32, (c, c), 0) == lax.broadcasted_iota(jnp.int32, (c, c), 1)).astype(F32)
    n_bf = n.astype(BF16)
    zero = jnp.zeros((), BF16)
    d = eye - jnp.where(level_masks[0], n, 0.0)
    for mask in level_masks[1:]:
        c_bf = jnp.where(mask, n_bf, zero)
        d_bf = d.astype(BF16)
        t = _heads(_f32_dot, d_bf, c_bf)
        yield
        d = d - _heads(_f32_dot, t.astype(BF16), d_bf)
        yield
    return d


def _mlstm_chunk(q, k, v, qk, qc, i_col, b_col, i_row, b_row, causal, c_state, n_state, m_state):
    c = q.shape[1]
    a_col = b_col + m_state
    d = jnp.where(causal, b_col - b_row + i_row, NEG)
    mt = jnp.maximum(a_col, jnp.max(d, axis=2, keepdims=True))
    p = qk * jnp.exp(d - mt)
    w_st = jnp.exp(a_col - mt)
    yield
    num = _heads(_dot, p, v) + w_st * qc
    yield
    den = jnp.sum(p, axis=2, keepdims=True) + w_st * jnp.sum(q * n_state, axis=2, keepdims=True)
    h = num / jnp.maximum(jnp.abs(den), jnp.exp(-mt))
    m_new = mt[:, c - 1:c, :]
    b_last = b_col[:, c - 1:c, :]
    ws_col = jnp.exp(b_last - b_col + i_col - m_new)
    wc = jnp.exp(b_last + m_state - m_new)
    kw = k * ws_col
    yield
    c_new = wc * c_state + _heads(_dot_tn, kw, v)
    n_new = wc * n_state + jnp.sum(kw, axis=1, keepdims=True)
    return h, c_new, n_new, m_new


def _gdn_chunk(q, k, v, qk_kk, beta_col, g_col, g_row, incl, strict, level_masks, s_state):
    c = q.shape[1]
    dh = v.shape[2]
    decay = jnp.exp(jnp.where(incl, g_col - g_row, NEG))
    qk = qk_kk[:, :c]
    kk = qk_kk[:, c:]
    n = jnp.where(strict, beta_col * kk * decay, 0.0)
    yield
    a_inv = yield from _inv_unit_lower(n, level_masks)
    eg_col = jnp.exp(g_col)
    rhs = jnp.concatenate([beta_col * v, (beta_col * eg_col) * k], axis=2)
    yield
    sol = _heads(_dot, a_inv, rhs)
    u = sol[:, :, :dh]
    w = sol[:, :, dh:]
    yield
    wq_s = _heads(_dot, jnp.concatenate([w, q * eg_col], axis=1), s_state)
    v_new = u - wq_s[:, :c]
    yield
    o = wq_s[:, c:] + _heads(_dot, qk * decay, v_new)
    g_last = g_col[:, c - 1:c, :]
    yield
    s_new = jnp.exp(g_last) * s_state + _heads(_dot_tn, k * jnp.exp(g_last - g_col), v_new)
    return o, s_new


def _row_block_bcast(x, block, row):
    nh, c, l = x.shape
    x3 = x.reshape(nh * c // block, block, l)
    return jnp.broadcast_to(x3[:, row:row + 1, :], x3.shape).reshape(nh, c, l)


def _boundary_rows(x, hb):
    if 2 * hb >= SUBLANES:
        return _row_block_bcast(x, 2 * hb, hb - 1)
    sub = lax.broadcasted_iota(jnp.int32, (1, x.shape[1], 1), 1) % SUBLANES
    out = _row_block_bcast(x, SUBLANES, hb - 1)
    for start in range(2 * hb, SUBLANES, 2 * hb):
        out = jnp.where(sub >= start, _row_block_bcast(x, SUBLANES, start + hb - 1), out)
    return out


def _half_block_sizes(c):
    assert c & (c - 1) == 0
    return [1 << e for e in range(c.bit_length() - 1)]


def _level_masks(c):
    t = lax.broadcasted_iota(jnp.int32, (c, c), 0)
    s = lax.broadcasted_iota(jnp.int32, (c, c), 1)
    masks = []
    for hb in _half_block_sizes(c):
        same = (t // (2 * hb)) == (s // (2 * hb))
        masks.append(same & ((t % (2 * hb)) >= hb) & ((s % (2 * hb)) < hb))
    return masks


def _hgrn_chunk(q, k, v, bc, level_masks, s_state):
    nh, c, dh = q.shape
    diag = lax.broadcasted_iota(jnp.int32, (c, c), 0) == lax.broadcasted_iota(jnp.int32, (c, c), 1)
    o = _heads(_dot, q * jnp.exp(bc), s_state)
    yield

    a = jnp.where(diag, jnp.sum(q * k, axis=2, keepdims=True), 0.0)
    for hb, mask in zip(_half_block_sizes(c), level_masks):
        e = jnp.exp(-jnp.abs(bc - _boundary_rows(bc, hb)))
        a = a + jnp.where(mask, _heads(_dot_nt, q * e, k * e), 0.0)
        yield
    o = o + _heads(_dot, a, v)
    yield

    b_last = bc[:, c - 1:c, :]
    decay_col = _heads(lambda r: jnp.broadcast_to(r, (SUBLANES, dh)).T[:, 0:1], jnp.exp(b_last))
    s_new = decay_col * s_state + _heads(_dot_tn, k * jnp.exp(b_last - bc), v)
    return o, s_new


def _mixers_kernel(*refs, layer, chunk, n_valid, bb, has_state, fused):
    gates_ref = None
    if fused:
        x_ref, nw_ref, wa_ref, wb_ref, wc_ref, wsmall_ref, *refs = refs
        *refs, gates_ref, convbuf_last = refs
        refs = [None, None, *refs, convbuf_last]
    if has_state:
        (big_ref, small_ref, bias_ref, alog_ref, mlnw_ref, convw_ref, gdnw_ref, lbl_ref, hgnw_ref,
         c0_ref, n0_ref, m0_ref, sg0_ref, cv0_ref, sh0_ref,
         ys_ref, c_ref, n_ref, m_ref, sg_ref, cv_ref, sh_ref, convbuf) = refs
    else:
        (big_ref, small_ref, bias_ref, alog_ref, mlnw_ref, convw_ref, gdnw_ref, lbl_ref, hgnw_ref,
         ys_ref, c_ref, n_ref, m_ref, sg_ref, cv_ref, sh_ref, convbuf) = refs
    j = pl.program_id(1)
    nj = pl.num_programs(1)
    c = chunk
    flat = not fused and len(big_ref.shape) == 2
    w = MIX_WIDTH
    dh = HEAD_DIM
    nh = N_HEADS
    tail = CONV_W - 1

    @pl.when(j == 0)
    def _():
        if has_state:
            c_ref[...] = c0_ref[...]
            n_ref[...] = n0_ref[...]
            m_ref[...] = m0_ref[...]
            sg_ref[...] = sg0_ref[...]
            sh_ref[...] = sh0_ref[...]
            convbuf[:, SUBLANES - tail:SUBLANES, :] = cv0_ref[...]
        else:
            c_ref[...] = jnp.zeros_like(c_ref)
            n_ref[...] = jnp.zeros_like(n_ref)
            m_ref[...] = jnp.zeros_like(m_ref)
            sg_ref[...] = jnp.zeros_like(sg_ref)
            sh_ref[...] = jnp.zeros_like(sh_ref)
            convbuf[:, SUBLANES - tail:SUBLANES, :] = jnp.zeros((bb, tail, N_BRANCH * w), F32)

    if fused:
        xn = (_rms(x_ref[...].reshape(bb * c, x_ref.shape[2])) * nw_ref[...]).astype(BF16)

        def project(w_ref, lo, hi):
            return jnp.dot(xn, w_ref[:, lo:hi], preferred_element_type=F32).reshape(bb, c, hi - lo)

        small_proj = project(wsmall_ref, 0, SMALL_COLS)
        big_proj = []
        for w_ref in (wa_ref, wb_ref, wc_ref):
            for n0 in range(0, 4 * w, PROJ_TN):
                cols = project(w_ref, n0, n0 + PROJ_TN)
                big_proj += [cols[:, :, k0:k0 + w] for k0 in range(0, PROJ_TN, w)]

        def gate_projection():
            n_gate = wc_ref.shape[1] - 4 * w
            for n0 in range(0, n_gate, PROJ_TN):
                gates_ref[:, :, n0:n0 + PROJ_TN] = project(wc_ref, 4 * w + n0, 4 * w + n0 + PROJ_TN)
                yield

    def load_rows(ref, lo, hi):
        if not flat:
            return ref[:, :, lo:hi]
        x = ref[:, lo:hi]
        zeros = jnp.zeros((c - n_valid, hi - lo), x.dtype)
        return jnp.stack([jnp.concatenate([x[b * n_valid:(b + 1) * n_valid], zeros], axis=0) for b in range(bb)])

    def load_wide(lo, hi):
        if fused:
            return jnp.concatenate(big_proj[lo // w:hi // w], axis=2) if hi - lo > w else big_proj[lo // w]
        return load_rows(big_ref, lo, hi)

    def store(col0, y):
        if not flat:
            ys_ref[:, :, col0:col0 + y.shape[2]] = y.astype(ys_ref.dtype)
        else:
            rows_out = jnp.concatenate([y[b, :n_valid] for b in range(bb)], axis=0)
            ys_ref[:, col0:col0 + y.shape[2]] = rows_out.astype(ys_ref.dtype)

    def seg(i):
        return load_wide(i * w, (i + 1) * w)

    def per_seq(fn, x):
        return jnp.stack([fn(x[b]) for b in range(bb)])

    def head_major(x):
        return jnp.stack([x[b][:, h * dh:(h + 1) * dh] for b in range(bb) for h in range(nh)])

    def token_major(x):
        return jnp.stack([jnp.concatenate([x[b * nh + h] for h in range(nh)], axis=1) for b in range(bb)])

    def cols(x, first):
        return jnp.stack([x[b][:, first + h:first + h + 1] for b in range(bb) for h in range(nh)])

    def rows(x, first):
        return jnp.stack([x[b][first + h:first + h + 1, :] for b in range(bb) for h in range(nh)])

    lane = lax.broadcasted_iota(jnp.int32, (1, c, SMALL_COLS), 2)
    rowi = lax.broadcasted_iota(jnp.int32, (1, c, SMALL_COLS), 1)
    pre = (small_proj if fused else load_rows(small_ref, 0, SMALL_COLS)) + bias_ref[...]
    is_f = (lane >= SM_F) & (lane < SM_B)
    is_b = (lane >= SM_B) & (lane < SM_A)
    is_a = (lane >= SM_A) & (lane < SM_A + nh)
    logg = -jnp.exp(alog_ref[...]) * _softplus(pre)
    gates = jnp.where(is_f, _log_sigmoid(pre), jnp.where(is_b, _sigmoid(pre), jnp.where(is_a, logg, pre)))
    gates = jnp.where(rowi < n_valid, gates, jnp.where(lane < SM_F, NEG, 0.0))
    tt = lax.broadcasted_iota(jnp.int32, (c, c), 0)
    ss = lax.broadcasted_iota(jnp.int32, (c, c), 1)
    incl = tt >= ss
    strict = tt > ss
    ltri = incl.astype(F32)
    utri = (ss >= tt).astype(F32)
    cum_src = jnp.where(is_f | is_a, gates, 0.0)
    cums = per_seq(lambda x: _dot_exact_lhs(ltri, x), cum_src)
    gates_t = per_seq(lambda x: x.T, gates)
    cums_t = per_seq(lambda x: _dot_exact_rhs(x.T, utri), cum_src)

    lbl = lbl_ref[...]
    sm = jnp.exp(lbl - jnp.max(lbl, axis=0, keepdims=True))
    sm = sm / jnp.sum(sm, axis=0, keepdims=True)
    lb = jnp.zeros((1, w), F32)
    for l in range(1, layer + 1):
        lb = lb + sm[l:l + 1, :]
    valid = lax.broadcasted_iota(jnp.int32, (1, c, w), 1) < n_valid
    hf = seg(SEG_HG_F)
    g_f = lb + (1.0 - lb) * _sigmoid(hf)
    k_hg = jnp.where(valid, (1.0 - lb) * _sigmoid(-hf), 0.0)
    f_log = jnp.where(valid, jnp.log(g_f), 0.0)
    bc = per_seq(lambda x: _dot_exact_lhs(ltri, x), f_log)

    convbuf[:, SUBLANES:SUBLANES + c, :] = load_wide(SEG_GD_Q * w, (SEG_GD_V + 1) * w)
    conv = convw_ref[CONV_W - 1:CONV_W, :] * convbuf[:, SUBLANES:SUBLANES + c, :]
    for jj in range(CONV_W - 1):
        off = SUBLANES - tail + jj
        conv = conv + convw_ref[jj:jj + 1, :] * convbuf[:, off:off + c, :]
    conv = _silu(conv)
    new_tail = convbuf[:, SUBLANES + n_valid - tail:SUBLANES + n_valid, :]
    convbuf[:, SUBLANES - tail:SUBLANES, :] = new_tail
    cq = head_major(conv[:, :, 0:w])
    ck = head_major(conv[:, :, w:2 * w])
    cv = head_major(conv[:, :, 2 * w:3 * w])
    qn = cq * lax.rsqrt(jnp.sum(cq * cq, axis=2, keepdims=True) + EPS) * (dh ** -0.5)
    kn = ck * lax.rsqrt(jnp.sum(ck * ck, axis=2, keepdims=True) + EPS)
    qk_kk = _heads(_dot_nt, jnp.concatenate([qn, kn], axis=1), kn)

    ml_q = head_major(seg(SEG_ML_Q))
    ml_k = head_major(seg(SEG_ML_K)) * (dh ** -0.5)
    c_old = c_ref[...].reshape(bb * nh, dh, dh)
    ml_qk = _heads(_dot_nt, ml_q, ml_k)
    ml_qc = _heads(_dot, ml_q, c_old)
    m_old = jnp.stack([m_ref[b][:, h:h + 1] for b in range(bb) for h in range(nh)])

    mlstm = _mlstm_chunk(
        ml_q, ml_k, head_major(seg(SEG_ML_V)), ml_qk, ml_qc,
        cols(gates, SM_I), cols(cums, SM_F), rows(gates_t, SM_I), rows(cums_t, SM_F), incl,
        c_old, n_ref[...].reshape(bb * nh, 1, dh), m_old)
    level_masks = _level_masks(c)
    gdn = _gdn_chunk(qn, kn, cv, qk_kk, cols(gates, SM_B), cols(cums, SM_A), rows(cums_t, SM_A),
                     incl, strict, level_masks, sg_ref[...].reshape(bb * nh, dh, dh))
    hgrn = _hgrn_chunk(head_major(_silu(seg(SEG_HG_Q))), head_major(k_hg), head_major(seg(SEG_HG_I)),
                       head_major(bc), level_masks, sh_ref[...].reshape(bb * nh, dh, dh))
    streams, per_round = (mlstm, gdn, hgrn), STAGES_PER_ROUND
    if fused:
        streams, per_round = streams + (gate_projection(),), per_round + (GATE_STAGES_PER_ROUND,)
    (hh, c_new, n_new, m_new), (o_gd, sg_new), (o_hg, sh_new) = _interleave(streams, per_round)[:3]

    c_ref[...] = c_new.reshape(c_ref.shape)
    n_ref[...] = n_new.reshape(n_ref.shape)
    m_lane = lax.broadcasted_iota(jnp.int32, (1, SMALL_COLS), 1)
    m_rows = []
    for b in range(bb):
        m_row = jnp.zeros((1, SMALL_COLS), F32)
        for h in range(nh):
            m_row = jnp.where(m_lane == h, m_new[b * nh + h], m_row)
        m_rows.append(m_row)
    m_ref[...] = jnp.stack(m_rows)
    y_ml = token_major(_rms(hh)) * mlnw_ref[...] * _sigmoid(seg(SEG_ML_O))
    store(0, y_ml)

    sg_ref[...] = sg_new.reshape(sg_ref.shape)
    y_gd = token_major(_rms(o_gd) * gdnw_ref[...]) * _silu(seg(SEG_GD_Z))
    store(w, y_gd)

    sh_ref[...] = sh_new.reshape(sh_ref.shape)
    y_hg = _rms(token_major(o_hg)) * hgnw_ref[...] * _silu(seg(SEG_HG_G))
    store(2 * w, y_hg)

    @pl.when(j == nj - 1)
    def _():
        cv_ref[...] = convbuf[:, SUBLANES - tail:SUBLANES, :]


def _mixers(big2, small2, params, states_in, states_prev, proj_inputs, *, b, t, layer, depth, chunk, bb):
    rows = min(t, chunk)
    assert t % rows == 0 and chunk % BF16_ROWS == 0 and CONV_W - 1 <= rows and b % bb == 0
    fused = proj_inputs is not None
    assert fused or big2.shape[0] == b * t
    w, dh, nh = MIX_WIDTH, HEAD_DIM, N_HEADS
    has_state = states_in is not None
    bias_row, alog_row, ml_norm_w, conv_w, gd_norm_w, lb_logits, hg_norm_w = params

    def const(shape):
        return pl.BlockSpec(shape, lambda i, j: (0,) * len(shape))

    def st(shape):
        return pl.BlockSpec((None, bb) + shape, lambda i, j: (layer, i) + (0,) * len(shape))

    flat = rows < chunk

    def view_shape(l):
        return (b * t, l) if flat else (b, t, l)

    def view(x):
        return x.reshape(view_shape(x.shape[-1]))

    def rows_spec(l):
        if flat:
            return pl.BlockSpec((bb * rows, l), lambda i, j: (i, 0))
        return pl.BlockSpec((bb, rows, l), lambda i, j: (i, j, 0))

    state_shapes = [(nh, dh, dh), (nh, 1, dh), (1, SMALL_COLS), (nh, dh, dh), (CONV_W - 1, N_BRANCH * w), (nh, dh, dh)]
    if fused:
        assert not flat
        x2d, norm_w, w_big, w_small = proj_inputs
        d = x2d.shape[1]
        assert [wg.shape[2] for wg in w_big[:2]] == [4 * w, 4 * w] and w_big[2].shape[2] > 4 * w
        resident = dict(pipeline_mode=pl.Buffered(1))
        in_specs = [rows_spec(d), pl.BlockSpec((None, 1, d), lambda i, j: (layer, 0, 0), **resident)]
        in_specs += [pl.BlockSpec((None, d, wg.shape[2]), lambda i, j: (layer, 0, 0), **resident)
                     for wg in (*w_big, w_small)]
        args = [view(x2d), norm_w, *w_big, w_small]
        n_gate = w_big[2].shape[2] - 4 * w
    else:
        in_specs = [rows_spec(N_MIX_SEGS * w), rows_spec(SMALL_COLS)]
        args = [view(big2), view(small2)]
    in_specs += [
        const((1, SMALL_COLS)), const((1, SMALL_COLS)), const((1, w)), const((CONV_W, N_BRANCH * w)),
        const((1, dh)), const((depth, w)), const((1, w)),
    ]
    args += [bias_row, alog_row, ml_norm_w, conv_w, gd_norm_w, lb_logits, hg_norm_w]
    if has_state:
        in_specs += [st(s) for s in state_shapes]
        args += list(states_in)
    aliases = {}
    if states_prev is not None:
        for k_out, arr in enumerate(states_prev):
            aliases[len(args)] = 1 + k_out
            in_specs.append(pl.BlockSpec(memory_space=pl.ANY))
            args.append(arr)
    out_specs = [rows_spec(N_BRANCH * w)] + [st(s) for s in state_shapes]
    out_shape = [jax.ShapeDtypeStruct(view_shape(N_BRANCH * w), BF16)] + [
        jax.ShapeDtypeStruct((depth, b) + s, F32) for s in state_shapes]
    if fused:
        out_specs.append(rows_spec(n_gate))
        out_shape.append(jax.ShapeDtypeStruct(view_shape(n_gate), F32))

    def body(*refs):
        n_in = len(args) - (len(states_prev) if states_prev is not None else 0)
        kept = refs[:n_in] + refs[len(args):]
        _mixers_kernel(*kept, layer=layer, chunk=chunk, n_valid=rows, bb=bb, has_state=has_state, fused=fused)

    outs = pl.pallas_call(
        body,
        grid=(b // bb, t // rows),
        in_specs=in_specs,
        out_specs=out_specs,
        out_shape=out_shape,
        scratch_shapes=[pltpu.VMEM((bb, SUBLANES + chunk, N_BRANCH * w), F32)],
        input_output_aliases=aliases,
        compiler_params=pltpu.CompilerParams(
            dimension_semantics=("parallel", "arbitrary"), vmem_limit_bytes=VMEM_LIMIT_BYTES),
        name="mixers",
    )(*args)
    gates = outs[-1].reshape(b * t, -1) if fused else None
    return outs[0].reshape(b * t, N_BRANCH * w), tuple(outs[1:1 + len(state_shapes)]), gates


def _merge_ffn_kernel(x_ref, ys_ref, gate_ref, wbr_ref, wout_ref, n2_ref, wup_ref, wdown_ref, fin_ref,
                      out_ref, *, last):
    w = MIX_WIDTH
    d = x_ref.shape[1]
    merged = jnp.zeros(x_ref.shape, F32)
    for n in range(N_BRANCH):
        br = jnp.dot(ys_ref[:, n * w:(n + 1) * w], wbr_ref[n], preferred_element_type=F32)
        merged = merged + _sigmoid(gate_ref[:, n * d:(n + 1) * d]) * br
    x1 = x_ref[...] + jnp.dot(merged.astype(BF16), wout_ref[...], preferred_element_type=F32)
    xn = (_rms(x1) * n2_ref[...]).astype(BF16)
    hid = jnp.square(jnp.maximum(jnp.dot(xn, wup_ref[...], preferred_element_type=F32), 0.0))
    x2 = x1 + jnp.dot(hid.astype(BF16), wdown_ref[...], preferred_element_type=F32)
    if last:
        x2 = _rms(x2) * fin_ref[...]
    out_ref[...] = x2


def _merge_ffn(x2d, ys2d, gate_src, w_branch, w_out, norm2_w, w_up, w_down, final_w, *, layer, tm, last):
    m, d = x2d.shape
    w = MIX_WIDTH
    d_ff = w_up.shape[2]
    gate_block = N_BRANCH * d
    assert m % tm == 0 and gate_src.shape[1] % gate_block == 0
    gate_idx = gate_src.shape[1] // gate_block - 1

    def per_layer(shape):
        return pl.BlockSpec((None,) + shape, lambda i: (layer,) + (0,) * len(shape), pipeline_mode=pl.Buffered(1))

    return pl.pallas_call(
        functools.partial(_merge_ffn_kernel, last=last),
        grid=(m // tm,),
        in_specs=[
            pl.BlockSpec((tm, d), lambda i: (i, 0)),
            pl.BlockSpec((tm, N_BRANCH * w), lambda i: (i, 0)),
            pl.BlockSpec((tm, gate_block), lambda i: (i, gate_idx)),
            per_layer((N_BRANCH, w, d)), per_layer((d, d)), per_layer((1, d)), per_layer((d, d_ff)),
            per_layer((d_ff, d)),
            pl.BlockSpec((1, d), lambda i: (0, 0), pipeline_mode=pl.Buffered(1)),
        ],
        out_specs=pl.BlockSpec((tm, d), lambda i: (i, 0)),
        out_shape=jax.ShapeDtypeStruct((m, d), F32),
        compiler_params=pltpu.CompilerParams(
            dimension_semantics=("parallel",), vmem_limit_bytes=VMEM_LIMIT_BYTES),
        name="merge_ffn",
    )(x2d, ys2d, gate_src, w_branch, w_out, norm2_w, w_up, w_down, final_w)


def _split_w_in(w_in):
    w, nh = MIX_WIDTH, N_HEADS
    a0, a1 = 4 * w, 4 * w + 2 * nh
    b0, b1 = a1 + 4 * w, a1 + 4 * w + 2 * nh
    w_big = tuple(part.astype(BF16) for part in (w_in[:, :, :a0], w_in[:, :, a1:b0], w_in[:, :, b1:]))
    small = jnp.concatenate([w_in[:, :, a0:a1], w_in[:, :, b0:b1]], axis=2)
    w_small = jnp.pad(small, ((0, 0), (0, 0), (0, SMALL_COLS - 4 * nh))).astype(BF16)
    return w_big, w_small


def _pad_lanes(parts, total):
    row = jnp.concatenate([p.reshape(1, -1).astype(F32) for p in parts], axis=1)
    return jnp.pad(row, ((0, 0), (0, total - row.shape[1])))


def _tiles(m):
    return math.gcd(m, 256), math.gcd(m, 256)


def _run_group(x3, states_in, weights, mix_params, seqs_per_step):
    b, t, d = x3.shape
    depth = len(mix_params)
    chunk = CHUNK if t % CHUNK == 0 else BF16_ROWS
    assert t % CHUNK == 0 or t <= BF16_ROWS
    x2d = x3.reshape(b * t, d)
    tm, tm_merge = _tiles(b * t)
    states_prev = None
    fuse_projection = t % CHUNK == 0
    for l in range(depth):
        big = small = proj_inputs = None
        if fuse_projection:
            proj_inputs = (x2d, weights["norm1_w"], weights["w_big"], weights["w_small"])
        else:
            big, small = _in_proj(x2d, weights["norm1_w"], weights["w_big"], weights["w_small"], layer=l, tm=tm)
        ys, states_prev, gates = _mixers(
            big, small, mix_params[l], states_in, states_prev, proj_inputs,
            b=b, t=t, layer=l, depth=depth, chunk=chunk, bb=math.gcd(b, seqs_per_step))
        x2d = _merge_ffn(x2d, ys, gates if fuse_projection else big, weights["w_branch"], weights["w_out"],
                         weights["norm2_w"], weights["w_up"], weights["w_down"], weights["final_norm_w"],
                         layer=l, tm=tm_merge, last=(l == depth - 1))
    return x2d.reshape(b, t, d), states_prev


def kernel(x_prompt, x_sample, state_mlstm_C, state_mlstm_n, state_mlstm_m, state_gdn_S, state_gdn_conv, state_hgrn_S, norm1_w, w_in, ml_i_bias, ml_f_bias, ml_norm_w, gd_conv_w, gd_A_log, gd_dt_bias, gd_norm_w, hg_lb_logits, hg_norm_w, w_branch, w_out, norm2_w, w_up, w_down, final_norm_w):
    depth, d = norm1_w.shape
    nh = N_HEADS
    w_big, w_small = _split_w_in(w_in)
    weights = dict(
        norm1_w=norm1_w.reshape(depth, 1, d), w_big=w_big, w_small=w_small,
        w_branch=w_branch.astype(BF16), w_out=w_out.astype(BF16), norm2_w=norm2_w.reshape(depth, 1, d),
        w_up=w_up.astype(BF16), w_down=w_down.astype(BF16), final_norm_w=final_norm_w.reshape(1, d))
    mix_params = []
    for l in range(depth):
        zeros_h = jnp.zeros((nh,), F32)
        bias_row = _pad_lanes([ml_i_bias[l], ml_f_bias[l], zeros_h, gd_dt_bias[l]], SMALL_COLS)
        alog_row = _pad_lanes([zeros_h, zeros_h, zeros_h, gd_A_log[l]], SMALL_COLS)
        mix_params.append((bias_row, alog_row, ml_norm_w[l].reshape(1, -1), gd_conv_w[l],
                           gd_norm_w[l].reshape(1, -1), hg_lb_logits, hg_norm_w[l].reshape(1, -1)))

    def unpack(states):
        s_c, s_n, s_m, s_g, s_cv, s_h = states
        return s_c, s_n.reshape(s_n.shape[:3] + s_n.shape[4:]), s_m[:, :, 0, :nh], s_g, s_cv, s_h

    y_prompt, p_states = _run_group(x_prompt, None, weights, mix_params, PROMPT_SEQS_PER_STEP)

    bs = x_sample.shape[0]
    m_in = jnp.pad(state_mlstm_m, ((0, 0), (0, 0), (0, SMALL_COLS - nh))).reshape(depth, bs, 1, SMALL_COLS)
    n_in = state_mlstm_n.reshape(depth, bs, nh, 1, HEAD_DIM)
    s_in = (state_mlstm_C, n_in, m_in, state_gdn_S, state_gdn_conv, state_hgrn_S)
    y_sample, s_states = _run_group(x_sample, s_in, weights, mix_params, SAMPLE_SEQS_PER_STEP)

    return (y_prompt, y_sample) + unpack(p_states) + unpack(s_states)
```

```python
import functools
import math

import jax
import jax.numpy as jnp
from jax import lax
from jax.experimental import pallas as pl
from jax.experimental.pallas import tpu as pltpu

F32 = jnp.float32
BF16 = jnp.bfloat16

HEAD_DIM = 128
N_HEADS = 4
MIX_WIDTH = N_HEADS * HEAD_DIM
N_BRANCH = 3
CONV_W = 4
EPS = 1e-6
CHUNK = 128
SUBLANES = 8
BF16_ROWS = 16
SMALL_COLS = 128
NEG = -1e30
VMEM_LIMIT_BYTES = 56 * 1024 * 1024
PROJ_TN = 1024
PROMPT_SEQS_PER_STEP = 2
SAMPLE_SEQS_PER_STEP = 8
STAGES_PER_ROUND = (1, 4, 2)
GATE_STAGES_PER_ROUND = 3

SEG_ML_Q, SEG_ML_K, SEG_ML_V, SEG_ML_O = 0, 1, 2, 3
SEG_GD_Q, SEG_GD_K, SEG_GD_V, SEG_GD_Z = 4, 5, 6, 7
SEG_HG_Q, SEG_HG_F, SEG_HG_I, SEG_HG_G = 8, 9, 10, 11
N_MIX_SEGS = 12
SM_I, SM_F, SM_B, SM_A = 0, N_HEADS, 2 * N_HEADS, 3 * N_HEADS


def _dot(a, b):
    return jnp.dot(a.astype(BF16), b.astype(BF16), preferred_element_type=F32)


def _dot_nt(a, b):
    return lax.dot_general(a.astype(BF16), b.astype(BF16), (((1,), (1,)), ((), ())),
                           preferred_element_type=F32)


def _dot_tn(a, b):
    return _dot(a.T, b)


def _f32_dot(a, b):
    return jnp.dot(a, b, preferred_element_type=F32)


def _split3(a):
    hi = a.astype(BF16)
    r = a - hi.astype(F32)
    mid = r.astype(BF16)
    lo = (r - mid.astype(F32)).astype(BF16)
    return hi, mid, lo


def _dot_exact_lhs(a01, b):
    a = a01.astype(BF16)
    return sum(_f32_dot(a, part) for part in _split3(b))


def _dot_exact_rhs(a, b01):
    b = b01.astype(BF16)
    return sum(_f32_dot(part, b) for part in _split3(a))


def _sigmoid(x):
    return 1.0 / (1.0 + jnp.exp(-x))


def _silu(x):
    return x * _sigmoid(x)


def _softplus(x):
    return jnp.maximum(x, 0.0) + jnp.log(1.0 + jnp.exp(-jnp.abs(x)))


def _log_sigmoid(x):
    return -_softplus(-x)


def _rms(x, axis=-1):
    return x * lax.rsqrt(jnp.mean(x * x, axis=axis, keepdims=True) + EPS)


def _in_proj_kernel(x_ref, nw_ref, *refs):
    *wbig_refs, wsmall_ref, big_ref, small_ref = refs
    xn = (_rms(x_ref[...]) * nw_ref[...]).astype(BF16)
    small_ref[...] = jnp.dot(xn, wsmall_ref[...], preferred_element_type=F32)
    out0 = 0
    for wbig_ref in wbig_refs:
        for n0 in range(0, wbig_ref.shape[1], PROJ_TN):
            big_ref[:, out0 + n0:out0 + n0 + PROJ_TN] = jnp.dot(
                xn, wbig_ref[:, n0:n0 + PROJ_TN], preferred_element_type=F32)
        out0 += wbig_ref.shape[1]


def _in_proj(x2d, norm_w, w_big, w_small, *, layer, tm):
    m, d = x2d.shape
    n_big = sum(wg.shape[2] for wg in w_big)
    assert m % tm == 0 and all(wg.shape[2] % PROJ_TN == 0 for wg in w_big)
    resident = dict(pipeline_mode=pl.Buffered(1))
    return pl.pallas_call(
        _in_proj_kernel,
        grid=(m // tm,),
        in_specs=[
            pl.BlockSpec((tm, d), lambda i: (i, 0)),
            pl.BlockSpec((None, 1, d), lambda i: (layer, 0, 0), **resident),
            *[pl.BlockSpec((None, d, wg.shape[2]), lambda i: (layer, 0, 0), **resident) for wg in w_big],
            pl.BlockSpec((None, d, SMALL_COLS), lambda i: (layer, 0, 0), **resident),
        ],
        out_specs=[
            pl.BlockSpec((tm, n_big), lambda i: (i, 0)),
            pl.BlockSpec((tm, SMALL_COLS), lambda i: (i, 0)),
        ],
        out_shape=[
            jax.ShapeDtypeStruct((m, n_big), F32),
            jax.ShapeDtypeStruct((m, SMALL_COLS), F32),
        ],
        compiler_params=pltpu.CompilerParams(
            dimension_semantics=("parallel",), vmem_limit_bytes=VMEM_LIMIT_BYTES),
        name="in_proj",
    )(x2d, norm_w, *w_big, w_small)


def _heads(fn, *xs):
    return jnp.stack([fn(*(x[h] for x in xs)) for h in range(xs[0].shape[0])])


def _interleave(stage_generators, stages_per_round):
    results = [None] * len(stage_generators)
    live = list(enumerate(stage_generators))
    while live:
        still = []
        for idx, gen in live:
            try:
                for _ in range(stages_per_round[idx]):
                    next(gen)
                still.append((idx, gen))
            except StopIteration as stop:
                results[idx] = stop.value
        live = still
    return results


def _inv_unit_lower(n, level_masks):
    c = n.shape[-1]
    eye = (lax.broadcasted_iota(jnp.int32, (c, c), 0) == lax.broadcasted_iota(jnp.int32, (c, c), 1)).astype(F32)
    n_bf = n.astype(BF16)
    zero = jnp.zeros((), BF16)
    d = eye - jnp.where(level_masks[0], n, 0.0)
    for mask in level_masks[1:]:
        c_bf = jnp.where(mask, n_bf, zero)
        d_bf = d.astype(BF16)
        t = _heads(_f32_dot, d_bf, c_bf)
        yield
        d = d - _heads(_f32_dot, t.astype(BF16), d_bf)
        yield
    return d


def _mlstm_chunk(q, k, v, qk, qc, i_col, b_col, i_row, b_row, causal, c_state, n_state, m_state):
    c = q.shape[1]
    a_col = b_col + m_state
    d = jnp.where(causal, b_col - b_row + i_row, NEG)
    mt = jnp.maximum(a_col, jnp.max(d, axis=2, keepdims=True))
    p = qk * jnp.exp(d - mt)
    w_st = jnp.exp(a_col - mt)
    yield
    num = _heads(_dot, p, v) + w_st * qc
    yield
    den = jnp.sum(p, axis=2, keepdims=True) + w_st * jnp.sum(q * n_state, axis=2, keepdims=True)
    h = num / jnp.maximum(jnp.abs(den), jnp.exp(-mt))
    m_new = mt[:, c - 1:c, :]
    b_last = b_col[:, c - 1:c, :]
    ws_col = jnp.exp(b_last - b_col + i_col - m_new)
    wc = jnp.exp(b_last + m_state - m_new)
    kw = k * ws_col
    yield
    c_new = wc * c_state + _heads(_dot_tn, kw, v)
    n_new = wc * n_state + jnp.sum(kw, axis=1, keepdims=True)
    return h, c_new, n_new, m_new


def _gdn_chunk(q, k, v, qk_kk, beta_col, g_col, g_row, incl, strict, level_masks, s_state):
    c = q.shape[1]
    dh = v.shape[2]
    decay = jnp.exp(jnp.where(incl, g_col - g_row, NEG))
    qk = qk_kk[:, :c]
    kk = qk_kk[:, c:]
    n = jnp.where(strict, beta_col * kk * decay, 0.0)
    yield
    a_inv = yield from _inv_unit_lower(n, level_masks)
    eg_col = jnp.exp(g_col)
    rhs = jnp.concatenate([beta_col * v, (beta_col * eg_col) * k], axis=2)
    yield
    sol = _heads(_dot, a_inv, rhs)
    u = sol[:, :, :dh]
    w = sol[:, :, dh:]
    yield
    wq_s = _heads(_dot, jnp.concatenate([w, q * eg_col], axis=1), s_state)
    v_new = u - wq_s[:, :c]
    yield
    o = wq_s[:, c:] + _heads(_dot, qk * decay, v_new)
    g_last = g_col[:, c - 1:c, :]
    yield
    s_new = jnp.exp(g_last) * s_state + _heads(_dot_tn, k * jnp.exp(g_last - g_col), v_new)
    return o, s_new


def _row_block_bcast(x, block, row):
    nh, c, l = x.shape
    x3 = x.reshape(nh * c // block, block, l)
    return jnp.broadcast_to(x3[:, row:row + 1, :], x3.shape).reshape(nh, c, l)


def _boundary_rows(x, hb):
    if 2 * hb >= SUBLANES:
        return _row_block_bcast(x, 2 * hb, hb - 1)
    sub = lax.broadcasted_iota(jnp.int32, (1, x.shape[1], 1), 1) % SUBLANES
    out = _row_block_bcast(x, SUBLANES, hb - 1)
    for start in range(2 * hb, SUBLANES, 2 * hb):
        out = jnp.where(sub >= start, _row_block_bcast(x, SUBLANES, start + hb - 1), out)
    return out


def _half_block_sizes(c):
    assert c & (c - 1) == 0
    return [1 << e for e in range(c.bit_length() - 1)]


def _level_masks(c):
    t = lax.broadcasted_iota(jnp.int32, (c, c), 0)
    s = lax.broadcasted_iota(jnp.int32, (c, c), 1)
    masks = []
    for hb in _half_block_sizes(c):
        same = (t // (2 * hb)) == (s // (2 * hb))
        masks.append(same & ((t % (2 * hb)) >= hb) & ((s % (2 * hb)) < hb))
    return masks


def _hgrn_chunk(q, k, v, bc, level_masks, s_state):
    nh, c, dh = q.shape
    diag = lax.broadcasted_iota(jnp.int32, (c, c), 0) == lax.broadcasted_iota(jnp.int32, (c, c), 1)
    o = _heads(_dot, q * jnp.exp(bc), s_state)
    yield

    a = jnp.where(diag, jnp.sum(q * k, axis=2, keepdims=True), 0.0)
    for hb, mask in zip(_half_block_sizes(c), level_masks):
        e = jnp.exp(-jnp.abs(bc - _boundary_rows(bc, hb)))
        a = a + jnp.where(mask, _heads(_dot_nt, q * e, k * e), 0.0)
        yield
    o = o + _heads(_dot, a, v)
    yield

    b_last = bc[:, c - 1:c, :]
    decay_col = _heads(lambda r: jnp.broadcast_to(r, (SUBLANES, dh)).T[:, 0:1], jnp.exp(b_last))
    s_new = decay_col * s_state + _heads(_dot_tn, k * jnp.exp(b_last - bc), v)
    return o, s_new


def _mixers_kernel(*refs, layer, chunk, n_valid, bb, has_state, fused):
    gates_ref = None
    if fused:
        x_ref, nw_ref, wa_ref, wb_ref, wc_ref, wsmall_ref, *refs = refs
        *refs, gates_ref, convbuf_last = refs
        refs = [None, None, *refs, convbuf_last]
    if has_state:
        (big_ref, small_ref, bias_ref, alog_ref, mlnw_ref, convw_ref, gdnw_ref, lbl_ref, hgnw_ref,
         c0_ref, n0_ref, m0_ref, sg0_ref, cv0_ref, sh0_ref,
         ys_ref, c_ref, n_ref, m_ref, sg_ref, cv_ref, sh_ref, convbuf) = refs
    else:
        (big_ref, small_ref, bias_ref, alog_ref, mlnw_ref, convw_ref, gdnw_ref, lbl_ref, hgnw_ref,
         ys_ref, c_ref, n_ref, m_ref, sg_ref, cv_ref, sh_ref, convbuf) = refs
    j = pl.program_id(1)
    nj = pl.num_programs(1)
    c = chunk
    flat = not fused and len(big_ref.shape) == 2
    w = MIX_WIDTH
    dh = HEAD_DIM
    nh = N_HEADS
    tail = CONV_W - 1

    @pl.when(j == 0)
    def _():
        if has_state:
            c_ref[...] = c0_ref[...]
            n_ref[...] = n0_ref[...]
            m_ref[...] = m0_ref[...]
            sg_ref[...] = sg0_ref[...]
            sh_ref[...] = sh0_ref[...]
            convbuf[:, SUBLANES - tail:SUBLANES, :] = cv0_ref[...]
        else:
            c_ref[...] = jnp.zeros_like(c_ref)
            n_ref[...] = jnp.zeros_like(n_ref)
            m_ref[...] = jnp.zeros_like(m_ref)
            sg_ref[...] = jnp.zeros_like(sg_ref)
            sh_ref[...] = jnp.zeros_like(sh_ref)
            convbuf[:, SUBLANES - tail:SUBLANES, :] = jnp.zeros((bb, tail, N_BRANCH * w), F32)

    if fused:
        xn = (_rms(x_ref[...].reshape(bb * c, x_ref.shape[2])) * nw_ref[...]).astype(BF16)

        def project(w_ref, lo, hi):
            return jnp.dot(xn, w_ref[:, lo:hi], preferred_element_type=F32).reshape(bb, c, hi - lo)

        small_proj = project(wsmall_ref, 0, SMALL_COLS)
        big_proj = []
        for w_ref in (wa_ref, wb_ref, wc_ref):
            for n0 in range(0, 4 * w, PROJ_TN):
                cols = project(w_ref, n0, n0 + PROJ_TN)
                big_proj += [cols[:, :, k0:k0 + w] for k0 in range(0, PROJ_TN, w)]

        def gate_projection():
            n_gate = wc_ref.shape[1] - 4 * w
            for n0 in range(0, n_gate, PROJ_TN):
                gates_ref[:, :, n0:n0 + PROJ_TN] = project(wc_ref, 4 * w + n0, 4 * w + n0 + PROJ_TN)
                yield

    def load_rows(ref, lo, hi):
        if not flat:
            return ref[:, :, lo:hi]
        x = ref[:, lo:hi]
        zeros = jnp.zeros((c - n_valid, hi - lo), x.dtype)
        return jnp.stack([jnp.concatenate([x[b * n_valid:(b + 1) * n_valid], zeros], axis=0) for b in range(bb)])

    def load_wide(lo, hi):
        if fused:
            return jnp.concatenate(big_proj[lo // w:hi // w], axis=2) if hi - lo > w else big_proj[lo // w]
        return load_rows(big_ref, lo, hi)

    def store(col0, y):
        if not flat:
            ys_ref[:, :, col0:col0 + y.shape[2]] = y.astype(ys_ref.dtype)
        else:
            rows_out = jnp.concatenate([y[b, :n_valid] for b in range(bb)], axis=0)
            ys_ref[:, col0:col0 + y.shape[2]] = rows_out.astype(ys_ref.dtype)

    def seg(i):
        return load_wide(i * w, (i + 1) * w)

    def per_seq(fn, x):
        return jnp.stack([fn(x[b]) for b in range(bb)])

    def head_major(x):
        return jnp.stack([x[b][:, h * dh:(h + 1) * dh] for b in range(bb) for h in range(nh)])

    def token_major(x):
        return jnp.stack([jnp.concatenate([x[b * nh + h] for h in range(nh)], axis=1) for b in range(bb)])

    def cols(x, first):
        return jnp.stack([x[b][:, first + h:first + h + 1] for b in range(bb) for h in range(nh)])

    def rows(x, first):
        return jnp.stack([x[b][first + h:first + h + 1, :] for b in range(bb) for h in range(nh)])

    lane = lax.broadcasted_iota(jnp.int32, (1, c, SMALL_COLS), 2)
    rowi = lax.broadcasted_iota(jnp.int32, (1, c, SMALL_COLS), 1)
    pre = (small_proj if fused else load_rows(small_ref, 0, SMALL_COLS)) + bias_ref[...]
    is_f = (lane >= SM_F) & (lane < SM_B)
    is_b = (lane >= SM_B) & (lane < SM_A)
    is_a = (lane >= SM_A) & (lane < SM_A + nh)
    logg = -jnp.exp(alog_ref[...]) * _softplus(pre)
    gates = jnp.where(is_f, _log_sigmoid(pre), jnp.where(is_b, _sigmoid(pre), jnp.where(is_a, logg, pre)))
    gates = jnp.where(rowi < n_valid, gates, jnp.where(lane < SM_F, NEG, 0.0))
    tt = lax.broadcasted_iota(jnp.int32, (c, c), 0)
    ss = lax.broadcasted_iota(jnp.int32, (c, c), 1)
    incl = tt >= ss
    strict = tt > ss
    ltri = incl.astype(F32)
    utri = (ss >= tt).astype(F32)
    cum_src = jnp.where(is_f | is_a, gates, 0.0)
    cums = per_seq(lambda x: _dot_exact_lhs(ltri, x), cum_src)
    gates_t = per_seq(lambda x: x.T, gates)
    cums_t = per_seq(lambda x: _dot_exact_rhs(x.T, utri), cum_src)

    lbl = lbl_ref[...]
    sm = jnp.exp(lbl - jnp.max(lbl, axis=0, keepdims=True))
    sm = sm / jnp.sum(sm, axis=0, keepdims=True)
    lb = jnp.zeros((1, w), F32)
    for l in range(1, layer + 1):
        lb = lb + sm[l:l + 1, :]
    valid = lax.broadcasted_iota(jnp.int32, (1, c, w), 1) < n_valid
    hf = seg(SEG_HG_F)
    g_f = lb + (1.0 - lb) * _sigmoid(hf)
    k_hg = jnp.where(valid, (1.0 - lb) * _sigmoid(-hf), 0.0)
    f_log = jnp.where(valid, jnp.log(g_f), 0.0)
    bc = per_seq(lambda x: _dot_exact_lhs(ltri, x), f_log)

    convbuf[:, SUBLANES:SUBLANES + c, :] = load_wide(SEG_GD_Q * w, (SEG_GD_V + 1) * w)
    conv = convw_ref[CONV_W - 1:CONV_W, :] * convbuf[:, SUBLANES:SUBLANES + c, :]
    for jj in range(CONV_W - 1):
        off = SUBLANES - tail + jj
        conv = conv + convw_ref[jj:jj + 1, :] * convbuf[:, off:off + c, :]
    conv = _silu(conv)
    new_tail = convbuf[:, SUBLANES + n_valid - tail:SUBLANES + n_valid, :]
    convbuf[:, SUBLANES - tail:SUBLANES, :] = new_tail
    cq = head_major(conv[:, :, 0:w])
    ck = head_major(conv[:, :, w:2 * w])
    cv = head_major(conv[:, :, 2 * w:3 * w])
    qn = cq * lax.rsqrt(jnp.sum(cq * cq, axis=2, keepdims=True) + EPS) * (dh ** -0.5)
    kn = ck * lax.rsqrt(jnp.sum(ck * ck, axis=2, keepdims=True) + EPS)
    qk_kk = _heads(_dot_nt, jnp.concatenate([qn, kn], axis=1), kn)

    ml_q = head_major(seg(SEG_ML_Q))
    ml_k = head_major(seg(SEG_ML_K)) * (dh ** -0.5)
    c_old = c_ref[...].reshape(bb * nh, dh, dh)
    ml_qk = _heads(_dot_nt, ml_q, ml_k)
    ml_qc = _heads(_dot, ml_q, c_old)
    m_old = jnp.stack([m_ref[b][:, h:h + 1] for b in range(bb) for h in range(nh)])

    mlstm = _mlstm_chunk(
        ml_q, ml_k, head_major(seg(SEG_ML_V)), ml_qk, ml_qc,
        cols(gates, SM_I), cols(cums, SM_F), rows(gates_t, SM_I), rows(cums_t, SM_F), incl,
        c_old, n_ref[...].reshape(bb * nh, 1, dh), m_old)
    level_masks = _level_masks(c)
    gdn = _gdn_chunk(qn, kn, cv, qk_kk, cols(gates, SM_B), cols(cums, SM_A), rows(cums_t, SM_A),
                     incl, strict, level_masks, sg_ref[...].reshape(bb * nh, dh, dh))
    hgrn = _hgrn_chunk(head_major(_silu(seg(SEG_HG_Q))), head_major(k_hg), head_major(seg(SEG_HG_I)),
                       head_major(bc), level_masks, sh_ref[...].reshape(bb * nh, dh, dh))
    streams, per_round = (mlstm, gdn, hgrn), STAGES_PER_ROUND
    if fused:
        streams, per_round = streams + (gate_projection(),), per_round + (GATE_STAGES_PER_ROUND,)
    (hh, c_new, n_new, m_new), (o_gd, sg_new), (o_hg, sh_new) = _interleave(streams, per_round)[:3]

    c_ref[...] = c_new.reshape(c_ref.shape)
    n_ref[...] = n_new.reshape(n_ref.shape)
    m_lane = lax.broadcasted_iota(jnp.int32, (1, SMALL_COLS), 1)
    m_rows = []
    for b in range(bb):
        m_row = jnp.zeros((1, SMALL_COLS), F32)
        for h in range(nh):
            m_row = jnp.where(m_lane == h, m_new[b * nh + h], m_row)
        m_rows.append(m_row)
    m_ref[...] = jnp.stack(m_rows)
    y_ml = token_major(_rms(hh)) * mlnw_ref[...] * _sigmoid(seg(SEG_ML_O))
    store(0, y_ml)

    sg_ref[...] = sg_new.reshape(sg_ref.shape)
    y_gd = token_major(_rms(o_gd) * gdnw_ref[...]) * _silu(seg(SEG_GD_Z))
    store(w, y_gd)

    sh_ref[...] = sh_new.reshape(sh_ref.shape)
    y_hg = _rms(token_major(o_hg)) * hgnw_ref[...] * _silu(seg(SEG_HG_G))
    store(2 * w, y_hg)

    @pl.when(j == nj - 1)
    def _():
        cv_ref[...] = convbuf[:, SUBLANES - tail:SUBLANES, :]


def _mixers(big2, small2, params, states_in, states_prev, proj_inputs, *, b, t, layer, depth, chunk, bb):
    rows = min(t, chunk)
    assert t % rows == 0 and chunk % BF16_ROWS == 0 and CONV_W - 1 <= rows and b % bb == 0
    fused = proj_inputs is not None
    assert fused or big2.shape[0] == b * t
    w, dh, nh = MIX_WIDTH, HEAD_DIM, N_HEADS
    has_state = states_in is not None
    bias_row, alog_row, ml_norm_w, conv_w, gd_norm_w, lb_logits, hg_norm_w = params

    def const(shape):
        return pl.BlockSpec(shape, lambda i, j: (0,) * len(shape))

    def st(shape):
        return pl.BlockSpec((None, bb) + shape, lambda i, j: (layer, i) + (0,) * len(shape))

    flat = rows < chunk

    def view_shape(l):
        return (b * t, l) if flat else (b, t, l)

    def view(x):
        return x.reshape(view_shape(x.shape[-1]))

    def rows_spec(l):
        if flat:
            return pl.BlockSpec((bb * rows, l), lambda i, j: (i, 0))
        return pl.BlockSpec((bb, rows, l), lambda i, j: (i, j, 0))

    state_shapes = [(nh, dh, dh), (nh, 1, dh), (1, SMALL_COLS), (nh, dh, dh), (CONV_W - 1, N_BRANCH * w), (nh, dh, dh)]
    if fused:
        assert not flat
        x2d, norm_w, w_big, w_small = proj_inputs
        d = x2d.shape[1]
        assert [wg.shape[2] for wg in w_big[:2]] == [4 * w, 4 * w] and w_big[2].shape[2] > 4 * w
        resident = dict(pipeline_mode=pl.Buffered(1))
        in_specs = [rows_spec(d), pl.BlockSpec((None, 1, d), lambda i, j: (layer, 0, 0), **resident)]
        in_specs += [pl.BlockSpec((None, d, wg.shape[2]), lambda i, j: (layer, 0, 0), **resident)
                     for wg in (*w_big, w_small)]
        args = [view(x2d), norm_w, *w_big, w_small]
        n_gate = w_big[2].shape[2] - 4 * w
    else:
        in_specs = [rows_spec(N_MIX_SEGS * w), rows_spec(SMALL_COLS)]
        args = [view(big2), view(small2)]
    in_specs += [
        const((1, SMALL_COLS)), const((1, SMALL_COLS)), const((1, w)), const((CONV_W, N_BRANCH * w)),
        const((1, dh)), const((depth, w)), const((1, w)),
    ]
    args += [bias_row, alog_row, ml_norm_w, conv_w, gd_norm_w, lb_logits, hg_norm_w]
    if has_state:
        in_specs += [st(s) for s in state_shapes]
        args += list(states_in)
    aliases = {}
    if states_prev is not None:
        for k_out, arr in enumerate(states_prev):
            aliases[len(args)] = 1 + k_out
            in_specs.append(pl.BlockSpec(memory_space=pl.ANY))
            args.append(arr)
    out_specs = [rows_spec(N_BRANCH * w)] + [st(s) for s in state_shapes]
    out_shape = [jax.ShapeDtypeStruct(view_shape(N_BRANCH * w), BF16)] + [
        jax.ShapeDtypeStruct((depth, b) + s, F32) for s in state_shapes]
    if fused:
        out_specs.append(rows_spec(n_gate))
        out_shape.append(jax.ShapeDtypeStruct(view_shape(n_gate), F32))

    def body(*refs):
        n_in = len(args) - (len(states_prev) if states_prev is not None else 0)
        kept = refs[:n_in] + refs[len(args):]
        _mixers_kernel(*kept, layer=layer, chunk=chunk, n_valid=rows, bb=bb, has_state=has_state, fused=fused)

    outs = pl.pallas_call(
        body,
        grid=(b // bb, t // rows),
        in_specs=in_specs,
        out_specs=out_specs,
        out_shape=out_shape,
        scratch_shapes=[pltpu.VMEM((bb, SUBLANES + chunk, N_BRANCH * w), F32)],
        input_output_aliases=aliases,
        compiler_params=pltpu.CompilerParams(
            dimension_semantics=("parallel", "arbitrary"), vmem_limit_bytes=VMEM_LIMIT_BYTES),
        name="mixers",
    )(*args)
    gates = outs[-1].reshape(b * t, -1) if fused else None
    return outs[0].reshape(b * t, N_BRANCH * w), tuple(outs[1:1 + len(state_shapes)]), gates


def _merge_ffn_kernel(x_ref, ys_ref, gate_ref, wbr_ref, wout_ref, n2_ref, wup_ref, wdown_ref, fin_ref,
                      out_ref, *, last):
    w = MIX_WIDTH
    d = x_ref.shape[1]
    merged = jnp.zeros(x_ref.shape, F32)
    for n in range(N_BRANCH):
        br = jnp.dot(ys_ref[:, n * w:(n + 1) * w], wbr_ref[n], preferred_element_type=F32)
        merged = merged + _sigmoid(gate_ref[:, n * d:(n + 1) * d]) * br
    x1 = x_ref[...] + jnp.dot(merged.astype(BF16), wout_ref[...], preferred_element_type=F32)
    xn = (_rms(x1) * n2_ref[...]).astype(BF16)
    hid = jnp.square(jnp.maximum(jnp.dot(xn, wup_ref[...], preferred_element_type=F32), 0.0))
    x2 = x1 + jnp.dot(hid.astype(BF16), wdown_ref[...], preferred_element_type=F32)
    if last:
        x2 = _rms(x2) * fin_ref[...]
    out_ref[...] = x2


def _merge_ffn(x2d, ys2d, gate_src, w_branch, w_out, norm2_w, w_up, w_down, final_w, *, layer, tm, last):
    m, d = x2d.shape
    w = MIX_WIDTH
    d_ff = w_up.shape[2]
    gate_block = N_BRANCH * d
    assert m % tm == 0 and gate_src.shape[1] % gate_block == 0
    gate_idx = gate_src.shape[1] // gate_block - 1

    def per_layer(shape):
        return pl.BlockSpec((None,) + shape, lambda i: (layer,) + (0,) * len(shape), pipeline_mode=pl.Buffered(1))

    return pl.pallas_call(
        functools.partial(_merge_ffn_kernel, last=last),
        grid=(m // tm,),
        in_specs=[
            pl.BlockSpec((tm, d), lambda i: (i, 0)),
            pl.BlockSpec((tm, N_BRANCH * w), lambda i: (i, 0)),
            pl.BlockSpec((tm, gate_block), lambda i: (i, gate_idx)),
            per_layer((N_BRANCH, w, d)), per_layer((d, d)), per_layer((1, d)), per_layer((d, d_ff)),
            per_layer((d_ff, d)),
            pl.BlockSpec((1, d), lambda i: (0, 0), pipeline_mode=pl.Buffered(1)),
        ],
        out_specs=pl.BlockSpec((tm, d), lambda i: (i, 0)),
        out_shape=jax.ShapeDtypeStruct((m, d), F32),
        compiler_params=pltpu.CompilerParams(
            dimension_semantics=("parallel",), vmem_limit_bytes=VMEM_LIMIT_BYTES),
        name="merge_ffn",
    )(x2d, ys2d, gate_src, w_branch, w_out, norm2_w, w_up, w_down, final_w)


def _split_w_in(w_in):
    w, nh = MIX_WIDTH, N_HEADS
    a0, a1 = 4 * w, 4 * w + 2 * nh
    b0, b1 = a1 + 4 * w, a1 + 4 * w + 2 * nh
    w_big = tuple(part.astype(BF16) for part in (w_in[:, :, :a0], w_in[:, :, a1:b0], w_in[:, :, b1:]))
    small = jnp.concatenate([w_in[:, :, a0:a1], w_in[:, :, b0:b1]], axis=2)
    w_small = jnp.pad(small, ((0, 0), (0, 0), (0, SMALL_COLS - 4 * nh))).astype(BF16)
    return w_big, w_small


def _pad_lanes(parts, total):
    row = jnp.concatenate([p.reshape(1, -1).astype(F32) for p in parts], axis=1)
    return jnp.pad(row, ((0, 0), (0, total - row.shape[1])))


def _tiles(m):
    return math.gcd(m, 256), math.gcd(m, 512)


def _run_group(x3, states_in, weights, mix_params, seqs_per_step):
    b, t, d = x3.shape
    depth = len(mix_params)
    chunk = CHUNK if t % CHUNK == 0 else BF16_ROWS
    assert t % CHUNK == 0 or t <= BF16_ROWS
    x2d = x3.reshape(b * t, d)
    tm, tm_merge = _tiles(b * t)
    states_prev = None
    fuse_projection = t % CHUNK == 0
    for l in range(depth):
        big = small = proj_inputs = None
        if fuse_projection:
            proj_inputs = (x2d, weights["norm1_w"], weights["w_big"], weights["w_small"])
        else:
            big, small = _in_proj(x2d, weights["norm1_w"], weights["w_big"], weights["w_small"], layer=l, tm=tm)
        ys, states_prev, gates = _mixers(
            big, small, mix_params[l], states_in, states_prev, proj_inputs,
            b=b, t=t, layer=l, depth=depth, chunk=chunk, bb=math.gcd(b, seqs_per_step))
        x2d = _merge_ffn(x2d, ys, gates if fuse_projection else big, weights["w_branch"], weights["w_out"],
                         weights["norm2_w"], weights["w_up"], weights["w_down"], weights["final_norm_w"],
                         layer=l, tm=tm_merge, last=(l == depth - 1))
    return x2d.reshape(b, t, d), states_prev


def kernel(x_prompt, x_sample, state_mlstm_C, state_mlstm_n, state_mlstm_m, state_gdn_S, state_gdn_conv, state_hgrn_S, norm1_w, w_in, ml_i_bias, ml_f_bias, ml_norm_w, gd_conv_w, gd_A_log, gd_dt_bias, gd_norm_w, hg_lb_logits, hg_norm_w, w_branch, w_out, norm2_w, w_up, w_down, final_norm_w):
    depth, d = norm1_w.shape
    nh = N_HEADS
    w_big, w_small = _split_w_in(w_in)
    weights = dict(
        norm1_w=norm1_w.reshape(depth, 1, d), w_big=w_big, w_small=w_small,
        w_branch=w_branch.astype(BF16), w_out=w_out.astype(BF16), norm2_w=norm2_w.reshape(depth, 1, d),
        w_up=w_up.astype(BF16), w_down=w_down.astype(BF16), final_norm_w=final_norm_w.reshape(1, d))
    mix_params = []
    for l in range(depth):
        zeros_h = jnp.zeros((nh,), F32)
        bias_row = _pad_lanes([ml_i_bias[l], ml_f_bias[l], zeros_h, gd_dt_bias[l]], SMALL_COLS)
        alog_row = _pad_lanes([zeros_h, zeros_h, zeros_h, gd_A_log[l]], SMALL_COLS)
        mix_params.append((bias_row, alog_row, ml_norm_w[l].reshape(1, -1), gd_conv_w[l],
                           gd_norm_w[l].reshape(1, -1), hg_lb_logits, hg_norm_w[l].reshape(1, -1)))

    def unpack(states):
        s_c, s_n, s_m, s_g, s_cv, s_h = states
        return s_c, s_n.reshape(s_n.shape[:3] + s_n.shape[4:]), s_m[:, :, 0, :nh], s_g, s_cv, s_h

    y_prompt, p_states = _run_group(x_prompt, None, weights, mix_params, PROMPT_SEQS_PER_STEP)

    bs = x_sample.shape[0]
    m_in = jnp.pad(state_mlstm_m, ((0, 0), (0, 0), (0, SMALL_COLS - nh))).reshape(depth, bs, 1, SMALL_COLS)
    n_in = state_mlstm_n.reshape(depth, bs, nh, 1, HEAD_DIM)
    s_in = (state_mlstm_C, n_in, m_in, state_gdn_S, state_gdn_conv, state_hgrn_S)
    y_sample, s_states = _run_group(x_sample, s_in, weights, mix_params, SAMPLE_SEQS_PER_STEP)

    return (y_prompt, y_sample) + unpack(p_states) + unpack(s_states)
```

```python
import functools
import math

import jax
import jax.numpy as jnp
from jax import lax
from jax.experimental import pallas as pl
from jax.experimental.pallas import tpu as pltpu

F32 = jnp.float32
BF16 = jnp.bfloat16

HEAD_DIM = 128
N_HEADS = 4
MIX_WIDTH = N_HEADS * HEAD_DIM
N_BRANCH = 3
CONV_W = 4
EPS = 1e-6
CHUNK = 128
SUBLANES = 8
BF16_ROWS = 16
SMALL_COLS = 128
NEG = -1e30
VMEM_LIMIT_BYTES = 56 * 1024 * 1024
PROJ_TN = 1024
FFN_CHUNK = 1024
PROMPT_SEQS_PER_STEP = 2
SAMPLE_SEQS_PER_STEP = 8
STAGES_PER_ROUND = (1, 4, 2)
GATE_STAGES_PER_ROUND = 3

SEG_ML_Q, SEG_ML_K, SEG_ML_V, SEG_ML_O = 0, 1, 2, 3
SEG_GD_Q, SEG_GD_K, SEG_GD_V, SEG_GD_Z = 4, 5, 6, 7
SEG_HG_Q, SEG_HG_F, SEG_HG_I, SEG_HG_G = 8, 9, 10, 11
N_MIX_SEGS = 12
SM_I, SM_F, SM_B, SM_A = 0, N_HEADS, 2 * N_HEADS, 3 * N_HEADS


def _dot(a, b):
    return jnp.dot(a.astype(BF16), b.astype(BF16), preferred_element_type=F32)


def _dot_nt(a, b):
    return lax.dot_general(a.astype(BF16), b.astype(BF16), (((1,), (1,)), ((), ())),
                           preferred_element_type=F32)


def _dot_tn(a, b):
    return _dot(a.T, b)


def _f32_dot(a, b):
    return jnp.dot(a, b, preferred_element_type=F32)


def _split3(a):
    hi = a.astype(BF16)
    r = a - hi.astype(F32)
    mid = r.astype(BF16)
    lo = (r - mid.astype(F32)).astype(BF16)
    return hi, mid, lo


def _dot_exact_lhs(a01, b):
    a = a01.astype(BF16)
    return sum(_f32_dot(a, part) for part in _split3(b))


def _dot_exact_rhs(a, b01):
    b = b01.astype(BF16)
    return sum(_f32_dot(part, b) for part in _split3(a))


def _sigmoid(x):
    return 1.0 / (1.0 + jnp.exp(-x))


def _silu(x):
    return x * _sigmoid(x)


def _softplus(x):
    return jnp.maximum(x, 0.0) + jnp.log(1.0 + jnp.exp(-jnp.abs(x)))


def _log_sigmoid(x):
    return -_softplus(-x)


def _rms(x, axis=-1):
    return x * lax.rsqrt(jnp.mean(x * x, axis=axis, keepdims=True) + EPS)


def _in_proj_kernel(x_ref, nw_ref, *refs):
    *wbig_refs, wsmall_ref, big_ref, small_ref = refs
    xn = (_rms(x_ref[...]) * nw_ref[...]).astype(BF16)
    small_ref[...] = jnp.dot(xn, wsmall_ref[...], preferred_element_type=F32)
    out0 = 0
    for wbig_ref in wbig_refs:
        for n0 in range(0, wbig_ref.shape[1], PROJ_TN):
            big_ref[:, out0 + n0:out0 + n0 + PROJ_TN] = jnp.dot(
                xn, wbig_ref[:, n0:n0 + PROJ_TN], preferred_element_type=F32)
        out0 += wbig_ref.shape[1]


def _in_proj(x2d, norm_w, w_big, w_small, *, layer, tm):
    m, d = x2d.shape
    n_big = sum(wg.shape[2] for wg in w_big)
    assert m % tm == 0 and all(wg.shape[2] % PROJ_TN == 0 for wg in w_big)
    resident = dict(pipeline_mode=pl.Buffered(1))
    return pl.pallas_call(
        _in_proj_kernel,
        grid=(m // tm,),
        in_specs=[
            pl.BlockSpec((tm, d), lambda i: (i, 0)),
            pl.BlockSpec((None, 1, d), lambda i: (layer, 0, 0), **resident),
            *[pl.BlockSpec((None, d, wg.shape[2]), lambda i: (layer, 0, 0), **resident) for wg in w_big],
            pl.BlockSpec((None, d, SMALL_COLS), lambda i: (layer, 0, 0), **resident),
        ],
        out_specs=[
            pl.BlockSpec((tm, n_big), lambda i: (i, 0)),
            pl.BlockSpec((tm, SMALL_COLS), lambda i: (i, 0)),
        ],
        out_shape=[
            jax.ShapeDtypeStruct((m, n_big), F32),
            jax.ShapeDtypeStruct((m, SMALL_COLS), F32),
        ],
        compiler_params=pltpu.CompilerParams(
            dimension_semantics=("parallel",), vmem_limit_bytes=VMEM_LIMIT_BYTES),
        name="in_proj",
    )(x2d, norm_w, *w_big, w_small)


def _heads(fn, *xs):
    return jnp.stack([fn(*(x[h] for x in xs)) for h in range(xs[0].shape[0])])


def _interleave(stage_generators, stages_per_round):
    results = [None] * len(stage_generators)
    live = list(enumerate(stage_generators))
    while live:
        still = []
        for idx, gen in live:
            try:
                for _ in range(stages_per_round[idx]):
                    next(gen)
                still.append((idx, gen))
            except StopIteration as stop:
                results[idx] = stop.value
        live = still
    return results


def _inv_unit_lower(n, level_masks):
    c = n.shape[-1]
    eye = (lax.broadcasted_iota(jnp.int32, (c, c), 0) == lax.broadcasted_iota(jnp.int32, (c, c), 1)).astype(F32)
    n_bf = n.astype(BF16)
    zero = jnp.zeros((), BF16)
    d = eye - jnp.where(level_masks[0], n, 0.0)
    for mask in level_masks[1:]:
        c_bf = jnp.where(mask, n_bf, zero)
        d_bf = d.astype(BF16)
        t = _heads(_f32_dot, d_bf, c_bf)
        yield
        d = d - _heads(_f32_dot, t.astype(BF16), d_bf)
        yield
    return d


def _mlstm_chunk(q, k, v, qk, qc, i_col, b_col, i_row, b_row, causal, c_state, n_state, m_state):
    c = q.shape[1]
    a_col = b_col + m_state
    d = jnp.where(causal, b_col - b_row + i_row, NEG)
    mt = jnp.maximum(a_col, jnp.max(d, axis=2, keepdims=True))
    p = qk * jnp.exp(d - mt)
    w_st = jnp.exp(a_col - mt)
    yield
    num = _heads(_dot, p, v) + w_st * qc
    yield
    den = jnp.sum(p, axis=2, keepdims=True) + w_st * jnp.sum(q * n_state, axis=2, keepdims=True)
    h = num / jnp.maximum(jnp.abs(den), jnp.exp(-mt))
    m_new = mt[:, c - 1:c, :]
    b_last = b_col[:, c - 1:c, :]
    ws_col = jnp.exp(b_last - b_col + i_col - m_new)
    wc = jnp.exp(b_last + m_state - m_new)
    kw = k * ws_col
    yield
    c_new = wc * c_state + _heads(_dot_tn, kw, v)
    n_new = wc * n_state + jnp.sum(kw, axis=1, keepdims=True)
    return h, c_new, n_new, m_new


def _gdn_chunk(q, k, v, qk_kk, beta_col, g_col, g_row, incl, strict, level_masks, s_state):
    c = q.shape[1]
    dh = v.shape[2]
    decay = jnp.exp(jnp.where(incl, g_col - g_row, NEG))
    qk = qk_kk[:, :c]
    kk = qk_kk[:, c:]
    n = jnp.where(strict, beta_col * kk * decay, 0.0)
    yield
    a_inv = yield from _inv_unit_lower(n, level_masks)
    eg_col = jnp.exp(g_col)
    rhs = jnp.concatenate([beta_col * v, (beta_col * eg_col) * k], axis=2)
    yield
    sol = _heads(_dot, a_inv, rhs)
    u = sol[:, :, :dh]
    w = sol[:, :, dh:]
    yield
    wq_s = _heads(_dot, jnp.concatenate([w, q * eg_col], axis=1), s_state)
    v_new = u - wq_s[:, :c]
    yield
    o = wq_s[:, c:] + _heads(_dot, qk * decay, v_new)
    g_last = g_col[:, c - 1:c, :]
    yield
    s_new = jnp.exp(g_last) * s_state + _heads(_dot_tn, k * jnp.exp(g_last - g_col), v_new)
    return o, s_new


def _row_block_bcast(x, block, row):
    nh, c, l = x.shape
    x3 = x.reshape(nh * c // block, block, l)
    return jnp.broadcast_to(x3[:, row:row + 1, :], x3.shape).reshape(nh, c, l)


def _boundary_rows(x, hb):
    if 2 * hb >= SUBLANES:
        return _row_block_bcast(x, 2 * hb, hb - 1)
    sub = lax.broadcasted_iota(jnp.int32, (1, x.shape[1], 1), 1) % SUBLANES
    out = _row_block_bcast(x, SUBLANES, hb - 1)
    for start in range(2 * hb, SUBLANES, 2 * hb):
        out = jnp.where(sub >= start, _row_block_bcast(x, SUBLANES, start + hb - 1), out)
    return out


def _half_block_sizes(c):
    assert c & (c - 1) == 0
    return [1 << e for e in range(c.bit_length() - 1)]


def _level_masks(c):
    t = lax.broadcasted_iota(jnp.int32, (c, c), 0)
    s = lax.broadcasted_iota(jnp.int32, (c, c), 1)
    masks = []
    for hb in _half_block_sizes(c):
        same = (t // (2 * hb)) == (s // (2 * hb))
        masks.append(same & ((t % (2 * hb)) >= hb) & ((s % (2 * hb)) < hb))
    return masks


def _hgrn_chunk(q, k, v, bc, level_masks, s_state):
    nh, c, dh = q.shape
    diag = lax.broadcasted_iota(jnp.int32, (c, c), 0) == lax.broadcasted_iota(jnp.int32, (c, c), 1)
    o = _heads(_dot, q * jnp.exp(bc), s_state)
    yield

    a = jnp.where(diag, jnp.sum(q * k, axis=2, keepdims=True), 0.0)
    for hb, mask in zip(_half_block_sizes(c), level_masks):
        e = jnp.exp(-jnp.abs(bc - _boundary_rows(bc, hb)))
        a = a + jnp.where(mask, _heads(_dot_nt, q * e, k * e), 0.0)
        yield
    o = o + _heads(_dot, a, v)
    yield

    b_last = bc[:, c - 1:c, :]
    decay_col = _heads(lambda r: jnp.broadcast_to(r, (SUBLANES, dh)).T[:, 0:1], jnp.exp(b_last))
    s_new = decay_col * s_state + _heads(_dot_tn, k * jnp.exp(b_last - bc), v)
    return o, s_new


def _mixers_kernel(*refs, layer, chunk, n_valid, bb, has_state, fused):
    gates_ref = None
    if fused:
        x_ref, nw_ref, wa_ref, wb_ref, wc_ref, wsmall_ref, *refs = refs
        *refs, gates_ref, convbuf_last = refs
        refs = [None, None, *refs, convbuf_last]
    if has_state:
        (big_ref, small_ref, bias_ref, alog_ref, mlnw_ref, convw_ref, gdnw_ref, lbl_ref, hgnw_ref,
         c0_ref, n0_ref, m0_ref, sg0_ref, cv0_ref, sh0_ref,
         ys_ref, c_ref, n_ref, m_ref, sg_ref, cv_ref, sh_ref, convbuf) = refs
    else:
        (big_ref, small_ref, bias_ref, alog_ref, mlnw_ref, convw_ref, gdnw_ref, lbl_ref, hgnw_ref,
         ys_ref, c_ref, n_ref, m_ref, sg_ref, cv_ref, sh_ref, convbuf) = refs
    j = pl.program_id(1)
    nj = pl.num_programs(1)
    c = chunk
    flat = not fused and len(big_ref.shape) == 2
    w = MIX_WIDTH
    dh = HEAD_DIM
    nh = N_HEADS
    tail = CONV_W - 1

    @pl.when(j == 0)
    def _():
        if has_state:
            c_ref[...] = c0_ref[...]
            n_ref[...] = n0_ref[...]
            m_ref[...] = m0_ref[...]
            sg_ref[...] = sg0_ref[...]
            sh_ref[...] = sh0_ref[...]
            convbuf[:, SUBLANES - tail:SUBLANES, :] = cv0_ref[...]
        else:
            c_ref[...] = jnp.zeros_like(c_ref)
            n_ref[...] = jnp.zeros_like(n_ref)
            m_ref[...] = jnp.zeros_like(m_ref)
            sg_ref[...] = jnp.zeros_like(sg_ref)
            sh_ref[...] = jnp.zeros_like(sh_ref)
            convbuf[:, SUBLANES - tail:SUBLANES, :] = jnp.zeros((bb, tail, N_BRANCH * w), F32)

    if fused:
        xn = (_rms(x_ref[...].reshape(bb * c, x_ref.shape[2])) * nw_ref[...]).astype(BF16)

        def project(w_ref, lo, hi):
            return jnp.dot(xn, w_ref[:, lo:hi], preferred_element_type=F32).reshape(bb, c, hi - lo)

        small_proj = project(wsmall_ref, 0, SMALL_COLS)
        big_proj = []
        for w_ref in (wa_ref, wb_ref, wc_ref):
            for n0 in range(0, 4 * w, PROJ_TN):
                cols = project(w_ref, n0, n0 + PROJ_TN)
                big_proj += [cols[:, :, k0:k0 + w] for k0 in range(0, PROJ_TN, w)]

        def gate_projection():
            n_gate = wc_ref.shape[1] - 4 * w
            for n0 in range(0, n_gate, PROJ_TN):
                gates_ref[:, :, n0:n0 + PROJ_TN] = project(wc_ref, 4 * w + n0, 4 * w + n0 + PROJ_TN)
                yield

    def load_rows(ref, lo, hi):
        if not flat:
            return ref[:, :, lo:hi]
        x = ref[:, lo:hi]
        zeros = jnp.zeros((c - n_valid, hi - lo), x.dtype)
        return jnp.stack([jnp.concatenate([x[b * n_valid:(b + 1) * n_valid], zeros], axis=0) for b in range(bb)])

    def load_wide(lo, hi):
        if fused:
            return jnp.concatenate(big_proj[lo // w:hi // w], axis=2) if hi - lo > w else big_proj[lo // w]
        return load_rows(big_ref, lo, hi)

    def store(col0, y):
        if not flat:
            ys_ref[:, :, col0:col0 + y.shape[2]] = y.astype(ys_ref.dtype)
        else:
            rows_out = jnp.concatenate([y[b, :n_valid] for b in range(bb)], axis=0)
            ys_ref[:, col0:col0 + y.shape[2]] = rows_out.astype(ys_ref.dtype)

    def seg(i):
        return load_wide(i * w, (i + 1) * w)

    def per_seq(fn, x):
        return jnp.stack([fn(x[b]) for b in range(bb)])

    def head_major(x):
        return jnp.stack([x[b][:, h * dh:(h + 1) * dh] for b in range(bb) for h in range(nh)])

    def token_major(x):
        return jnp.stack([jnp.concatenate([x[b * nh + h] for h in range(nh)], axis=1) for b in range(bb)])

    def cols(x, first):
        return jnp.stack([x[b][:, first + h:first + h + 1] for b in range(bb) for h in range(nh)])

    def rows(x, first):
        return jnp.stack([x[b][first + h:first + h + 1, :] for b in range(bb) for h in range(nh)])

    lane = lax.broadcasted_iota(jnp.int32, (1, c, SMALL_COLS), 2)
    rowi = lax.broadcasted_iota(jnp.int32, (1, c, SMALL_COLS), 1)
    pre = (small_proj if fused else load_rows(small_ref, 0, SMALL_COLS)) + bias_ref[...]
    is_f = (lane >= SM_F) & (lane < SM_B)
    is_b = (lane >= SM_B) & (lane < SM_A)
    is_a = (lane >= SM_A) & (lane < SM_A + nh)
    logg = -jnp.exp(alog_ref[...]) * _softplus(pre)
    gates = jnp.where(is_f, _log_sigmoid(pre), jnp.where(is_b, _sigmoid(pre), jnp.where(is_a, logg, pre)))
    gates = jnp.where(rowi < n_valid, gates, jnp.where(lane < SM_F, NEG, 0.0))
    tt = lax.broadcasted_iota(jnp.int32, (c, c), 0)
    ss = lax.broadcasted_iota(jnp.int32, (c, c), 1)
    incl = tt >= ss
    strict = tt > ss
    ltri = incl.astype(F32)
    utri = (ss >= tt).astype(F32)
    cum_src = jnp.where(is_f | is_a, gates, 0.0)
    cums = per_seq(lambda x: _dot_exact_lhs(ltri, x), cum_src)
    gates_t = per_seq(lambda x: x.T, gates)
    cums_t = per_seq(lambda x: _dot_exact_rhs(x.T, utri), cum_src)

    lbl = lbl_ref[...]
    sm = jnp.exp(lbl - jnp.max(lbl, axis=0, keepdims=True))
    sm = sm / jnp.sum(sm, axis=0, keepdims=True)
    lb = jnp.zeros((1, w), F32)
    for l in range(1, layer + 1):
        lb = lb + sm[l:l + 1, :]
    valid = lax.broadcasted_iota(jnp.int32, (1, c, w), 1) < n_valid
    hf = seg(SEG_HG_F)
    g_f = lb + (1.0 - lb) * _sigmoid(hf)
    k_hg = jnp.where(valid, (1.0 - lb) * _sigmoid(-hf), 0.0)
    f_log = jnp.where(valid, jnp.log(g_f), 0.0)
    bc = per_seq(lambda x: _dot_exact_lhs(ltri, x), f_log)

    convbuf[:, SUBLANES:SUBLANES + c, :] = load_wide(SEG_GD_Q * w, (SEG_GD_V + 1) * w)
    conv = convw_ref[CONV_W - 1:CONV_W, :] * convbuf[:, SUBLANES:SUBLANES + c, :]
    for jj in range(CONV_W - 1):
        off = SUBLANES - tail + jj
        conv = conv + convw_ref[jj:jj + 1, :] * convbuf[:, off:off + c, :]
    conv = _silu(conv)
    new_tail = convbuf[:, SUBLANES + n_valid - tail:SUBLANES + n_valid, :]
    convbuf[:, SUBLANES - tail:SUBLANES, :] = new_tail
    cq = head_major(conv[:, :, 0:w])
    ck = head_major(conv[:, :, w:2 * w])
    cv = head_major(conv[:, :, 2 * w:3 * w])
    qn = cq * lax.rsqrt(jnp.sum(cq * cq, axis=2, keepdims=True) + EPS) * (dh ** -0.5)
    kn = ck * lax.rsqrt(jnp.sum(ck * ck, axis=2, keepdims=True) + EPS)
    qk_kk = _heads(_dot_nt, jnp.concatenate([qn, kn], axis=1), kn)

    ml_q = head_major(seg(SEG_ML_Q))
    ml_k = head_major(seg(SEG_ML_K)) * (dh ** -0.5)
    c_old = c_ref[...].reshape(bb * nh, dh, dh)
    ml_qk = _heads(_dot_nt, ml_q, ml_k)
    ml_qc = _heads(_dot, ml_q, c_old)
    m_old = jnp.stack([m_ref[b][:, h:h + 1] for b in range(bb) for h in range(nh)])

    mlstm = _mlstm_chunk(
        ml_q, ml_k, head_major(seg(SEG_ML_V)), ml_qk, ml_qc,
        cols(gates, SM_I), cols(cums, SM_F), rows(gates_t, SM_I), rows(cums_t, SM_F), incl,
        c_old, n_ref[...].reshape(bb * nh, 1, dh), m_old)
    level_masks = _level_masks(c)
    gdn = _gdn_chunk(qn, kn, cv, qk_kk, cols(gates, SM_B), cols(cums, SM_A), rows(cums_t, SM_A),
                     incl, strict, level_masks, sg_ref[...].reshape(bb * nh, dh, dh))
    hgrn = _hgrn_chunk(head_major(_silu(seg(SEG_HG_Q))), head_major(k_hg), head_major(seg(SEG_HG_I)),
                       head_major(bc), level_masks, sh_ref[...].reshape(bb * nh, dh, dh))
    streams, per_round = (mlstm, gdn, hgrn), STAGES_PER_ROUND
    if fused:
        streams, per_round = streams + (gate_projection(),), per_round + (GATE_STAGES_PER_ROUND,)
    (hh, c_new, n_new, m_new), (o_gd, sg_new), (o_hg, sh_new) = _interleave(streams, per_round)[:3]

    c_ref[...] = c_new.reshape(c_ref.shape)
    n_ref[...] = n_new.reshape(n_ref.shape)
    m_lane = lax.broadcasted_iota(jnp.int32, (1, SMALL_COLS), 1)
    m_rows = []
    for b in range(bb):
        m_row = jnp.zeros((1, SMALL_COLS), F32)
        for h in range(nh):
            m_row = jnp.where(m_lane == h, m_new[b * nh + h], m_row)
        m_rows.append(m_row)
    m_ref[...] = jnp.stack(m_rows)
    y_ml = token_major(_rms(hh)) * mlnw_ref[...] * _sigmoid(seg(SEG_ML_O))
    store(0, y_ml)

    sg_ref[...] = sg_new.reshape(sg_ref.shape)
    y_gd = token_major(_rms(o_gd) * gdnw_ref[...]) * _silu(seg(SEG_GD_Z))
    store(w, y_gd)

    sh_ref[...] = sh_new.reshape(sh_ref.shape)
    y_hg = _rms(token_major(o_hg)) * hgnw_ref[...] * _silu(seg(SEG_HG_G))
    store(2 * w, y_hg)

    @pl.when(j == nj - 1)
    def _():
        cv_ref[...] = convbuf[:, SUBLANES - tail:SUBLANES, :]


def _mixers(big2, small2, params, states_in, states_prev, proj_inputs, *, b, t, layer, depth, chunk, bb):
    rows = min(t, chunk)
    assert t % rows == 0 and chunk % BF16_ROWS == 0 and CONV_W - 1 <= rows and b % bb == 0
    fused = proj_inputs is not None
    assert fused or big2.shape[0] == b * t
    w, dh, nh = MIX_WIDTH, HEAD_DIM, N_HEADS
    has_state = states_in is not None
    bias_row, alog_row, ml_norm_w, conv_w, gd_norm_w, lb_logits, hg_norm_w = params

    def const(shape):
        return pl.BlockSpec(shape, lambda i, j: (0,) * len(shape))

    def st(shape):
        return pl.BlockSpec((None, bb) + shape, lambda i, j: (layer, i) + (0,) * len(shape))

    flat = rows < chunk

    def view_shape(l):
        return (b * t, l) if flat else (b, t, l)

    def view(x):
        return x.reshape(view_shape(x.shape[-1]))

    def rows_spec(l):
        if flat:
            return pl.BlockSpec((bb * rows, l), lambda i, j: (i, 0))
        return pl.BlockSpec((bb, rows, l), lambda i, j: (i, j, 0))

    state_shapes = [(nh, dh, dh), (nh, 1, dh), (1, SMALL_COLS), (nh, dh, dh), (CONV_W - 1, N_BRANCH * w), (nh, dh, dh)]
    if fused:
        assert not flat
        x2d, norm_w, w_big, w_small = proj_inputs
        d = x2d.shape[1]
        assert [wg.shape[2] for wg in w_big[:2]] == [4 * w, 4 * w] and w_big[2].shape[2] > 4 * w
        resident = dict(pipeline_mode=pl.Buffered(1))
        in_specs = [rows_spec(d), pl.BlockSpec((None, 1, d), lambda i, j: (layer, 0, 0), **resident)]
        in_specs += [pl.BlockSpec((None, d, wg.shape[2]), lambda i, j: (layer, 0, 0), **resident)
                     for wg in (*w_big, w_small)]
        args = [view(x2d), norm_w, *w_big, w_small]
        n_gate = w_big[2].shape[2] - 4 * w
    else:
        in_specs = [rows_spec(N_MIX_SEGS * w), rows_spec(SMALL_COLS)]
        args = [view(big2), view(small2)]
    in_specs += [
        const((1, SMALL_COLS)), const((1, SMALL_COLS)), const((1, w)), const((CONV_W, N_BRANCH * w)),
        const((1, dh)), const((depth, w)), const((1, w)),
    ]
    args += [bias_row, alog_row, ml_norm_w, conv_w, gd_norm_w, lb_logits, hg_norm_w]
    if has_state:
        in_specs += [st(s) for s in state_shapes]
        args += list(states_in)
    aliases = {}
    if states_prev is not None:
        for k_out, arr in enumerate(states_prev):
            aliases[len(args)] = 1 + k_out
            in_specs.append(pl.BlockSpec(memory_space=pl.ANY))
            args.append(arr)
    out_specs = [rows_spec(N_BRANCH * w)] + [st(s) for s in state_shapes]
    out_shape = [jax.ShapeDtypeStruct(view_shape(N_BRANCH * w), BF16)] + [
        jax.ShapeDtypeStruct((depth, b) + s, F32) for s in state_shapes]
    if fused:
        out_specs.append(rows_spec(n_gate))
        out_shape.append(jax.ShapeDtypeStruct(view_shape(n_gate), F32))

    def body(*refs):
        n_in = len(args) - (len(states_prev) if states_prev is not None else 0)
        kept = refs[:n_in] + refs[len(args):]
        _mixers_kernel(*kept, layer=layer, chunk=chunk, n_valid=rows, bb=bb, has_state=has_state, fused=fused)

    outs = pl.pallas_call(
        body,
        grid=(b // bb, t // rows),
        in_specs=in_specs,
        out_specs=out_specs,
        out_shape=out_shape,
        scratch_shapes=[pltpu.VMEM((bb, SUBLANES + chunk, N_BRANCH * w), F32)],
        input_output_aliases=aliases,
        compiler_params=pltpu.CompilerParams(
            dimension_semantics=("parallel", "arbitrary"), vmem_limit_bytes=VMEM_LIMIT_BYTES),
        name="mixers",
    )(*args)
    gates = outs[-1].reshape(b * t, -1) if fused else None
    return outs[0].reshape(b * t, N_BRANCH * w), tuple(outs[1:1 + len(state_shapes)]), gates


def _merge_ffn_kernel(x_ref, ys_ref, gate_ref, wbr_ref, wout_ref, n2_ref, wup_ref, wdown_ref, fin_ref,
                      out_ref, *, last):
    w = MIX_WIDTH
    d = x_ref.shape[1]
    merged = jnp.zeros(x_ref.shape, F32)
    for n in range(N_BRANCH):
        br = jnp.dot(ys_ref[:, n * w:(n + 1) * w], wbr_ref[n], preferred_element_type=F32)
        merged = merged + _sigmoid(gate_ref[:, n * d:(n + 1) * d]) * br
    x1 = x_ref[...] + jnp.dot(merged.astype(BF16), wout_ref[...], preferred_element_type=F32)
    xn = (_rms(x1) * n2_ref[...]).astype(BF16)
    x2 = x1
    for f0 in range(0, wup_ref.shape[1], FFN_CHUNK):
        hid = jnp.square(jnp.maximum(
            jnp.dot(xn, wup_ref[:, f0:f0 + FFN_CHUNK], preferred_element_type=F32), 0.0))
        x2 = x2 + jnp.dot(hid.astype(BF16), wdown_ref[f0:f0 + FFN_CHUNK, :], preferred_element_type=F32)
    if last:
        x2 = _rms(x2) * fin_ref[...]
    out_ref[...] = x2


def _merge_ffn(x2d, ys2d, gate_src, w_branch, w_out, norm2_w, w_up, w_down, final_w, *, layer, tm, last):
    m, d = x2d.shape
    w = MIX_WIDTH
    d_ff = w_up.shape[2]
    gate_block = N_BRANCH * d
    assert m % tm == 0 and gate_src.shape[1] % gate_block == 0
    gate_idx = gate_src.shape[1] // gate_block - 1

    def per_layer(shape):
        return pl.BlockSpec((None,) + shape, lambda i: (layer,) + (0,) * len(shape), pipeline_mode=pl.Buffered(1))

    return pl.pallas_call(
        functools.partial(_merge_ffn_kernel, last=last),
        grid=(m // tm,),
        in_specs=[
            pl.BlockSpec((tm, d), lambda i: (i, 0)),
            pl.BlockSpec((tm, N_BRANCH * w), lambda i: (i, 0)),
            pl.BlockSpec((tm, gate_block), lambda i: (i, gate_idx)),
            per_layer((N_BRANCH, w, d)), per_layer((d, d)), per_layer((1, d)), per_layer((d, d_ff)),
            per_layer((d_ff, d)),
            pl.BlockSpec((1, d), lambda i: (0, 0), pipeline_mode=pl.Buffered(1)),
        ],
        out_specs=pl.BlockSpec((tm, d), lambda i: (i, 0)),
        out_shape=jax.ShapeDtypeStruct((m, d), F32),
        compiler_params=pltpu.CompilerParams(
            dimension_semantics=("parallel",), vmem_limit_bytes=VMEM_LIMIT_BYTES),
        name="merge_ffn",
    )(x2d, ys2d, gate_src, w_branch, w_out, norm2_w, w_up, w_down, final_w)


def _split_w_in(w_in):
    w, nh = MIX_WIDTH, N_HEADS
    a0, a1 = 4 * w, 4 * w + 2 * nh
    b0, b1 = a1 + 4 * w, a1 + 4 * w + 2 * nh
    w_big = tuple(part.astype(BF16) for part in (w_in[:, :, :a0], w_in[:, :, a1:b0], w_in[:, :, b1:]))
    small = jnp.concatenate([w_in[:, :, a0:a1], w_in[:, :, b0:b1]], axis=2)
    w_small = jnp.pad(small, ((0, 0), (0, 0), (0, SMALL_COLS - 4 * nh))).astype(BF16)
    return w_big, w_small


def _pad_lanes(parts, total):
    row = jnp.concatenate([p.reshape(1, -1).astype(F32) for p in parts], axis=1)
    return jnp.pad(row, ((0, 0), (0, total - row.shape[1])))


def _tiles(m):
    return math.gcd(m, 256), math.gcd(m, 512)


def _run_group(x3, states_in, weights, mix_params, seqs_per_step):
    b, t, d = x3.shape
    depth = len(mix_params)
    chunk = CHUNK if t % CHUNK == 0 else BF16_ROWS
    assert t % CHUNK == 0 or t <= BF16_ROWS
    x2d = x3.reshape(b * t, d)
    tm, tm_merge = _tiles(b * t)
    states_prev = None
    fuse_projection = t % CHUNK == 0
    for l in range(depth):
        big = small = proj_inputs = None
        if fuse_projection:
            proj_inputs = (x2d, weights["norm1_w"], weights["w_big"], weights["w_small"])
        else:
            big, small = _in_proj(x2d, weights["norm1_w"], weights["w_big"], weights["w_small"], layer=l, tm=tm)
        ys, states_prev, gates = _mixers(
            big, small, mix_params[l], states_in, states_prev, proj_inputs,
            b=b, t=t, layer=l, depth=depth, chunk=chunk, bb=math.gcd(b, seqs_per_step))
        x2d = _merge_ffn(x2d, ys, gates if fuse_projection else big, weights["w_branch"], weights["w_out"],
                         weights["norm2_w"], weights["w_up"], weights["w_down"], weights["final_norm_w"],
                         layer=l, tm=tm_merge, last=(l == depth - 1))
    return x2d.reshape(b, t, d), states_prev


def kernel(x_prompt, x_sample, state_mlstm_C, state_mlstm_n, state_mlstm_m, state_gdn_S, state_gdn_conv, state_hgrn_S, norm1_w, w_in, ml_i_bias, ml_f_bias, ml_norm_w, gd_conv_w, gd_A_log, gd_dt_bias, gd_norm_w, hg_lb_logits, hg_norm_w, w_branch, w_out, norm2_w, w_up, w_down, final_norm_w):
    depth, d = norm1_w.shape
    nh = N_HEADS
    w_big, w_small = _split_w_in(w_in)
    weights = dict(
        norm1_w=norm1_w.reshape(depth, 1, d), w_big=w_big, w_small=w_small,
        w_branch=w_branch.astype(BF16), w_out=w_out.astype(BF16), norm2_w=norm2_w.reshape(depth, 1, d),
        w_up=w_up.astype(BF16), w_down=w_down.astype(BF16), final_norm_w=final_norm_w.reshape(1, d))
    mix_params = []
    for l in range(depth):
        zeros_h = jnp.zeros((nh,), F32)
        bias_row = _pad_lanes([ml_i_bias[l], ml_f_bias[l], zeros_h, gd_dt_bias[l]], SMALL_COLS)
        alog_row = _pad_lanes([zeros_h, zeros_h, zeros_h, gd_A_log[l]], SMALL_COLS)
        mix_params.append((bias_row, alog_row, ml_norm_w[l].reshape(1, -1), gd_conv_w[l],
                           gd_norm_w[l].reshape(1, -1), hg_lb_logits, hg_norm_w[l].reshape(1, -1)))

    def unpack(states):
        s_c, s_n, s_m, s_g, s_cv, s_h = states
        return s_c, s_n.reshape(s_n.shape[:3] + s_n.shape[4:]), s_m[:, :, 0, :nh], s_g, s_cv, s_h

    y_prompt, p_states = _run_group(x_prompt, None, weights, mix_params, PROMPT_SEQS_PER_STEP)

    bs = x_sample.shape[0]
    m_in = jnp.pad(state_mlstm_m, ((0, 0), (0, 0), (0, SMALL_COLS - nh))).reshape(depth, bs, 1, SMALL_COLS)
    n_in = state_mlstm_n.reshape(depth, bs, nh, 1, HEAD_DIM)
    s_in = (state_mlstm_C, n_in, m_in, state_gdn_S, state_gdn_conv, state_hgrn_S)
    y_sample, s_states = _run_group(x_sample, s_in, weights, mix_params, SAMPLE_SEQS_PER_STEP)

    return (y_prompt, y_sample) + unpack(p_states) + unpack(s_states)
```
